```python
import math
import jax, jax.numpy as jnp
from jax import lax
import numpy as np

D_MODEL = 2048
BATCH = 4
SEQ = 4096
DEPTH = 1

MIX_WIDTH = D_MODEL
MLA_HEADS = 8
MLA_NOPE_DIM = 128
MLA_ROPE_DIM = 64
MLA_V_DIM = 128
MLA_QK_DIM = MLA_NOPE_DIM + MLA_ROPE_DIM
MLA_Q_RANK = D_MODEL // 4
MLA_KV_RANK = D_MODEL // 8
MLA_WIDTH = MLA_HEADS * MLA_V_DIM
ROPE_THETA = 10000.0
DIFF_WIDTH = MIX_WIDTH - MLA_WIDTH
DIFF_HEAD_DIM = 64
DIFF_HEADS = DIFF_WIDTH // (2 * DIFF_HEAD_DIM)

EPS = 1e-6
Q_BLOCK = 128

IN_SPLITS = [MLA_Q_RANK, MLA_KV_RANK, MLA_ROPE_DIM, MLA_WIDTH,
             DIFF_WIDTH, DIFF_WIDTH, DIFF_WIDTH, DIFF_WIDTH]
IN_WIDTH = sum(IN_SPLITS)
IN_SPLIT_IDX = [int(v) for v in np.cumsum(IN_SPLITS)[:-1]]

kernel_name = "hybrid_mla_diffattn_parallel_heads"


def rmsnorm(x, g):
    xf = x.astype(jnp.float32)
    y = xf * lax.rsqrt(jnp.mean(xf * xf, axis=-1, keepdims=True) + EPS)
    return (y * g.astype(jnp.float32)).astype(x.dtype)


def rope(t, pos):
    d = t.shape[-1]
    freqs = 1.0 / (ROPE_THETA ** (jnp.arange(0, d, 2, dtype=jnp.float32) / d))
    ang = pos.astype(jnp.float32)[:, None, :, None] * freqs
    cos, sin = jnp.cos(ang), jnp.sin(ang)
    tf = t.astype(jnp.float32)
    t1, t2 = tf[..., : d // 2], tf[..., d // 2:]
    return jnp.concatenate([t1 * cos - t2 * sin, t2 * cos + t1 * sin], axis=-1).astype(t.dtype)


def _to_blocks(t):
    b, h, s, d = t.shape
    return t.reshape(b, h, s // Q_BLOCK, Q_BLOCK, d).transpose(2, 0, 1, 3, 4)


def _from_blocks(t):
    nb, b, h, qb, d = t.shape
    return t.transpose(1, 0, 3, 2, 4).reshape(b, nb * qb, h, d)


def mla_attention(q, k, v):
    s_len = k.shape[2]
    scale = q.shape[-1] ** -0.5
    k_idx = jnp.arange(s_len)

    def step(args):
        qi, start = args
        s = jnp.einsum('bhqd,bhkd->bhqk', qi, k, preferred_element_type=jnp.float32) * scale
        causal = (start + jnp.arange(Q_BLOCK))[:, None] >= k_idx[None, :]
        s = jnp.where(causal, s, -jnp.inf)
        p = jax.nn.softmax(s, axis=-1)
        return jnp.einsum('bhqk,bhkd->bhqd', p.astype(v.dtype), v)

    starts = jnp.arange(s_len // Q_BLOCK) * Q_BLOCK
    return _from_blocks(lax.map(step, (_to_blocks(q), starts)))


def diff_attention(q1, q2, k1, k2, v, pos, lam):
    b, h, s_len, d = q1.shape
    scale = d ** -0.5
    k_idx = jnp.arange(s_len)
    slopes = 2.0 ** (-8.0 * (jnp.arange(h, dtype=jnp.float32) + 1.0) / h)
    pos_f = pos.astype(jnp.float32)
    pos_blocks = pos_f.reshape(b, s_len // Q_BLOCK, Q_BLOCK).transpose(1, 0, 2)

    def step(args):
        q1i, q2i, pi, start = args
        dist = jnp.abs(pi[:, None, :, None] - pos_f[:, None, None, :])
        bias = -slopes[None, :, None, None] * dist
        causal = (start + jnp.arange(Q_BLOCK))[:, None] >= k_idx[None, :]
        s1 = jnp.einsum('bhqd,bhkd->bhqk', q1i, k1, preferred_element_type=jnp.float32) * scale + bias
        s2 = jnp.einsum('bhqd,bhkd->bhqk', q2i, k2, preferred_element_type=jnp.float32) * scale + bias
        a = jax.nn.softmax(jnp.where(causal, s1, -jnp.inf), axis=-1) - lam * jax.nn.softmax(jnp.where(causal, s2, -jnp.inf), axis=-1)
        return jnp.einsum('bhqk,bhkd->bhqd', a.astype(v.dtype), v)

    starts = jnp.arange(s_len // Q_BLOCK) * Q_BLOCK
    return _from_blocks(lax.map(step, (_to_blocks(q1), _to_blocks(q2), pos_blocks, starts)))


def setup_inputs(seed: int = 0) -> dict:
    key = jax.random.key(seed)
    ks = jax.random.split(key, 16)
    f32 = jnp.float32

    def nrm(k, shape, fan_in):
        return jax.random.normal(k, shape, f32) * fan_in ** -0.5

    def gain(k, shape):
        return 1.0 + 0.05 * jax.random.normal(k, shape, f32)

    x = jax.random.normal(ks[0], (BATCH, SEQ, D_MODEL), f32)
    offs = jax.random.randint(ks[1], (BATCH, 1), 0, 1024, dtype=jnp.int32)
    positions = (jnp.arange(SEQ, dtype=jnp.int32)[None, :] + offs).astype(jnp.int32)
    return {
        "x": x,
        "positions": positions,
        "g_pre": gain(ks[2], (DEPTH, D_MODEL)),
        "w_in": nrm(ks[3], (DEPTH, D_MODEL, IN_WIDTH), D_MODEL),
        "g_q_a": gain(ks[4], (DEPTH, MLA_Q_RANK)),
        "w_q_b": nrm(ks[5], (DEPTH, MLA_Q_RANK, MLA_HEADS, MLA_QK_DIM), MLA_Q_RANK),
        "g_kv_a": gain(ks[6], (DEPTH, MLA_KV_RANK)),
        "w_kv_b": nrm(ks[7], (DEPTH, MLA_KV_RANK, MLA_HEADS, MLA_NOPE_DIM + MLA_V_DIM), MLA_KV_RANK),
        "lambda_q1": 0.1 * jax.random.normal(ks[8], (DEPTH, DIFF_HEAD_DIM), f32),
        "lambda_k1": 0.1 * jax.random.normal(ks[9], (DEPTH, DIFF_HEAD_DIM), f32),
        "lambda_q2": 0.1 * jax.random.normal(ks[10], (DEPTH, DIFF_HEAD_DIM), f32),
        "lambda_k2": 0.1 * jax.random.normal(ks[11], (DEPTH, DIFF_HEAD_DIM), f32),
        "g_diff_sub": gain(ks[12], (DEPTH, 2 * DIFF_HEAD_DIM)),
        "w_out": nrm(ks[13], (DEPTH, MIX_WIDTH, D_MODEL), MIX_WIDTH),
        "g_post": gain(ks[14], (DEPTH, D_MODEL)),
    }


def reference(x, positions, g_pre, w_in, g_q_a, w_q_b, g_kv_a, w_kv_b,
              lambda_q1, lambda_k1, lambda_q2, lambda_k2, g_diff_sub, w_out, g_post):
    b, s_len, _ = x.shape
    for l in range(DEPTH):
        h = rmsnorm(x, g_pre[l])
        proj = jnp.einsum('bsd,de->bse', h, w_in[l])
        q_lat, kv_lat, k_pe, gate_mla, dq, dk, dv, gate_diff = jnp.split(proj, IN_SPLIT_IDX, axis=-1)

        c_q = rmsnorm(q_lat, g_q_a[l])
        q = jnp.einsum('bsr,rhd->bhsd', c_q, w_q_b[l])
        q = jnp.concatenate([q[..., :MLA_NOPE_DIM], rope(q[..., MLA_NOPE_DIM:], positions)], axis=-1)
        c_kv = rmsnorm(kv_lat, g_kv_a[l])
        kv = jnp.einsum('bsr,rhd->bhsd', c_kv, w_kv_b[l])
        k_nope, v_mla = kv[..., :MLA_NOPE_DIM], kv[..., MLA_NOPE_DIM:]
        k_rot = rope(k_pe[:, None, :, :], positions)
        k = jnp.concatenate([k_nope, jnp.broadcast_to(k_rot, k_nope.shape[:-1] + (MLA_ROPE_DIM,))], axis=-1)
        o_mla = mla_attention(q, k, v_mla).reshape(b, s_len, MLA_WIDTH)

        dq = dq.reshape(b, s_len, DIFF_HEADS, 2, DIFF_HEAD_DIM).transpose(0, 2, 1, 3, 4)
        dk = dk.reshape(b, s_len, DIFF_HEADS, 2, DIFF_HEAD_DIM).transpose(0, 2, 1, 3, 4)
        dv = dv.reshape(b, s_len, DIFF_HEADS, 2 * DIFF_HEAD_DIM).transpose(0, 2, 1, 3)
        lam_init = 0.8 - 0.6 * math.exp(-0.3 * l)
        lam = (jnp.exp(jnp.sum(lambda_q1[l].astype(jnp.float32) * lambda_k1[l].astype(jnp.float32)))
               - jnp.exp(jnp.sum(lambda_q2[l].astype(jnp.float32) * lambda_k2[l].astype(jnp.float32)))
               + lam_init)
        o_diff = diff_attention(dq[..., 0, :], dq[..., 1, :], dk[..., 0, :], dk[..., 1, :], dv, positions, lam)
        o_diff = (rmsnorm(o_diff, g_diff_sub[l]) * (1.0 - lam_init)).reshape(b, s_len, DIFF_WIDTH)

        mixed = jnp.concatenate([o_mla * jax.nn.silu(gate_mla), o_diff * jax.nn.silu(gate_diff)], axis=-1)
        y = jnp.einsum('bse,ed->bsd', mixed, w_out[l])
        x = x + rmsnorm(y, g_post[l])
    return x
```

```python
import functools
import math

import jax
import jax.numpy as jnp
from jax import lax
from jax.experimental import pallas as pl
from jax.experimental.pallas import tpu as pltpu

F32 = jnp.float32
BF16 = jnp.bfloat16

EPS = 1e-6
LOG2E = 1.4426950408889634
ROPE_THETA = 10000.0

MLA_HEADS = 8
MLA_NOPE = 128
MLA_ROPE = 64
MLA_V = 128
MLA_QK = MLA_NOPE + MLA_ROPE
DIFF_HEADS = 8
DIFF_HEAD_DIM = 64
LANE = 128
QK_PAD = 256

VMEM_LIMIT = 48 * 1024 * 1024


def _rms(xf, g):
    ms = jnp.mean(xf * xf, axis=-1, keepdims=True)
    return xf * lax.rsqrt(ms + EPS) * g


def _dot(a, b):
    return jnp.dot(a, b, preferred_element_type=F32)


def _dot_nt(a, b):
    return lax.dot_general(a, b, (((1,), (1,)), ((), ())), preferred_element_type=F32)


def _inproj_body(x_ref, g_ref, w_ref, cs_ref, o_ref, h_ref):
    @pl.when(pl.program_id(1) == 0)
    def _():
        h_ref[...] = _rms(x_ref[...], g_ref[...]).astype(BF16)

    acc = _dot(h_ref[...], w_ref[...])
    o_ref[...] = (acc * cs_ref[...]).astype(BF16)


def _inproj(x2, g_pre, w_big, col_scale, tm, tn):
    m, d = x2.shape
    n = w_big.shape[1]
    return pl.pallas_call(
        _inproj_body,
        grid=(m // tm, n // tn),
        in_specs=[
            pl.BlockSpec((tm, d), lambda i, j: (i, 0)),
            pl.BlockSpec((1, d), lambda i, j: (0, 0)),
            pl.BlockSpec((d, tn), lambda i, j: (0, j)),
            pl.BlockSpec((1, tn), lambda i, j: (0, j)),
        ],
        out_specs=pl.BlockSpec((tm, tn), lambda i, j: (i, j)),
        out_shape=jax.ShapeDtypeStruct((m, n), BF16),
        scratch_shapes=[pltpu.VMEM((tm, d), BF16)],
        compiler_params=pltpu.CompilerParams(
            dimension_semantics=("arbitrary", "arbitrary"),
            vmem_limit_bytes=VMEM_LIMIT),
        name="inproj",
    )(x2, g_pre, w_big, col_scale)


def _latent_body(x_ref, pos_ref, freq_ref, gpre_ref, wlat_ref, gq_ref, wq_ref,
                 gkv_ref, wkv_ref, q_ref, k_ref, v_ref, *, q_scale):
    h = _rms(x_ref[...], gpre_ref[...]).astype(BF16)
    lat = _dot(h, wlat_ref[...])
    ang = pos_ref[...] * freq_ref[...]
    cos = jnp.cos(ang)
    sin = jnp.sin(ang)
    seg = lax.broadcasted_iota(jnp.int32, (1, LANE), 1) // (MLA_ROPE // 2)
    fq = jnp.where(seg == 1, -sin, jnp.where(seg == 3, sin, cos)) * q_scale
    sk = jnp.where(seg < 2, -sin, sin)
    k_ext = (lat[:, 768:896] * cos + lat[:, 896:1024] * sk).astype(BF16)

    c_q = _rms(lat[:, :512], gq_ref[...]).astype(BF16)
    qf = _dot(c_q, wq_ref[...])
    c_kv = _rms(lat[:, 512:768], gkv_ref[...]).astype(BF16)
    kvf = _dot(c_kv, wkv_ref[...])
    for hd in range(MLA_HEADS):
        o = hd * QK_PAD
        q_ref[:, o:o + LANE] = (qf[:, o:o + LANE] * q_scale).astype(BF16)
        q_ref[:, o + LANE:o + QK_PAD] = (qf[:, o + LANE:o + QK_PAD] * fq).astype(BF16)
        k_ref[:, o:o + LANE] = kvf[:, hd * LANE:(hd + 1) * LANE].astype(BF16)
        k_ref[:, o + LANE:o + QK_PAD] = k_ext
    v_ref[...] = kvf[:, MLA_HEADS * LANE:].astype(BF16)


def _latent(x2, pos_col, freq, g_pre, w_lat, g_q, w_q, g_kv, w_kv, tm, q_scale):
    m, d = x2.shape
    nq = MLA_HEADS * QK_PAD
    nv = MLA_HEADS * MLA_V
    full = lambda a: pl.BlockSpec(a.shape, lambda i: (0,) * a.ndim)
    return pl.pallas_call(
        functools.partial(_latent_body, q_scale=q_scale),
        grid=(m // tm,),
        in_specs=[
            pl.BlockSpec((tm, d), lambda i: (i, 0)),
            pl.BlockSpec((tm, 1), lambda i: (i, 0)),
            full(freq), full(g_pre), full(w_lat), full(g_q), full(w_q), full(g_kv), full(w_kv),
        ],
        out_specs=[
            pl.BlockSpec((tm, nq), lambda i: (i, 0)),
            pl.BlockSpec((tm, nq), lambda i: (i, 0)),
            pl.BlockSpec((tm, nv), lambda i: (i, 0)),
        ],
        out_shape=[
            jax.ShapeDtypeStruct((m, nq), BF16),
            jax.ShapeDtypeStruct((m, nq), BF16),
            jax.ShapeDtypeStruct((m, nv), BF16),
        ],
        compiler_params=pltpu.CompilerParams(
            dimension_semantics=("arbitrary",),
            vmem_limit_bytes=VMEM_LIMIT),
        name="latent",
    )(x2, pos_col, freq, g_pre, w_lat, g_q, w_q, g_kv, w_kv)


def _causal_mask(blk):
    row = lax.broadcasted_iota(jnp.int32, (blk, blk), 0)
    col = lax.broadcasted_iota(jnp.int32, (blk, blk), 1)
    return row >= col


def _online_update(s, v, m_ref, l_ref, acc_ref):
    m_prev = m_ref[...]
    m_new = jnp.maximum(m_prev, jnp.max(s, axis=-1, keepdims=True))
    alpha = jnp.exp2(m_prev - m_new)
    p = jnp.exp2(s - m_new)
    l_ref[...] = alpha * l_ref[...] + jnp.sum(p, axis=-1, keepdims=True)
    acc_ref[...] = alpha * acc_ref[...] + _dot(p.astype(BF16), v)
    m_ref[...] = m_new


def _mla_body(q_ref, k_ref, v_ref, o_ref, m_ref, l_ref, acc_ref, *, blk):
    i = pl.program_id(2)
    q = q_ref[...]
    m_ref[...] = jnp.full(m_ref.shape, -jnp.inf, F32)
    l_ref[...] = jnp.zeros(l_ref.shape, F32)
    acc_ref[...] = jnp.zeros(acc_ref.shape, F32)

    def step(j, masked):
        start = pl.multiple_of(j * blk, blk)
        k = k_ref[pl.ds(start, blk), :]
        v = v_ref[pl.ds(start, blk), :]
        s = _dot_nt(q, k)
        if masked:
            s = jnp.where(_causal_mask(blk), s, -jnp.inf)
        _online_update(s, v, m_ref, l_ref, acc_ref)

    def loop_body(j, carry):
        step(j, False)
        return carry

    lax.fori_loop(0, i, loop_body, 0)
    step(i, True)
    o_ref[...] = (acc_ref[...] / l_ref[...]).astype(BF16)


def _mla_attention(q, k, v, batch, seq, blk):
    nq = seq // blk
    return pl.pallas_call(
        functools.partial(_mla_body, blk=blk),
        grid=(batch, MLA_HEADS, nq),
        in_specs=[
            pl.BlockSpec((blk, QK_PAD), lambda b, h, i: (b * nq + i, h)),
            pl.BlockSpec((seq, QK_PAD), lambda b, h, i: (b, h)),
            pl.BlockSpec((seq, MLA_V), lambda b, h, i: (b, h)),
        ],
        out_specs=pl.BlockSpec((blk, MLA_V), lambda b, h, i: (b * nq + i, h)),
        out_shape=jax.ShapeDtypeStruct((batch * seq, MLA_HEADS * MLA_V), BF16),
        scratch_shapes=[
            pltpu.VMEM((blk, 1), F32),
            pltpu.VMEM((blk, 1), F32),
            pltpu.VMEM((blk, MLA_V), F32),
        ],
        compiler_params=pltpu.CompilerParams(
            dimension_semantics=("arbitrary", "arbitrary", "arbitrary"),
            vmem_limit_bytes=VMEM_LIMIT),
        name="mla_attn",
    )(q, k, v)


def _diff_body(slope_ref, q_ref, k_ref, v_ref, pc_ref, pr_ref, lq1_ref, lk1_ref, lq2_ref,
               lk2_ref, gsub_ref, o_ref, m1_ref, l1_ref, a1_ref, m2_ref, l2_ref, a2_ref,
               *, blk, lam_init):
    hd = pl.program_id(1)
    i = pl.program_id(2)
    q = q_ref[...]
    lane = lax.broadcasted_iota(jnp.int32, (1, 2 * DIFF_HEAD_DIM), 1)
    zero = jnp.zeros_like(q)
    q1 = jnp.where(lane < DIFF_HEAD_DIM, q, zero)
    q2 = jnp.where(lane >= DIFF_HEAD_DIM, q, zero)
    neg_slope = slope_ref[hd]
    pq = pc_ref[...]
    for m_ref, l_ref, a_ref in ((m1_ref, l1_ref, a1_ref), (m2_ref, l2_ref, a2_ref)):
        m_ref[...] = jnp.full(m_ref.shape, -jnp.inf, F32)
        l_ref[...] = jnp.zeros(l_ref.shape, F32)
        a_ref[...] = jnp.zeros(a_ref.shape, F32)

    def step(j, masked):
        start = pl.multiple_of(j * blk, blk)
        k = k_ref[pl.ds(start, blk), :]
        v = v_ref[pl.ds(start, blk), :]
        pk = pr_ref[0, :, pl.ds(start, blk)]
        bias = jnp.abs(pq - pk) * neg_slope
        s1 = _dot_nt(q1, k) + bias
        s2 = _dot_nt(q2, k) + bias
        if masked:
            keep = _causal_mask(blk)
            s1 = jnp.where(keep, s1, -jnp.inf)
            s2 = jnp.where(keep, s2, -jnp.inf)
        _online_update(s1, v, m1_ref, l1_ref, a1_ref)
        _online_update(s2, v, m2_ref, l2_ref, a2_ref)

    def loop_body(j, carry):
        step(j, False)
        return carry

    lax.fori_loop(0, i, loop_body, 0)
    step(i, True)

    lam = (jnp.exp(jnp.sum(lq1_ref[...] * lk1_ref[...], axis=-1, keepdims=True))
           - jnp.exp(jnp.sum(lq2_ref[...] * lk2_ref[...], axis=-1, keepdims=True))
           + lam_init)
    o = a1_ref[...] / l1_ref[...] - lam * (a2_ref[...] / l2_ref[...])
    o_ref[...] = (_rms(o, gsub_ref[...]) * (1.0 - lam_init)).astype(BF16)


def _diff_attention(big, pos_col, pos_row, neg_slopes, lq1, lk1, lq2, lk2, g_sub,
                    batch, seq, blk, lam_init):
    nq = seq // blk
    width = DIFF_HEADS * 2 * DIFF_HEAD_DIM
    cb = width // LANE
    q_off, k_off, v_off = 1 * cb, 2 * cb, 3 * cb
    small = lambda a: pl.BlockSpec(a.shape, lambda b, h, i: (0,) * a.ndim)
    return pl.pallas_call(
        functools.partial(_diff_body, blk=blk, lam_init=lam_init),
        grid=(batch, DIFF_HEADS, nq),
        in_specs=[
            pl.BlockSpec(memory_space=pltpu.SMEM),
            pl.BlockSpec((blk, LANE), lambda b, h, i: (b * nq + i, q_off + h)),
            pl.BlockSpec((seq, LANE), lambda b, h, i: (b, k_off + h)),
            pl.BlockSpec((seq, LANE), lambda b, h, i: (b, v_off + h)),
            pl.BlockSpec((blk, 1), lambda b, h, i: (b * nq + i, 0)),
            pl.BlockSpec((1, 1, seq), lambda b, h, i: (b, 0, 0)),
            small(lq1), small(lk1), small(lq2), small(lk2), small(g_sub),
        ],
        out_specs=pl.BlockSpec((blk, LANE), lambda b, h, i: (b * nq + i, h)),
        out_shape=jax.ShapeDtypeStruct((batch * seq, width), BF16),
        scratch_shapes=[
            pltpu.VMEM((blk, 1), F32), pltpu.VMEM((blk, 1), F32), pltpu.VMEM((blk, LANE), F32),
            pltpu.VMEM((blk, 1), F32), pltpu.VMEM((blk, 1), F32), pltpu.VMEM((blk, LANE), F32),
        ],
        compiler_params=pltpu.CompilerParams(
            dimension_semantics=("arbitrary", "arbitrary", "arbitrary"),
            vmem_limit_bytes=VMEM_LIMIT),
        name="diff_attn",
    )(neg_slopes, big, big, big, pos_col, pos_row, lq1, lk1, lq2, lk2, g_sub)


def _outproj_body(x_ref, om_ref, od_ref, gm_ref, gd_ref, w_ref, gpost_ref, o_ref):
    gm = gm_ref[...].astype(F32)
    gd = gd_ref[...].astype(F32)
    mm = (om_ref[...].astype(F32) * (gm * jax.nn.sigmoid(gm))).astype(BF16)
    md = (od_ref[...].astype(F32) * (gd * jax.nn.sigmoid(gd))).astype(BF16)
    half = mm.shape[1]
    y = _dot(mm, w_ref[:half, :]) + _dot(md, w_ref[half:, :])
    o_ref[...] = x_ref[...] + _rms(y, gpost_ref[...])


def _outproj(x2, o_mla, o_diff, big, w_out, g_post, tm):
    m, d = x2.shape
    half = o_mla.shape[1]
    gate_diff_blk = (big.shape[1] - half) // half
    return pl.pallas_call(
        _outproj_body,
        grid=(m // tm,),
        in_specs=[
            pl.BlockSpec((tm, d), lambda i: (i, 0)),
            pl.BlockSpec((tm, half), lambda i: (i, 0)),
            pl.BlockSpec((tm, half), lambda i: (i, 0)),
            pl.BlockSpec((tm, half), lambda i: (i, 0)),
            pl.BlockSpec((tm, half), lambda i: (i, gate_diff_blk)),
            pl.BlockSpec(w_out.shape, lambda i: (0, 0)),
            pl.BlockSpec((1, d), lambda i: (0, 0)),
        ],
        out_specs=pl.BlockSpec((tm, d), lambda i: (i, 0)),
        out_shape=jax.ShapeDtypeStruct((m, d), F32),
        compiler_params=pltpu.CompilerParams(
            dimension_semantics=("arbitrary",),
            vmem_limit_bytes=VMEM_LIMIT),
        name="outproj",
    )(x2, o_mla, o_diff, big, big, w_out, g_post)


def kernel(x, positions, g_pre, w_in, g_q_a, w_q_b, g_kv_a, w_kv_b, lambda_q1, lambda_k1,
           lambda_q2, lambda_k2, g_diff_sub, w_out, g_post):
    batch, seq, d = x.shape
    depth = g_pre.shape[0]
    q_rank = w_q_b.shape[1]
    kv_rank = w_kv_b.shape[1]
    half_rope = MLA_ROPE // 2
    lat_end = q_rank + kv_rank
    pe_end = lat_end + MLA_ROPE

    pos_f = positions.astype(F32)
    pos_col = pos_f.reshape(batch * seq, 1)
    pos_row = pos_f.reshape(batch, 1, seq)
    freqs = 1.0 / (ROPE_THETA ** (jnp.arange(0, MLA_ROPE, 2, dtype=F32) / MLA_ROPE))
    freq = jnp.tile(freqs, 4).reshape(1, LANE)
    slopes = 2.0 ** (-8.0 * (jnp.arange(DIFF_HEADS, dtype=F32) + 1.0) / DIFF_HEADS)
    neg_slopes = -slopes * LOG2E

    mla_scale = MLA_QK ** -0.5 * LOG2E
    diff_scale = DIFF_HEAD_DIM ** -0.5 * LOG2E
    width = DIFF_HEADS * 2 * DIFF_HEAD_DIM
    n_big = w_in.shape[2] - pe_end
    col_scale = jnp.ones((1, n_big), F32).at[:, width:2 * width].set(diff_scale)

    x2 = x.reshape(batch * seq, d)
    for l in range(depth):
        lam_init = 0.8 - 0.6 * math.exp(-0.3 * l)
        wi = w_in[l]
        u1 = wi[:, lat_end:lat_end + half_rope]
        u2 = wi[:, lat_end + half_rope:pe_end]
        w_lat = jnp.concatenate([wi[:, :lat_end], u1, u1, u2, u2, u2, u2, u1, u1], axis=1).astype(BF16)
        w_big = wi[:, pe_end:].astype(BF16)
        wq = w_q_b[l]
        t1 = wq[:, :, MLA_NOPE:MLA_NOPE + half_rope]
        t2 = wq[:, :, MLA_NOPE + half_rope:]
        w_q = jnp.concatenate([wq[:, :, :MLA_NOPE], t1, t2, t2, t1], axis=-1)
        w_q = w_q.reshape(q_rank, MLA_HEADS * QK_PAD).astype(BF16)
        wkv = w_kv_b[l]
        w_kv = jnp.concatenate([wkv[:, :, :MLA_NOPE].reshape(kv_rank, -1),
                                wkv[:, :, MLA_NOPE:].reshape(kv_rank, -1)], axis=1).astype(BF16)
        gp = g_pre[l].reshape(1, d)

        big = _inproj(x2, gp, w_big, col_scale, tm=1024, tn=512)
        q, k, v = _latent(x2, pos_col, freq, gp, w_lat, g_q_a[l].reshape(1, -1), w_q,
                          g_kv_a[l].reshape(1, -1), w_kv, tm=512, q_scale=mla_scale)
        o_mla = _mla_attention(q, k, v, batch, seq, blk=512)
        o_diff = _diff_attention(
            big, pos_col, pos_row, neg_slopes,
            lambda_q1[l].reshape(1, -1), lambda_k1[l].reshape(1, -1),
            lambda_q2[l].reshape(1, -1), lambda_k2[l].reshape(1, -1),
            g_diff_sub[l].reshape(1, -1), batch, seq, blk=512, lam_init=lam_init)
        x2 = _outproj(x2, o_mla, o_diff, big, w_out[l].astype(BF16), g_post[l].reshape(1, d), tm=512)
    return x2.reshape(batch, seq, d)
```

```python
import functools
import math

import numpy as np
import jax
import jax.numpy as jnp
from jax import lax
from jax.experimental import pallas as pl
from jax.experimental.pallas import tpu as pltpu

F32 = jnp.float32
BF16 = jnp.bfloat16

EPS = 1e-6
LOG2E = 1.4426950408889634
ROPE_THETA = 10000.0

MLA_HEADS = 8
MLA_NOPE = 128
MLA_ROPE = 64
MLA_V = 128
MLA_QK = MLA_NOPE + MLA_ROPE
DIFF_HEADS = 8
DIFF_HEAD_DIM = 64
LANE = 128
QK_PAD = 256

VMEM_LIMIT = 48 * 1024 * 1024


def _rms(xf, g):
    ms = jnp.mean(xf * xf, axis=-1, keepdims=True)
    return xf * lax.rsqrt(ms + EPS) * g


def _dot(a, b):
    return jnp.dot(a, b, preferred_element_type=F32)


def _dot_nt(a, b):
    return lax.dot_general(a, b, (((1,), (1,)), ((), ())), preferred_element_type=F32)


def _inproj_body(x_ref, g_ref, w_ref, cs_ref, o_ref, h_ref):
    @pl.when(pl.program_id(1) == 0)
    def _():
        h_ref[...] = _rms(x_ref[...], g_ref[...]).astype(BF16)

    acc = _dot(h_ref[...], w_ref[...])
    o_ref[...] = (acc * cs_ref[...]).astype(BF16)


def _inproj(x2, g_pre, w_big, col_scale, tm, tn):
    m, d = x2.shape
    n = w_big.shape[1]
    return pl.pallas_call(
        _inproj_body,
        grid=(m // tm, n // tn),
        in_specs=[
            pl.BlockSpec((tm, d), lambda i, j: (i, 0)),
            pl.BlockSpec((1, d), lambda i, j: (0, 0)),
            pl.BlockSpec((d, tn), lambda i, j: (0, j)),
            pl.BlockSpec((1, tn), lambda i, j: (0, j)),
        ],
        out_specs=pl.BlockSpec((tm, tn), lambda i, j: (i, j)),
        out_shape=jax.ShapeDtypeStruct((m, n), BF16),
        scratch_shapes=[pltpu.VMEM((tm, d), BF16)],
        compiler_params=pltpu.CompilerParams(
            dimension_semantics=("arbitrary", "arbitrary"),
            vmem_limit_bytes=VMEM_LIMIT),
        name="inproj",
    )(x2, g_pre, w_big, col_scale)


def _latent_body(x_ref, pos_ref, freq_ref, gpre_ref, wlat_ref, gq_ref, wq_ref,
                 gkv_ref, wkv_ref, q_ref, k_ref, v_ref, *, q_scale):
    h = _rms(x_ref[...], gpre_ref[...]).astype(BF16)
    lat = _dot(h, wlat_ref[...])
    ang = pos_ref[...] * freq_ref[...]
    cos = jnp.cos(ang)
    sin = jnp.sin(ang)
    seg = lax.broadcasted_iota(jnp.int32, (1, LANE), 1) // (MLA_ROPE // 2)
    fq = jnp.where(seg == 1, -sin, jnp.where(seg == 3, sin, cos)) * q_scale
    sk = jnp.where(seg < 2, -sin, sin)
    k_ext = (lat[:, 768:896] * cos + lat[:, 896:1024] * sk).astype(BF16)

    c_q = _rms(lat[:, :512], gq_ref[...]).astype(BF16)
    qf = _dot(c_q, wq_ref[...])
    c_kv = _rms(lat[:, 512:768], gkv_ref[...]).astype(BF16)
    kvf = _dot(c_kv, wkv_ref[...])
    for hd in range(MLA_HEADS):
        o = hd * QK_PAD
        q_ref[:, o:o + LANE] = (qf[:, o:o + LANE] * q_scale).astype(BF16)
        q_ref[:, o + LANE:o + QK_PAD] = (qf[:, o + LANE:o + QK_PAD] * fq).astype(BF16)
        k_ref[:, o:o + LANE] = kvf[:, hd * LANE:(hd + 1) * LANE].astype(BF16)
        k_ref[:, o + LANE:o + QK_PAD] = k_ext
    v_ref[...] = kvf[:, MLA_HEADS * LANE:].astype(BF16)


def _latent(x2, pos_col, freq, g_pre, w_lat, g_q, w_q, g_kv, w_kv, tm, q_scale):
    m, d = x2.shape
    nq = MLA_HEADS * QK_PAD
    nv = MLA_HEADS * MLA_V
    full = lambda a: pl.BlockSpec(a.shape, lambda i: (0,) * a.ndim)
    return pl.pallas_call(
        functools.partial(_latent_body, q_scale=q_scale),
        grid=(m // tm,),
        in_specs=[
            pl.BlockSpec((tm, d), lambda i: (i, 0)),
            pl.BlockSpec((tm, 1), lambda i: (i, 0)),
            full(freq), full(g_pre), full(w_lat), full(g_q), full(w_q), full(g_kv), full(w_kv),
        ],
        out_specs=[
            pl.BlockSpec((tm, nq), lambda i: (i, 0)),
            pl.BlockSpec((tm, nq), lambda i: (i, 0)),
            pl.BlockSpec((tm, nv), lambda i: (i, 0)),
        ],
        out_shape=[
            jax.ShapeDtypeStruct((m, nq), BF16),
            jax.ShapeDtypeStruct((m, nq), BF16),
            jax.ShapeDtypeStruct((m, nv), BF16),
        ],
        compiler_params=pltpu.CompilerParams(
            dimension_semantics=("arbitrary",),
            vmem_limit_bytes=VMEM_LIMIT),
        name="latent",
    )(x2, pos_col, freq, g_pre, w_lat, g_q, w_q, g_kv, w_kv)


def _blk_slice(idx, blk):
    if isinstance(idx, int):
        return slice(idx * blk, (idx + 1) * blk)
    return pl.ds(pl.multiple_of(idx * blk, blk), blk)


def _offdiag_pairs(nblk):
    return [(i, t) for i in range(1, nblk) for t in range(i)]


def _pair_table(nblk):
    return jnp.asarray(np.array(_offdiag_pairs(nblk), dtype=np.int32).T)


def _init_mask_bias(mb_ref, blk):
    kv = lax.broadcasted_iota(jnp.int32, (blk, blk), 0)
    qi = lax.broadcasted_iota(jnp.int32, (blk, blk), 1)
    mb_ref[...] = jnp.where(kv <= qi, 0.0, -jnp.inf).astype(F32)


def _transpose_v(v_ref, vt_ref, blk, nblk):
    for c in range(nblk):
        sl = _blk_slice(c, blk)
        vt_ref[:, sl] = v_ref[sl, :].astype(F32).T.astype(BF16)


def _run_pipeline(stage_a, stage_b_diag, stage_c_diag, stage_b, stage_c, tab_ref, nblk):
    pairs = _offdiag_pairs(nblk)
    npairs = len(pairs)
    stage_a(0, 0)
    stage_b_diag(0)
    stage_a(1, 1)

    def diag_loop(f, carry):
        stage_c_diag(f - 1)
        stage_b_diag(f)
        stage_a(f + 1, f + 1)
        return carry

    lax.fori_loop(1, nblk - 1, diag_loop, 0)

    stage_c_diag(nblk - 2)
    stage_b_diag(nblk - 1)
    stage_a(*pairs[0])
    stage_c_diag(nblk - 1)
    stage_b(*pairs[0])
    stage_a(*pairs[1])

    def off_loop(f, carry):
        stage_c(tab_ref[0, f - 1], tab_ref[1, f - 1])
        stage_b(tab_ref[0, f], tab_ref[1, f])
        stage_a(tab_ref[0, f + 1], tab_ref[1, f + 1])
        return carry

    lax.fori_loop(1, npairs - 1, off_loop, 0)

    stage_c(*pairs[npairs - 2])
    stage_b(*pairs[npairs - 1])
    stage_c(*pairs[npairs - 1])


def _mla_body(tab_ref, q_ref, k_ref, v_ref, o_ref, vt_ref, mb_ref, s_ref, p_ref, al_ref,
              m_ref, l_ref, acc_ref, *, blk, nblk):
    _init_mask_bias(mb_ref, blk)
    _transpose_v(v_ref, vt_ref, blk, nblk)

    def stage_a(i, t):
        s_ref[...] = _dot_nt(k_ref[_blk_slice(t, blk), :], q_ref[_blk_slice(i, blk), :])

    def stage_b_diag(i):
        s = s_ref[...] + mb_ref[...]
        m = jnp.max(s, axis=0, keepdims=True)
        p = jnp.exp2(s - m)
        m_ref[i] = m
        l_ref[i] = jnp.sum(p, axis=0, keepdims=True)
        p_ref[...] = p.astype(BF16)

    def stage_c_diag(i):
        acc_ref[i] = _dot(vt_ref[:, _blk_slice(i, blk)], p_ref[...])

    def stage_b(i, t):
        s = s_ref[...]
        m_prev = m_ref[i]
        m_new = jnp.maximum(m_prev, jnp.max(s, axis=0, keepdims=True))
        alpha = jnp.exp2(m_prev - m_new)
        p = jnp.exp2(s - m_new)
        l_ref[i] = alpha * l_ref[i] + jnp.sum(p, axis=0, keepdims=True)
        m_ref[i] = m_new
        al_ref[...] = alpha
        p_ref[...] = p.astype(BF16)

    def stage_c(i, t):
        acc_ref[i] = al_ref[...] * acc_ref[i] + _dot(vt_ref[:, _blk_slice(t, blk)], p_ref[...])

    _run_pipeline(stage_a, stage_b_diag, stage_c_diag, stage_b, stage_c, tab_ref, nblk)

    for i in range(nblk):
        o_t = acc_ref[i] * (1.0 / l_ref[i])
        o_ref[_blk_slice(i, blk), :] = o_t.T.astype(BF16)


def _mla_attention(q, k, v, batch, seq, blk):
    nblk = seq // blk
    assert nblk >= 3
    return pl.pallas_call(
        functools.partial(_mla_body, blk=blk, nblk=nblk),
        grid=(batch, MLA_HEADS),
        in_specs=[
            pl.BlockSpec(memory_space=pltpu.SMEM),
            pl.BlockSpec((seq, QK_PAD), lambda b, h: (b, h)),
            pl.BlockSpec((seq, QK_PAD), lambda b, h: (b, h)),
            pl.BlockSpec((seq, MLA_V), lambda b, h: (b, h)),
        ],
        out_specs=pl.BlockSpec((seq, MLA_V), lambda b, h: (b, h)),
        out_shape=jax.ShapeDtypeStruct((batch * seq, MLA_HEADS * MLA_V), BF16),
        scratch_shapes=[
            pltpu.VMEM((MLA_V, seq), BF16),
            pltpu.VMEM((blk, blk), F32),
            pltpu.VMEM((blk, blk), F32),
            pltpu.VMEM((blk, blk), BF16),
            pltpu.VMEM((1, blk), F32),
            pltpu.VMEM((nblk, 1, blk), F32),
            pltpu.VMEM((nblk, 1, blk), F32),
            pltpu.VMEM((nblk, MLA_V, blk), F32),
        ],
        compiler_params=pltpu.CompilerParams(
            dimension_semantics=("arbitrary", "arbitrary"),
            vmem_limit_bytes=VMEM_LIMIT),
        name="mla_attn",
    )(_pair_table(nblk), q, k, v)


def _diff_body(tab_ref, slope_ref, q_ref, k_ref, v_ref, pc_ref, pr_ref, lq1_ref, lk1_ref, lq2_ref,
               lk2_ref, gsub_ref, o_ref, vt_ref, mb_ref, qm_ref, cpk_ref, s_ref, p_ref, al_ref,
               m_ref, l_ref, acc_ref, *, blk, nblk, lam_init):
    c = slope_ref[pl.program_id(1)]
    _init_mask_bias(mb_ref, blk)
    _transpose_v(v_ref, vt_ref, blk, nblk)
    q = q_ref[...]
    lane = lax.broadcasted_iota(jnp.int32, (1, 2 * DIFF_HEAD_DIM), 1)
    zero = jnp.zeros_like(q)
    qm_ref[0] = jnp.where(lane < DIFF_HEAD_DIM, q, zero)
    qm_ref[1] = jnp.where(lane >= DIFF_HEAD_DIM, q, zero)
    cpk_ref[...] = jnp.broadcast_to(pc_ref[...] * c, cpk_ref.shape)

    def alibi(i, t):
        cpk = cpk_ref[_blk_slice(t, blk), :]
        cpq = pr_ref[0, :, _blk_slice(i, blk)] * c
        return jnp.abs(jnp.concatenate([cpk] * (blk // LANE), axis=1) - cpq)

    def stage_a(i, t):
        k = k_ref[_blk_slice(t, blk), :]
        for x in range(2):
            s_ref[x] = _dot_nt(k, qm_ref[x, _blk_slice(i, blk), :])

    def stage_b_diag(i):
        nb = mb_ref[...] - alibi(i, i)
        for x in range(2):
            s = s_ref[x] + nb
            m = jnp.max(s, axis=0, keepdims=True)
            p = jnp.exp2(s - m)
            m_ref[x, i] = m
            l_ref[x, i] = jnp.sum(p, axis=0, keepdims=True)
            p_ref[x] = p.astype(BF16)

    def stage_c_diag(i):
        vt = vt_ref[:, _blk_slice(i, blk)]
        for x in range(2):
            acc_ref[x, i] = _dot(vt, p_ref[x])

    def stage_b(i, t):
        bias = alibi(i, t)
        for x in range(2):
            s = s_ref[x] - bias
            m_prev = m_ref[x, i]
            m_new = jnp.maximum(m_prev, jnp.max(s, axis=0, keepdims=True))
            alpha = jnp.exp2(m_prev - m_new)
            p = jnp.exp2(s - m_new)
            l_ref[x, i] = alpha * l_ref[x, i] + jnp.sum(p, axis=0, keepdims=True)
            m_ref[x, i] = m_new
            al_ref[x] = alpha
            p_ref[x] = p.astype(BF16)

    def stage_c(i, t):
        vt = vt_ref[:, _blk_slice(t, blk)]
        for x in range(2):
            acc_ref[x, i] = al_ref[x] * acc_ref[x, i] + _dot(vt, p_ref[x])

    _run_pipeline(stage_a, stage_b_diag, stage_c_diag, stage_b, stage_c, tab_ref, nblk)

    lam = (jnp.exp(jnp.sum(lq1_ref[...] * lk1_ref[...], axis=-1, keepdims=True))
           - jnp.exp(jnp.sum(lq2_ref[...] * lk2_ref[...], axis=-1, keepdims=True))
           + lam_init)
    for i in range(nblk):
        o_t = (acc_ref[0, i] * (1.0 / l_ref[0, i])
               - lam * (acc_ref[1, i] * (1.0 / l_ref[1, i])))
        o = o_t.T
        o_ref[_blk_slice(i, blk), :] = (_rms(o, gsub_ref[...]) * (1.0 - lam_init)).astype(BF16)


def _diff_attention(big, pos_col, pos_row, slopes_l2e, lq1, lk1, lq2, lk2, g_sub,
                    batch, seq, blk, lam_init):
    nblk = seq // blk
    assert nblk >= 3
    width = DIFF_HEADS * 2 * DIFF_HEAD_DIM
    cb = width // LANE
    q_off, k_off, v_off = 1 * cb, 2 * cb, 3 * cb
    small = lambda a: pl.BlockSpec(a.shape, lambda b, h: (0,) * a.ndim)
    return pl.pallas_call(
        functools.partial(_diff_body, blk=blk, nblk=nblk, lam_init=lam_init),
        grid=(batch, DIFF_HEADS),
        in_specs=[
            pl.BlockSpec(memory_space=pltpu.SMEM),
            pl.BlockSpec(memory_space=pltpu.SMEM),
            pl.BlockSpec((seq, LANE), lambda b, h: (b, q_off + h)),
            pl.BlockSpec((seq, LANE), lambda b, h: (b, k_off + h)),
            pl.BlockSpec((seq, LANE), lambda b, h: (b, v_off + h)),
            pl.BlockSpec((seq, 1), lambda b, h: (b, 0)),
            pl.BlockSpec((1, 1, seq), lambda b, h: (b, 0, 0)),
            small(lq1), small(lk1), small(lq2), small(lk2), small(g_sub),
        ],
        out_specs=pl.BlockSpec((seq, LANE), lambda b, h: (b, h)),
        out_shape=jax.ShapeDtypeStruct((batch * seq, width), BF16),
        scratch_shapes=[
            pltpu.VMEM((LANE, seq), BF16),
            pltpu.VMEM((blk, blk), F32),
            pltpu.VMEM((2, seq, LANE), BF16),
            pltpu.VMEM((seq, LANE), F32),
            pltpu.VMEM((2, blk, blk), F32),
            pltpu.VMEM((2, blk, blk), BF16),
            pltpu.VMEM((2, 1, blk), F32),
            pltpu.VMEM((2, nblk, 1, blk), F32),
            pltpu.VMEM((2, nblk, 1, blk), F32),
            pltpu.VMEM((2, nblk, LANE, blk), F32),
        ],
        compiler_params=pltpu.CompilerParams(
            dimension_semantics=("arbitrary", "arbitrary"),
            vmem_limit_bytes=VMEM_LIMIT),
        name="diff_attn",
    )(_pair_table(nblk), slopes_l2e, big, big, big, pos_col, pos_row, lq1, lk1, lq2, lk2, g_sub)


def _outproj_body(x_ref, om_ref, od_ref, gm_ref, gd_ref, w_ref, gpost_ref, o_ref):
    gm = gm_ref[...].astype(F32)
    gd = gd_ref[...].astype(F32)
    mm = (om_ref[...].astype(F32) * (gm * jax.nn.sigmoid(gm))).astype(BF16)
    md = (od_ref[...].astype(F32) * (gd * jax.nn.sigmoid(gd))).astype(BF16)
    half = mm.shape[1]
    y = _dot(mm, w_ref[:half, :]) + _dot(md, w_ref[half:, :])
    o_ref[...] = x_ref[...] + _rms(y, gpost_ref[...])


def _outproj(x2, o_mla, o_diff, big, w_out, g_post, tm):
    m, d = x2.shape
    half = o_mla.shape[1]
    gate_diff_blk = (big.shape[1] - half) // half
    return pl.pallas_call(
        _outproj_body,
        grid=(m // tm,),
        in_specs=[
            pl.BlockSpec((tm, d), lambda i: (i, 0)),
            pl.BlockSpec((tm, half), lambda i: (i, 0)),
            pl.BlockSpec((tm, half), lambda i: (i, 0)),
            pl.BlockSpec((tm, half), lambda i: (i, 0)),
            pl.BlockSpec((tm, half), lambda i: (i, gate_diff_blk)),
            pl.BlockSpec(w_out.shape, lambda i: (0, 0)),
            pl.BlockSpec((1, d), lambda i: (0, 0)),
        ],
        out_specs=pl.BlockSpec((tm, d), lambda i: (i, 0)),
        out_shape=jax.ShapeDtypeStruct((m, d), F32),
        compiler_params=pltpu.CompilerParams(
            dimension_semantics=("arbitrary",),
            vmem_limit_bytes=VMEM_LIMIT),
        name="outproj",
    )(x2, o_mla, o_diff, big, big, w_out, g_post)


def kernel(x, positions, g_pre, w_in, g_q_a, w_q_b, g_kv_a, w_kv_b, lambda_q1, lambda_k1,
           lambda_q2, lambda_k2, g_diff_sub, w_out, g_post):
    batch, seq, d = x.shape
    depth = g_pre.shape[0]
    q_rank = w_q_b.shape[1]
    kv_rank = w_kv_b.shape[1]
    half_rope = MLA_ROPE // 2
    lat_end = q_rank + kv_rank
    pe_end = lat_end + MLA_ROPE

    pos_f = positions.astype(F32)
    pos_col = pos_f.reshape(batch * seq, 1)
    pos_row = pos_f.reshape(batch, 1, seq)
    freqs = 1.0 / (ROPE_THETA ** (jnp.arange(0, MLA_ROPE, 2, dtype=F32) / MLA_ROPE))
    freq = jnp.tile(freqs, 4).reshape(1, LANE)
    slopes = 2.0 ** (-8.0 * (jnp.arange(DIFF_HEADS, dtype=F32) + 1.0) / DIFF_HEADS)
    slopes_l2e = slopes * LOG2E

    mla_scale = MLA_QK ** -0.5 * LOG2E
    diff_scale = DIFF_HEAD_DIM ** -0.5 * LOG2E
    width = DIFF_HEADS * 2 * DIFF_HEAD_DIM
    n_big = w_in.shape[2] - pe_end
    col_scale = jnp.ones((1, n_big), F32).at[:, width:2 * width].set(diff_scale)

    x2 = x.reshape(batch * seq, d)
    for l in range(depth):
        lam_init = 0.8 - 0.6 * math.exp(-0.3 * l)
        wi = w_in[l]
        u1 = wi[:, lat_end:lat_end + half_rope]
        u2 = wi[:, lat_end + half_rope:pe_end]
        w_lat = jnp.concatenate([wi[:, :lat_end], u1, u1, u2, u2, u2, u2, u1, u1], axis=1).astype(BF16)
        w_big = wi[:, pe_end:].astype(BF16)
        wq = w_q_b[l]
        t1 = wq[:, :, MLA_NOPE:MLA_NOPE + half_rope]
        t2 = wq[:, :, MLA_NOPE + half_rope:]
        w_q = jnp.concatenate([wq[:, :, :MLA_NOPE], t1, t2, t2, t1], axis=-1)
        w_q = w_q.reshape(q_rank, MLA_HEADS * QK_PAD).astype(BF16)
        wkv = w_kv_b[l]
        w_kv = jnp.concatenate([wkv[:, :, :MLA_NOPE].reshape(kv_rank, -1),
                                wkv[:, :, MLA_NOPE:].reshape(kv_rank, -1)], axis=1).astype(BF16)
        gp = g_pre[l].reshape(1, d)

        big = _inproj(x2, gp, w_big, col_scale, tm=1024, tn=512)
        q, k, v = _latent(x2, pos_col, freq, gp, w_lat, g_q_a[l].reshape(1, -1), w_q,
                          g_kv_a[l].reshape(1, -1), w_kv, tm=512, q_scale=mla_scale)
        o_mla = _mla_attention(q, k, v, batch, seq, blk=512)
        o_diff = _diff_attention(
            big, pos_col, pos_row, slopes_l2e,
            lambda_q1[l].reshape(1, -1), lambda_k1[l].reshape(1, -1),
            lambda_q2[l].reshape(1, -1), lambda_k2[l].reshape(1, -1),
            g_diff_sub[l].reshape(1, -1), batch, seq, blk=512, lam_init=lam_init)
        x2 = _outproj(x2, o_mla, o_diff, big, w_out[l].astype(BF16), g_post[l].reshape(1, d), tm=512)
    return x2.reshape(batch, seq, d)
```

```python
import functools
import math

import numpy as np
import jax
import jax.numpy as jnp
from jax import lax
from jax.experimental import pallas as pl
from jax.experimental.pallas import tpu as pltpu

F32 = jnp.float32
BF16 = jnp.bfloat16

EPS = 1e-6
LOG2E = 1.4426950408889634
ROPE_THETA = 10000.0

MLA_HEADS = 8
MLA_NOPE = 128
MLA_ROPE = 64
MLA_V = 128
MLA_QK = MLA_NOPE + MLA_ROPE
DIFF_HEADS = 8
DIFF_HEAD_DIM = 64
LANE = 128
QK_PAD = 256
ALIBI_LANES = 9
ALIBI_BASE = (DIFF_HEAD_DIM, 0)

VMEM_LIMIT = 48 * 1024 * 1024


def _rms(xf, g):
    ms = jnp.mean(xf * xf, axis=-1, keepdims=True)
    return xf * lax.rsqrt(ms + EPS) * g


def _dot(a, b):
    return jnp.dot(a, b, preferred_element_type=F32)


def _dot_nt(a, b):
    return lax.dot_general(a, b, (((1,), (1,)), ((), ())), preferred_element_type=F32)


def _inproj_body(x_ref, g_ref, w_ref, cs_ref, o_ref, h_ref):
    @pl.when(pl.program_id(1) == 0)
    def _():
        h_ref[...] = _rms(x_ref[...], g_ref[...]).astype(BF16)

    acc = _dot(h_ref[...], w_ref[...])
    o_ref[...] = (acc * cs_ref[...]).astype(BF16)


def _inproj(x2, g_pre, w_big, col_scale, tm, tn):
    m, d = x2.shape
    n = w_big.shape[1]
    return pl.pallas_call(
        _inproj_body,
        grid=(m // tm, n // tn),
        in_specs=[
            pl.BlockSpec((tm, d), lambda i, j: (i, 0)),
            pl.BlockSpec((1, d), lambda i, j: (0, 0)),
            pl.BlockSpec((d, tn), lambda i, j: (0, j)),
            pl.BlockSpec((1, tn), lambda i, j: (0, j)),
        ],
        out_specs=pl.BlockSpec((tm, tn), lambda i, j: (i, j)),
        out_shape=jax.ShapeDtypeStruct((m, n), BF16),
        scratch_shapes=[pltpu.VMEM((tm, d), BF16)],
        compiler_params=pltpu.CompilerParams(
            dimension_semantics=("arbitrary", "arbitrary"),
            vmem_limit_bytes=VMEM_LIMIT),
        name="inproj",
    )(x2, g_pre, w_big, col_scale)


def _latent_body(x_ref, pos_ref, freq_ref, gpre_ref, wlat_ref, gq_ref, wq_ref,
                 gkv_ref, wkv_ref, q_ref, k_ref, v_ref, *, q_scale):
    h = _rms(x_ref[...], gpre_ref[...]).astype(BF16)
    lat = _dot(h, wlat_ref[...])
    ang = pos_ref[...] * freq_ref[...]
    cos = jnp.cos(ang)
    sin = jnp.sin(ang)
    seg = lax.broadcasted_iota(jnp.int32, (1, LANE), 1) // (MLA_ROPE // 2)
    fq = jnp.where(seg == 1, -sin, jnp.where(seg == 3, sin, cos)) * q_scale
    sk = jnp.where(seg < 2, -sin, sin)
    k_ext = (lat[:, 768:896] * cos + lat[:, 896:1024] * sk).astype(BF16)

    c_q = _rms(lat[:, :512], gq_ref[...]).astype(BF16)
    qf = _dot(c_q, wq_ref[...])
    c_kv = _rms(lat[:, 512:768], gkv_ref[...]).astype(BF16)
    kvf = _dot(c_kv, wkv_ref[...])
    for hd in range(MLA_HEADS):
        o = hd * QK_PAD
        q_ref[:, o:o + LANE] = (qf[:, o:o + LANE] * q_scale).astype(BF16)
        q_ref[:, o + LANE:o + QK_PAD] = (qf[:, o + LANE:o + QK_PAD] * fq).astype(BF16)
        k_ref[:, o:o + LANE] = kvf[:, hd * LANE:(hd + 1) * LANE].astype(BF16)
        k_ref[:, o + LANE:o + QK_PAD] = k_ext
    v_ref[...] = kvf[:, MLA_HEADS * LANE:].astype(BF16)


def _latent(x2, pos_col, freq, g_pre, w_lat, g_q, w_q, g_kv, w_kv, tm, q_scale):
    m, d = x2.shape
    nq = MLA_HEADS * QK_PAD
    nv = MLA_HEADS * MLA_V
    full = lambda a: pl.BlockSpec(a.shape, lambda i: (0,) * a.ndim)
    return pl.pallas_call(
        functools.partial(_latent_body, q_scale=q_scale),
        grid=(m // tm,),
        in_specs=[
            pl.BlockSpec((tm, d), lambda i: (i, 0)),
            pl.BlockSpec((tm, 1), lambda i: (i, 0)),
            full(freq), full(g_pre), full(w_lat), full(g_q), full(w_q), full(g_kv), full(w_kv),
        ],
        out_specs=[
            pl.BlockSpec((tm, nq), lambda i: (i, 0)),
            pl.BlockSpec((tm, nq), lambda i: (i, 0)),
            pl.BlockSpec((tm, nv), lambda i: (i, 0)),
        ],
        out_shape=[
            jax.ShapeDtypeStruct((m, nq), BF16),
            jax.ShapeDtypeStruct((m, nq), BF16),
            jax.ShapeDtypeStruct((m, nv), BF16),
        ],
        compiler_params=pltpu.CompilerParams(
            dimension_semantics=("arbitrary",),
            vmem_limit_bytes=VMEM_LIMIT),
        name="latent",
    )(x2, pos_col, freq, g_pre, w_lat, g_q, w_q, g_kv, w_kv)


def _blk_slice(idx, blk):
    if isinstance(idx, int):
        return slice(idx * blk, (idx + 1) * blk)
    return pl.ds(pl.multiple_of(idx * blk, blk), blk)


def _offdiag_pairs(nblk):
    return [(i, t) for i in range(1, nblk) for t in range(i)]


def _pair_table(nblk):
    return jnp.asarray(np.array(_offdiag_pairs(nblk), dtype=np.int32).T)


def _init_mask_bias(mb_ref, blk):
    kv = lax.broadcasted_iota(jnp.int32, (blk, blk), 0)
    qi = lax.broadcasted_iota(jnp.int32, (blk, blk), 1)
    mb_ref[...] = jnp.where(kv <= qi, 0.0, -jnp.inf).astype(F32)


def _transpose_v(v_ref, vt_ref, blk, nblk):
    for c in range(nblk):
        sl = _blk_slice(c, blk)
        vt_ref[:, sl] = v_ref[sl, :].astype(F32).T.astype(BF16)


def _run_pipeline(stage_a, stage_b_diag, stage_c_diag, stage_b, stage_c, tab_ref, nblk):
    pairs = _offdiag_pairs(nblk)
    npairs = len(pairs)
    stage_a(0, 0)
    stage_b_diag(0)
    stage_a(1, 1)

    def diag_loop(f, carry):
        stage_c_diag(f - 1)
        stage_b_diag(f)
        stage_a(f + 1, f + 1)
        return carry

    lax.fori_loop(1, nblk - 1, diag_loop, 0, unroll=2)

    stage_c_diag(nblk - 2)
    stage_b_diag(nblk - 1)
    stage_a(*pairs[0])
    stage_c_diag(nblk - 1)
    stage_b(*pairs[0])
    stage_a(*pairs[1])

    def off_loop(f, carry):
        stage_c(tab_ref[0, f - 1], tab_ref[1, f - 1])
        stage_b(tab_ref[0, f], tab_ref[1, f])
        stage_a(tab_ref[0, f + 1], tab_ref[1, f + 1])
        return carry

    lax.fori_loop(1, npairs - 1, off_loop, 0, unroll=2)

    stage_c(*pairs[npairs - 2])
    stage_b(*pairs[npairs - 1])
    stage_c(*pairs[npairs - 1])


def _mla_body(tab_ref, q_ref, k_ref, v_ref, o_ref, vt_ref, mb_ref, s_ref, p_ref, al_ref,
              m_ref, l_ref, acc_ref, *, blk, nblk):
    _init_mask_bias(mb_ref, blk)
    _transpose_v(v_ref, vt_ref, blk, nblk)

    def stage_a(i, t):
        s_ref[...] = _dot_nt(k_ref[_blk_slice(t, blk), :], q_ref[_blk_slice(i, blk), :])

    def stage_b_diag(i):
        s = s_ref[...] + mb_ref[...]
        m = jnp.max(s, axis=0, keepdims=True)
        p = jnp.exp2(s - m)
        m_ref[i] = m
        l_ref[i] = jnp.sum(p, axis=0, keepdims=True)
        p_ref[...] = p.astype(BF16)

    def stage_c_diag(i):
        acc_ref[i] = _dot(vt_ref[:, _blk_slice(i, blk)], p_ref[...])

    def stage_b(i, t):
        s = s_ref[...]
        m_prev = m_ref[i]
        m_new = jnp.maximum(m_prev, jnp.max(s, axis=0, keepdims=True))
        alpha = jnp.exp2(m_prev - m_new)
        p = jnp.exp2(s - m_new)
        l_ref[i] = alpha * l_ref[i] + jnp.sum(p, axis=0, keepdims=True)
        m_ref[i] = m_new
        al_ref[...] = alpha
        p_ref[...] = p.astype(BF16)

    def stage_c(i, t):
        acc_ref[i] = al_ref[...] * acc_ref[i] + _dot(vt_ref[:, _blk_slice(t, blk)], p_ref[...])

    _run_pipeline(stage_a, stage_b_diag, stage_c_diag, stage_b, stage_c, tab_ref, nblk)

    for i in range(nblk):
        o_t = acc_ref[i] * (1.0 / l_ref[i])
        o_ref[_blk_slice(i, blk), :] = o_t.T.astype(BF16)


def _mla_attention(q, k, v, batch, seq, blk):
    nblk = seq // blk
    assert nblk >= 3
    return pl.pallas_call(
        functools.partial(_mla_body, blk=blk, nblk=nblk),
        grid=(batch, MLA_HEADS),
        in_specs=[
            pl.BlockSpec(memory_space=pltpu.SMEM),
            pl.BlockSpec((seq, QK_PAD), lambda b, h: (b, h)),
            pl.BlockSpec((seq, QK_PAD), lambda b, h: (b, h)),
            pl.BlockSpec((seq, MLA_V), lambda b, h: (b, h)),
        ],
        out_specs=pl.BlockSpec((seq, MLA_V), lambda b, h: (b, h)),
        out_shape=jax.ShapeDtypeStruct((batch * seq, MLA_HEADS * MLA_V), BF16),
        scratch_shapes=[
            pltpu.VMEM((MLA_V, seq), BF16),
            pltpu.VMEM((blk, blk), F32),
            pltpu.VMEM((blk, blk), F32),
            pltpu.VMEM((blk, blk), BF16),
            pltpu.VMEM((1, blk), F32),
            pltpu.VMEM((nblk, 1, blk), F32),
            pltpu.VMEM((nblk, 1, blk), F32),
            pltpu.VMEM((nblk, MLA_V, blk), F32),
        ],
        compiler_params=pltpu.CompilerParams(
            dimension_semantics=("arbitrary", "arbitrary"),
            vmem_limit_bytes=VMEM_LIMIT),
        name="mla_attn",
    )(_pair_table(nblk), q, k, v)


def _alibi_lane_tables(slopes_l2e):
    c1 = slopes_l2e.astype(BF16).astype(F32)
    c2 = (slopes_l2e - c1).astype(BF16).astype(F32)
    c3 = (slopes_l2e - c1 - c2).astype(BF16).astype(F32)
    cw = (jnp.stack([c1, c2, c3], axis=1)[:, :, None]
          * jnp.asarray([1.0, 256.0, 65536.0], F32)[None, None, :]).reshape(-1, ALIBI_LANES)
    tab = jnp.zeros((slopes_l2e.shape[0], 4, LANE), F32)
    for x, base in enumerate(ALIBI_BASE):
        tab = tab.at[:, x, base + ALIBI_LANES:base + 2 * ALIBI_LANES].set(cw)
        tab = tab.at[:, 2 + x, base:base + ALIBI_LANES].set(-cw)
    return tab.astype(BF16)


def _diff_body(tab_ref, q_ref, k_ref, v_ref, pc_ref, sh_ref, lt_ref, lq1_ref, lk1_ref, lq2_ref,
               lk2_ref, gsub_ref, o_ref, vt_ref, mb_ref, dig_ref, qm_ref, km_ref, s_ref, p_ref, al_ref,
               m_ref, l_ref, acc_ref, *, blk, nblk, lam_init):
    _init_mask_bias(mb_ref, blk)
    _transpose_v(v_ref, vt_ref, blk, nblk)

    @pl.when(pl.program_id(1) == 0)
    def _():
        pcol = pc_ref[...]
        prel = jnp.broadcast_to(pcol - pcol[0:1, :], dig_ref.shape)
        shift = jnp.broadcast_to(sh_ref[...], dig_ref.shape)
        dig_ref[...] = (lax.shift_right_logical(prel, shift) & 255).astype(F32).astype(BF16)

    lane = lax.broadcasted_iota(jnp.int32, (1, LANE), 1)
    dig = dig_ref[...]
    q = q_ref[...]
    k = k_ref[...]
    for x, base in enumerate(ALIBI_BASE):
        own = (lane < DIFF_HEAD_DIM) if x == 0 else (lane >= DIFF_HEAD_DIM)
        q_digit = (lane >= base) & (lane < base + ALIBI_LANES)
        k_digit = (lane >= base + ALIBI_LANES) & (lane < base + 2 * ALIBI_LANES)
        qm_ref[x] = jnp.where(own, q, jnp.where(q_digit, dig, lt_ref[0, x:x + 1, :]))
        km_ref[x] = jnp.where(own, k, jnp.where(k_digit, dig, lt_ref[0, 2 + x:3 + x, :]))

    def stage_a(i, t):
        for x in range(2):
            s_ref[x] = _dot_nt(km_ref[x, _blk_slice(t, blk), :], qm_ref[x, _blk_slice(i, blk), :])

    def stage_b_diag(i):
        nb = mb_ref[...]
        for x in range(2):
            s = s_ref[x] + nb
            m = jnp.max(s, axis=0, keepdims=True)
            p = jnp.exp2(s - m)
            m_ref[x, i] = m
            l_ref[x, i] = jnp.sum(p, axis=0, keepdims=True)
            p_ref[x] = p.astype(BF16)

    def stage_c_diag(i):
        vt = vt_ref[:, _blk_slice(i, blk)]
        for x in range(2):
            acc_ref[x, i] = _dot(vt, p_ref[x])

    def stage_b(i, t):
        for x in range(2):
            s = s_ref[x]
            m_prev = m_ref[x, i]
            m_new = jnp.maximum(m_prev, jnp.max(s, axis=0, keepdims=True))
            alpha = jnp.exp2(m_prev - m_new)
            p = jnp.exp2(s - m_new)
            l_ref[x, i] = alpha * l_ref[x, i] + jnp.sum(p, axis=0, keepdims=True)
            m_ref[x, i] = m_new
            al_ref[x] = alpha
            p_ref[x] = p.astype(BF16)

    def stage_c(i, t):
        vt = vt_ref[:, _blk_slice(t, blk)]
        for x in range(2):
            acc_ref[x, i] = al_ref[x] * acc_ref[x, i] + _dot(vt, p_ref[x])

    _run_pipeline(stage_a, stage_b_diag, stage_c_diag, stage_b, stage_c, tab_ref, nblk)

    lam = (jnp.exp(jnp.sum(lq1_ref[...] * lk1_ref[...], axis=-1, keepdims=True))
           - jnp.exp(jnp.sum(lq2_ref[...] * lk2_ref[...], axis=-1, keepdims=True))
           + lam_init)
    for i in range(nblk):
        o_t = (acc_ref[0, i] * (1.0 / l_ref[0, i])
               - lam * (acc_ref[1, i] * (1.0 / l_ref[1, i])))
        o = o_t.T
        o_ref[_blk_slice(i, blk), :] = (_rms(o, gsub_ref[...]) * (1.0 - lam_init)).astype(BF16)


def _diff_attention(big, pos_icol, slopes_l2e, lq1, lk1, lq2, lk2, g_sub,
                    batch, seq, blk, lam_init):
    nblk = seq // blk
    assert nblk >= 3
    width = DIFF_HEADS * 2 * DIFF_HEAD_DIM
    cb = width // LANE
    q_off, k_off, v_off = 1 * cb, 2 * cb, 3 * cb
    lane_tab = _alibi_lane_tables(slopes_l2e)
    digit_shift = jnp.asarray(((np.arange(LANE) % DIFF_HEAD_DIM) % 3 * 8).reshape(1, LANE), jnp.int32)
    small = lambda a: pl.BlockSpec(a.shape, lambda b, h: (0,) * a.ndim)
    return pl.pallas_call(
        functools.partial(_diff_body, blk=blk, nblk=nblk, lam_init=lam_init),
        grid=(batch, DIFF_HEADS),
        in_specs=[
            pl.BlockSpec(memory_space=pltpu.SMEM),
            pl.BlockSpec((seq, LANE), lambda b, h: (b, q_off + h)),
            pl.BlockSpec((seq, LANE), lambda b, h: (b, k_off + h)),
            pl.BlockSpec((seq, LANE), lambda b, h: (b, v_off + h)),
            pl.BlockSpec((seq, 1), lambda b, h: (b, 0)),
            small(digit_shift),
            pl.BlockSpec((1, 4, LANE), lambda b, h: (h, 0, 0)),
            small(lq1), small(lk1), small(lq2), small(lk2), small(g_sub),
        ],
        out_specs=pl.BlockSpec((seq, LANE), lambda b, h: (b, h)),
        out_shape=jax.ShapeDtypeStruct((batch * seq, width), BF16),
        scratch_shapes=[
            pltpu.VMEM((LANE, seq), BF16),
            pltpu.VMEM((blk, blk), F32),
            pltpu.VMEM((seq, LANE), BF16),
            pltpu.VMEM((2, seq, LANE), BF16),
            pltpu.VMEM((2, seq, LANE), BF16),
            pltpu.VMEM((2, blk, blk), F32),
            pltpu.VMEM((2, blk, blk), BF16),
            pltpu.VMEM((2, 1, blk), F32),
            pltpu.VMEM((2, nblk, 1, blk), F32),
            pltpu.VMEM((2, nblk, 1, blk), F32),
            pltpu.VMEM((2, nblk, LANE, blk), F32),
        ],
        compiler_params=pltpu.CompilerParams(
            dimension_semantics=("arbitrary", "arbitrary"),
            vmem_limit_bytes=VMEM_LIMIT),
        name="diff_attn",
    )(_pair_table(nblk), big, big, big, pos_icol, digit_shift, lane_tab, lq1, lk1, lq2, lk2, g_sub)


def _outproj_body(x_ref, om_ref, od_ref, gm_ref, gd_ref, w_ref, gpost_ref, o_ref):
    gm = gm_ref[...].astype(F32)
    gd = gd_ref[...].astype(F32)
    mm = (om_ref[...].astype(F32) * (gm * jax.nn.sigmoid(gm))).astype(BF16)
    md = (od_ref[...].astype(F32) * (gd * jax.nn.sigmoid(gd))).astype(BF16)
    half = mm.shape[1]
    y = _dot(mm, w_ref[:half, :]) + _dot(md, w_ref[half:, :])
    o_ref[...] = x_ref[...] + _rms(y, gpost_ref[...])


def _outproj(x2, o_mla, o_diff, big, w_out, g_post, tm):
    m, d = x2.shape
    half = o_mla.shape[1]
    gate_diff_blk = (big.shape[1] - half) // half
    return pl.pallas_call(
        _outproj_body,
        grid=(m // tm,),
        in_specs=[
            pl.BlockSpec((tm, d), lambda i: (i, 0)),
            pl.BlockSpec((tm, half), lambda i: (i, 0)),
            pl.BlockSpec((tm, half), lambda i: (i, 0)),
            pl.BlockSpec((tm, half), lambda i: (i, 0)),
            pl.BlockSpec((tm, half), lambda i: (i, gate_diff_blk)),
            pl.BlockSpec(w_out.shape, lambda i: (0, 0)),
            pl.BlockSpec((1, d), lambda i: (0, 0)),
        ],
        out_specs=pl.BlockSpec((tm, d), lambda i: (i, 0)),
        out_shape=jax.ShapeDtypeStruct((m, d), F32),
        compiler_params=pltpu.CompilerParams(
            dimension_semantics=("arbitrary",),
            vmem_limit_bytes=VMEM_LIMIT),
        name="outproj",
    )(x2, o_mla, o_diff, big, big, w_out, g_post)


def kernel(x, positions, g_pre, w_in, g_q_a, w_q_b, g_kv_a, w_kv_b, lambda_q1, lambda_k1,
           lambda_q2, lambda_k2, g_diff_sub, w_out, g_post):
    batch, seq, d = x.shape
    depth = g_pre.shape[0]
    q_rank = w_q_b.shape[1]
    kv_rank = w_kv_b.shape[1]
    half_rope = MLA_ROPE // 2
    lat_end = q_rank + kv_rank
    pe_end = lat_end + MLA_ROPE

    pos_f = positions.astype(F32)
    pos_col = pos_f.reshape(batch * seq, 1)
    pos_icol = positions.astype(jnp.int32).reshape(batch * seq, 1)
    freqs = 1.0 / (ROPE_THETA ** (jnp.arange(0, MLA_ROPE, 2, dtype=F32) / MLA_ROPE))
    freq = jnp.tile(freqs, 4).reshape(1, LANE)
    slopes = 2.0 ** (-8.0 * (jnp.arange(DIFF_HEADS, dtype=F32) + 1.0) / DIFF_HEADS)
    slopes_l2e = slopes * LOG2E

    mla_scale = MLA_QK ** -0.5 * LOG2E
    diff_scale = DIFF_HEAD_DIM ** -0.5 * LOG2E
    width = DIFF_HEADS * 2 * DIFF_HEAD_DIM
    n_big = w_in.shape[2] - pe_end
    col_scale = jnp.ones((1, n_big), F32).at[:, width:2 * width].set(diff_scale)

    x2 = x.reshape(batch * seq, d)
    for l in range(depth):
        lam_init = 0.8 - 0.6 * math.exp(-0.3 * l)
        wi = w_in[l]
        u1 = wi[:, lat_end:lat_end + half_rope]
        u2 = wi[:, lat_end + half_rope:pe_end]
        w_lat = jnp.concatenate([wi[:, :lat_end], u1, u1, u2, u2, u2, u2, u1, u1], axis=1).astype(BF16)
        w_big = wi[:, pe_end:].astype(BF16)
        wq = w_q_b[l]
        t1 = wq[:, :, MLA_NOPE:MLA_NOPE + half_rope]
        t2 = wq[:, :, MLA_NOPE + half_rope:]
        w_q = jnp.concatenate([wq[:, :, :MLA_NOPE], t1, t2, t2, t1], axis=-1)
        w_q = w_q.reshape(q_rank, MLA_HEADS * QK_PAD).astype(BF16)
        wkv = w_kv_b[l]
        w_kv = jnp.concatenate([wkv[:, :, :MLA_NOPE].reshape(kv_rank, -1),
                                wkv[:, :, MLA_NOPE:].reshape(kv_rank, -1)], axis=1).astype(BF16)
        gp = g_pre[l].reshape(1, d)

        big = _inproj(x2, gp, w_big, col_scale, tm=1024, tn=512)
        q, k, v = _latent(x2, pos_col, freq, gp, w_lat, g_q_a[l].reshape(1, -1), w_q,
                          g_kv_a[l].reshape(1, -1), w_kv, tm=512, q_scale=mla_scale)
        o_mla = _mla_attention(q, k, v, batch, seq, blk=512)
        o_diff = _diff_attention(
            big, pos_icol, slopes_l2e,
            lambda_q1[l].reshape(1, -1), lambda_k1[l].reshape(1, -1),
            lambda_q2[l].reshape(1, -1), lambda_k2[l].reshape(1, -1),
            g_diff_sub[l].reshape(1, -1), batch, seq, blk=512, lam_init=lam_init)
        x2 = _outproj(x2, o_mla, o_diff, big, w_out[l].astype(BF16), g_post[l].reshape(1, d), tm=512)
    return x2.reshape(batch, seq, d)
```

```python
import functools
import math

import numpy as np
import jax
import jax.numpy as jnp
from jax import lax
from jax.experimental import pallas as pl
from jax.experimental.pallas import tpu as pltpu

F32 = jnp.float32
BF16 = jnp.bfloat16

EPS = 1e-6
LOG2E = 1.4426950408889634
ROPE_THETA = 10000.0

MLA_HEADS = 8
MLA_NOPE = 128
MLA_ROPE = 64
MLA_V = 128
MLA_QK = MLA_NOPE + MLA_ROPE
DIFF_HEADS = 8
DIFF_HEAD_DIM = 64
LANE = 128
QK_PAD = 256
ALIBI_LANES = 9
ALIBI_BASE = (DIFF_HEAD_DIM, 0)

VMEM_LIMIT = 48 * 1024 * 1024


def _rms(xf, g):
    ms = jnp.mean(xf * xf, axis=-1, keepdims=True)
    return xf * lax.rsqrt(ms + EPS) * g


def _dot(a, b):
    return jnp.dot(a, b, preferred_element_type=F32)


def _dot_nt(a, b):
    return lax.dot_general(a, b, (((1,), (1,)), ((), ())), preferred_element_type=F32)


def _inproj_body(x_ref, g_ref, w_ref, cs_ref, o_ref, h_ref):
    @pl.when(pl.program_id(1) == 0)
    def _():
        h_ref[...] = _rms(x_ref[...], g_ref[...]).astype(BF16)

    acc = _dot(h_ref[...], w_ref[...])
    o_ref[...] = (acc * cs_ref[...]).astype(BF16)


def _inproj(x2, g_pre, w_big, col_scale, tm, tn):
    m, d = x2.shape
    n = w_big.shape[1]
    return pl.pallas_call(
        _inproj_body,
        grid=(m // tm, n // tn),
        in_specs=[
            pl.BlockSpec((tm, d), lambda i, j: (i, 0)),
            pl.BlockSpec((1, d), lambda i, j: (0, 0)),
            pl.BlockSpec((d, tn), lambda i, j: (0, j)),
            pl.BlockSpec((1, tn), lambda i, j: (0, j)),
        ],
        out_specs=pl.BlockSpec((tm, tn), lambda i, j: (i, j)),
        out_shape=jax.ShapeDtypeStruct((m, n), BF16),
        scratch_shapes=[pltpu.VMEM((tm, d), BF16)],
        compiler_params=pltpu.CompilerParams(
            dimension_semantics=("arbitrary", "arbitrary"),
            vmem_limit_bytes=VMEM_LIMIT),
        name="inproj",
    )(x2, g_pre, w_big, col_scale)


def _latent_body(x_ref, pos_ref, freq_ref, gpre_ref, wlat_ref, gq_ref, wqt_ref,
                 gkv_ref, wk_ref, wvt_ref, qt_ref, k_ref, vt_ref, *, q_scale):
    h = _rms(x_ref[...], gpre_ref[...]).astype(BF16)
    lat = _dot(h, wlat_ref[...])
    ang = pos_ref[...] * freq_ref[...]
    cos = jnp.cos(ang)
    sin = jnp.sin(ang)
    seg = lax.broadcasted_iota(jnp.int32, (1, LANE), 1) // (MLA_ROPE // 2)
    fq = jnp.where(seg == 1, -sin, jnp.where(seg == 3, sin, cos)) * q_scale
    sk = jnp.where(seg < 2, -sin, sin)
    k_ext = (lat[:, 768:896] * cos + lat[:, 896:1024] * sk).astype(BF16)

    c_q = _rms(lat[:, :512], gq_ref[...]).astype(BF16)
    qft = _dot_nt(wqt_ref[...], c_q)
    fqt = fq.T
    c_kv = _rms(lat[:, 512:768], gkv_ref[...]).astype(BF16)
    kf = _dot(c_kv, wk_ref[...])
    vt_ref[...] = _dot_nt(wvt_ref[...], c_kv).astype(BF16)
    for hd in range(MLA_HEADS):
        o = hd * QK_PAD
        qt_ref[o:o + LANE, :] = (qft[o:o + LANE, :] * q_scale).astype(BF16)
        qt_ref[o + LANE:o + QK_PAD, :] = (qft[o + LANE:o + QK_PAD, :] * fqt).astype(BF16)
        k_ref[:, o:o + LANE] = kf[:, hd * LANE:(hd + 1) * LANE].astype(BF16)
        k_ref[:, o + LANE:o + QK_PAD] = k_ext


def _latent(x2, pos_col, freq, g_pre, w_lat, g_q, w_qt, g_kv, w_k, w_vt, tm, q_scale):
    m, d = x2.shape
    nq = MLA_HEADS * QK_PAD
    nv = MLA_HEADS * MLA_V
    full = lambda a: pl.BlockSpec(a.shape, lambda i: (0,) * a.ndim)
    return pl.pallas_call(
        functools.partial(_latent_body, q_scale=q_scale),
        grid=(m // tm,),
        in_specs=[
            pl.BlockSpec((tm, d), lambda i: (i, 0)),
            pl.BlockSpec((tm, 1), lambda i: (i, 0)),
            full(freq), full(g_pre), full(w_lat), full(g_q), full(w_qt), full(g_kv), full(w_k),
            full(w_vt),
        ],
        out_specs=[
            pl.BlockSpec((nq, tm), lambda i: (0, i)),
            pl.BlockSpec((tm, nq), lambda i: (i, 0)),
            pl.BlockSpec((nv, tm), lambda i: (0, i)),
        ],
        out_shape=[
            jax.ShapeDtypeStruct((nq, m), BF16),
            jax.ShapeDtypeStruct((m, nq), BF16),
            jax.ShapeDtypeStruct((nv, m), BF16),
        ],
        compiler_params=pltpu.CompilerParams(
            dimension_semantics=("arbitrary",),
            vmem_limit_bytes=VMEM_LIMIT),
        name="latent",
    )(x2, pos_col, freq, g_pre, w_lat, g_q, w_qt, g_kv, w_k, w_vt)


def _blk_slice(idx, blk):
    if isinstance(idx, int):
        return slice(idx * blk, (idx + 1) * blk)
    return pl.ds(pl.multiple_of(idx * blk, blk), blk)


def _offdiag_pairs(nblk):
    return [(i, t) for i in range(1, nblk) for t in range(i)]


def _pair_table(nblk):
    return jnp.asarray(np.array(_offdiag_pairs(nblk), dtype=np.int32).T)


def _init_mask_bias(mb_ref, blk):
    kv = lax.broadcasted_iota(jnp.int32, (blk, blk), 0)
    qi = lax.broadcasted_iota(jnp.int32, (blk, blk), 1)
    mb_ref[...] = jnp.where(kv <= qi, 0.0, -jnp.inf).astype(F32)


def _transpose_v(v_ref, vt_ref, blk, nblk):
    for c in range(nblk):
        sl = _blk_slice(c, blk)
        vt_ref[:, sl] = v_ref[sl, :].astype(F32).T.astype(BF16)


def _run_pipeline(stage_a, stage_b_diag, stage_c_diag, stage_b, stage_c, tab_ref, nblk):
    pairs = _offdiag_pairs(nblk)
    npairs = len(pairs)
    stage_a(0, 0)
    stage_b_diag(0)
    stage_a(1, 1)

    def diag_loop(f, carry):
        stage_c_diag(f - 1)
        stage_b_diag(f)
        stage_a(f + 1, f + 1)
        return carry

    lax.fori_loop(1, nblk - 1, diag_loop, 0, unroll=2)

    stage_c_diag(nblk - 2)
    stage_b_diag(nblk - 1)
    stage_a(*pairs[0])
    stage_c_diag(nblk - 1)
    stage_b(*pairs[0])
    stage_a(*pairs[1])

    def off_loop(f, carry):
        stage_c(tab_ref[0, f - 1], tab_ref[1, f - 1])
        stage_b(tab_ref[0, f], tab_ref[1, f])
        stage_a(tab_ref[0, f + 1], tab_ref[1, f + 1])
        return carry

    lax.fori_loop(1, npairs - 1, off_loop, 0, unroll=2)

    stage_c(*pairs[npairs - 2])
    stage_b(*pairs[npairs - 1])
    stage_c(*pairs[npairs - 1])


def _mla_body(tab_ref, qt_ref, k_ref, vt_ref, o_ref, mb_ref, s_ref, p_ref, al_ref,
              m_ref, l_ref, acc_ref, *, blk, nblk):
    _init_mask_bias(mb_ref, blk)

    def stage_a(i, t):
        s_ref[...] = _dot(k_ref[_blk_slice(t, blk), :], qt_ref[:, _blk_slice(i, blk)])

    def stage_b_diag(i):
        s = s_ref[...] + mb_ref[...]
        m = jnp.max(s, axis=0, keepdims=True)
        p = jnp.exp2(s - m)
        m_ref[i] = m
        l_ref[i] = jnp.sum(p, axis=0, keepdims=True)
        p_ref[...] = p.astype(BF16)

    def stage_c_diag(i):
        acc_ref[i] = _dot(vt_ref[:, _blk_slice(i, blk)], p_ref[...])

    def stage_b(i, t):
        s = s_ref[...]
        m_prev = m_ref[i]
        m_new = jnp.maximum(m_prev, jnp.max(s, axis=0, keepdims=True))
        alpha = jnp.exp2(m_prev - m_new)
        p = jnp.exp2(s - m_new)
        l_ref[i] = alpha * l_ref[i] + jnp.sum(p, axis=0, keepdims=True)
        m_ref[i] = m_new
        al_ref[...] = alpha
        p_ref[...] = p.astype(BF16)

    def stage_c(i, t):
        acc_ref[i] = al_ref[...] * acc_ref[i] + _dot(vt_ref[:, _blk_slice(t, blk)], p_ref[...])

    _run_pipeline(stage_a, stage_b_diag, stage_c_diag, stage_b, stage_c, tab_ref, nblk)

    for i in range(nblk):
        o_t = acc_ref[i] * (1.0 / l_ref[i])
        o_ref[_blk_slice(i, blk), :] = o_t.T.astype(BF16)


def _mla_attention(qt, k, vt, batch, seq, blk):
    nblk = seq // blk
    assert nblk >= 3
    return pl.pallas_call(
        functools.partial(_mla_body, blk=blk, nblk=nblk),
        grid=(batch, MLA_HEADS),
        in_specs=[
            pl.BlockSpec(memory_space=pltpu.SMEM),
            pl.BlockSpec((QK_PAD, seq), lambda b, h: (h, b)),
            pl.BlockSpec((seq, QK_PAD), lambda b, h: (b, h)),
            pl.BlockSpec((MLA_V, seq), lambda b, h: (h, b)),
        ],
        out_specs=pl.BlockSpec((seq, MLA_V), lambda b, h: (b, h)),
        out_shape=jax.ShapeDtypeStruct((batch * seq, MLA_HEADS * MLA_V), BF16),
        scratch_shapes=[
            pltpu.VMEM((blk, blk), F32),
            pltpu.VMEM((blk, blk), F32),
            pltpu.VMEM((blk, blk), BF16),
            pltpu.VMEM((1, blk), F32),
            pltpu.VMEM((nblk, 1, blk), F32),
            pltpu.VMEM((nblk, 1, blk), F32),
            pltpu.VMEM((nblk, MLA_V, blk), F32),
        ],
        compiler_params=pltpu.CompilerParams(
            dimension_semantics=("arbitrary", "arbitrary"),
            vmem_limit_bytes=VMEM_LIMIT),
        name="mla_attn",
    )(_pair_table(nblk), qt, k, vt)


def _alibi_lane_tables(slopes_l2e):
    c1 = slopes_l2e.astype(BF16).astype(F32)
    c2 = (slopes_l2e - c1).astype(BF16).astype(F32)
    c3 = (slopes_l2e - c1 - c2).astype(BF16).astype(F32)
    cw = (jnp.stack([c1, c2, c3], axis=1)[:, :, None]
          * jnp.asarray([1.0, 256.0, 65536.0], F32)[None, None, :]).reshape(-1, ALIBI_LANES)
    tab = jnp.zeros((slopes_l2e.shape[0], 4, LANE), F32)
    for x, base in enumerate(ALIBI_BASE):
        tab = tab.at[:, x, base + ALIBI_LANES:base + 2 * ALIBI_LANES].set(cw)
        tab = tab.at[:, 2 + x, base:base + ALIBI_LANES].set(-cw)
    return tab.astype(BF16)


def _diff_body(tab_ref, q_ref, k_ref, v_ref, pc_ref, sh_ref, lt_ref, lq1_ref, lk1_ref, lq2_ref,
               lk2_ref, gsub_ref, o_ref, vt_ref, mb_ref, dig_ref, qm_ref, km_ref, s_ref, p_ref, al_ref,
               m_ref, l_ref, acc_ref, *, blk, nblk, lam_init):
    _init_mask_bias(mb_ref, blk)
    _transpose_v(v_ref, vt_ref, blk, nblk)

    @pl.when(pl.program_id(1) == 0)
    def _():
        pcol = pc_ref[...]
        prel = jnp.broadcast_to(pcol - pcol[0:1, :], dig_ref.shape)
        shift = jnp.broadcast_to(sh_ref[...], dig_ref.shape)
        dig_ref[...] = (lax.shift_right_logical(prel, shift) & 255).astype(F32).astype(BF16)

    lane = lax.broadcasted_iota(jnp.int32, (1, LANE), 1)
    dig = dig_ref[...]
    q = q_ref[...]
    k = k_ref[...]
    for x, base in enumerate(ALIBI_BASE):
        own = (lane < DIFF_HEAD_DIM) if x == 0 else (lane >= DIFF_HEAD_DIM)
        q_digit = (lane >= base) & (lane < base + ALIBI_LANES)
        k_digit = (lane >= base + ALIBI_LANES) & (lane < base + 2 * ALIBI_LANES)
        qm_ref[x] = jnp.where(own, q, jnp.where(q_digit, dig, lt_ref[0, x:x + 1, :]))
        km_ref[x] = jnp.where(own, k, jnp.where(k_digit, dig, lt_ref[0, 2 + x:3 + x, :]))

    def stage_a(i, t):
        for x in range(2):
            s_ref[x] = _dot_nt(km_ref[x, _blk_slice(t, blk), :], qm_ref[x, _blk_slice(i, blk), :])

    def stage_b_diag(i):
        nb = mb_ref[...]
        for x in range(2):
            s = s_ref[x] + nb
            m = jnp.max(s, axis=0, keepdims=True)
            p = jnp.exp2(s - m)
            m_ref[x, i] = m
            l_ref[x, i] = jnp.sum(p, axis=0, keepdims=True)
            p_ref[x] = p.astype(BF16)

    def stage_c_diag(i):
        vt = vt_ref[:, _blk_slice(i, blk)]
        for x in range(2):
            acc_ref[x, i] = _dot(vt, p_ref[x])

    def stage_b(i, t):
        for x in range(2):
            s = s_ref[x]
            m_prev = m_ref[x, i]
            m_new = jnp.maximum(m_prev, jnp.max(s, axis=0, keepdims=True))
            alpha = jnp.exp2(m_prev - m_new)
            p = jnp.exp2(s - m_new)
            l_ref[x, i] = alpha * l_ref[x, i] + jnp.sum(p, axis=0, keepdims=True)
            m_ref[x, i] = m_new
            al_ref[x] = alpha
            p_ref[x] = p.astype(BF16)

    def stage_c(i, t):
        vt = vt_ref[:, _blk_slice(t, blk)]
        for x in range(2):
            acc_ref[x, i] = al_ref[x] * acc_ref[x, i] + _dot(vt, p_ref[x])

    _run_pipeline(stage_a, stage_b_diag, stage_c_diag, stage_b, stage_c, tab_ref, nblk)

    lam = (jnp.exp(jnp.sum(lq1_ref[...] * lk1_ref[...], axis=-1, keepdims=True))
           - jnp.exp(jnp.sum(lq2_ref[...] * lk2_ref[...], axis=-1, keepdims=True))
           + lam_init)
    for i in range(nblk):
        o_t = (acc_ref[0, i] * (1.0 / l_ref[0, i])
               - lam * (acc_ref[1, i] * (1.0 / l_ref[1, i])))
        o = o_t.T
        o_ref[_blk_slice(i, blk), :] = (_rms(o, gsub_ref[...]) * (1.0 - lam_init)).astype(BF16)


def _diff_attention(big, pos_icol, slopes_l2e, lq1, lk1, lq2, lk2, g_sub,
                    batch, seq, blk, lam_init):
    nblk = seq // blk
    assert nblk >= 3
    width = DIFF_HEADS * 2 * DIFF_HEAD_DIM
    cb = width // LANE
    q_off, k_off, v_off = 1 * cb, 2 * cb, 3 * cb
    lane_tab = _alibi_lane_tables(slopes_l2e)
    digit_shift = jnp.asarray(((np.arange(LANE) % DIFF_HEAD_DIM) % 3 * 8).reshape(1, LANE), jnp.int32)
    small = lambda a: pl.BlockSpec(a.shape, lambda b, h: (0,) * a.ndim)
    return pl.pallas_call(
        functools.partial(_diff_body, blk=blk, nblk=nblk, lam_init=lam_init),
        grid=(batch, DIFF_HEADS),
        in_specs=[
            pl.BlockSpec(memory_space=pltpu.SMEM),
            pl.BlockSpec((seq, LANE), lambda b, h: (b, q_off + h)),
            pl.BlockSpec((seq, LANE), lambda b, h: (b, k_off + h)),
            pl.BlockSpec((seq, LANE), lambda b, h: (b, v_off + h)),
            pl.BlockSpec((seq, 1), lambda b, h: (b, 0)),
            small(digit_shift),
            pl.BlockSpec((1, 4, LANE), lambda b, h: (h, 0, 0)),
            small(lq1), small(lk1), small(lq2), small(lk2), small(g_sub),
        ],
        out_specs=pl.BlockSpec((seq, LANE), lambda b, h: (b, h)),
        out_shape=jax.ShapeDtypeStruct((batch * seq, width), BF16),
        scratch_shapes=[
            pltpu.VMEM((LANE, seq), BF16),
            pltpu.VMEM((blk, blk), F32),
            pltpu.VMEM((seq, LANE), BF16),
            pltpu.VMEM((2, seq, LANE), BF16),
            pltpu.VMEM((2, seq, LANE), BF16),
            pltpu.VMEM((2, blk, blk), F32),
            pltpu.VMEM((2, blk, blk), BF16),
            pltpu.VMEM((2, 1, blk), F32),
            pltpu.VMEM((2, nblk, 1, blk), F32),
            pltpu.VMEM((2, nblk, 1, blk), F32),
            pltpu.VMEM((2, nblk, LANE, blk), F32),
        ],
        compiler_params=pltpu.CompilerParams(
            dimension_semantics=("arbitrary", "arbitrary"),
            vmem_limit_bytes=VMEM_LIMIT),
        name="diff_attn",
    )(_pair_table(nblk), big, big, big, pos_icol, digit_shift, lane_tab, lq1, lk1, lq2, lk2, g_sub)


def _outproj_body(x_ref, om_ref, od_ref, gm_ref, gd_ref, w_ref, gpost_ref, o_ref):
    gm = gm_ref[...].astype(F32)
    gd = gd_ref[...].astype(F32)
    mm = (om_ref[...].astype(F32) * (gm * jax.nn.sigmoid(gm))).astype(BF16)
    md = (od_ref[...].astype(F32) * (gd * jax.nn.sigmoid(gd))).astype(BF16)
    half = mm.shape[1]
    y = _dot(mm, w_ref[:half, :]) + _dot(md, w_ref[half:, :])
    o_ref[...] = x_ref[...] + _rms(y, gpost_ref[...])


def _outproj(x2, o_mla, o_diff, big, w_out, g_post, tm):
    m, d = x2.shape
    half = o_mla.shape[1]
    gate_diff_blk = (big.shape[1] - half) // half
    return pl.pallas_call(
        _outproj_body,
        grid=(m // tm,),
        in_specs=[
            pl.BlockSpec((tm, d), lambda i: (i, 0)),
            pl.BlockSpec((tm, half), lambda i: (i, 0)),
            pl.BlockSpec((tm, half), lambda i: (i, 0)),
            pl.BlockSpec((tm, half), lambda i: (i, 0)),
            pl.BlockSpec((tm, half), lambda i: (i, gate_diff_blk)),
            pl.BlockSpec(w_out.shape, lambda i: (0, 0)),
            pl.BlockSpec((1, d), lambda i: (0, 0)),
        ],
        out_specs=pl.BlockSpec((tm, d), lambda i: (i, 0)),
        out_shape=jax.ShapeDtypeStruct((m, d), F32),
        compiler_params=pltpu.CompilerParams(
            dimension_semantics=("arbitrary",),
            vmem_limit_bytes=VMEM_LIMIT),
        name="outproj",
    )(x2, o_mla, o_diff, big, big, w_out, g_post)


def kernel(x, positions, g_pre, w_in, g_q_a, w_q_b, g_kv_a, w_kv_b, lambda_q1, lambda_k1,
           lambda_q2, lambda_k2, g_diff_sub, w_out, g_post):
    batch, seq, d = x.shape
    depth = g_pre.shape[0]
    q_rank = w_q_b.shape[1]
    kv_rank = w_kv_b.shape[1]
    half_rope = MLA_ROPE // 2
    lat_end = q_rank + kv_rank
    pe_end = lat_end + MLA_ROPE

    pos_f = positions.astype(F32)
    pos_col = pos_f.reshape(batch * seq, 1)
    pos_icol = positions.astype(jnp.int32).reshape(batch * seq, 1)
    freqs = 1.0 / (ROPE_THETA ** (jnp.arange(0, MLA_ROPE, 2, dtype=F32) / MLA_ROPE))
    freq = jnp.tile(freqs, 4).reshape(1, LANE)
    slopes = 2.0 ** (-8.0 * (jnp.arange(DIFF_HEADS, dtype=F32) + 1.0) / DIFF_HEADS)
    slopes_l2e = slopes * LOG2E

    mla_scale = MLA_QK ** -0.5 * LOG2E
    diff_scale = DIFF_HEAD_DIM ** -0.5 * LOG2E
    width = DIFF_HEADS * 2 * DIFF_HEAD_DIM
    n_big = w_in.shape[2] - pe_end
    col_scale = jnp.ones((1, n_big), F32).at[:, width:2 * width].set(diff_scale)

    x2 = x.reshape(batch * seq, d)
    for l in range(depth):
        lam_init = 0.8 - 0.6 * math.exp(-0.3 * l)
        wi = w_in[l]
        u1 = wi[:, lat_end:lat_end + half_rope]
        u2 = wi[:, lat_end + half_rope:pe_end]
        w_lat = jnp.concatenate([wi[:, :lat_end], u1, u1, u2, u2, u2, u2, u1, u1], axis=1).astype(BF16)
        w_big = wi[:, pe_end:].astype(BF16)
        wq = w_q_b[l]
        t1 = wq[:, :, MLA_NOPE:MLA_NOPE + half_rope]
        t2 = wq[:, :, MLA_NOPE + half_rope:]
        w_q = jnp.concatenate([wq[:, :, :MLA_NOPE], t1, t2, t2, t1], axis=-1)
        w_qt = w_q.reshape(q_rank, MLA_HEADS * QK_PAD).T.astype(BF16)
        wkv = w_kv_b[l]
        w_k = wkv[:, :, :MLA_NOPE].reshape(kv_rank, -1).astype(BF16)
        w_vt = wkv[:, :, MLA_NOPE:].reshape(kv_rank, -1).T.astype(BF16)
        gp = g_pre[l].reshape(1, d)

        big = _inproj(x2, gp, w_big, col_scale, tm=1024, tn=512)
        qt, k, vt = _latent(x2, pos_col, freq, gp, w_lat, g_q_a[l].reshape(1, -1), w_qt,
                            g_kv_a[l].reshape(1, -1), w_k, w_vt, tm=512, q_scale=mla_scale)
        o_mla = _mla_attention(qt, k, vt, batch, seq, blk=512)
        o_diff = _diff_attention(
            big, pos_icol, slopes_l2e,
            lambda_q1[l].reshape(1, -1), lambda_k1[l].reshape(1, -1),
            lambda_q2[l].reshape(1, -1), lambda_k2[l].reshape(1, -1),
            g_diff_sub[l].reshape(1, -1), batch, seq, blk=512, lam_init=lam_init)
        x2 = _outproj(x2, o_mla, o_diff, big, w_out[l].astype(BF16), g_post[l].reshape(1, d), tm=512)
    return x2.reshape(batch, seq, d)
```

```python
import functools
import math

import numpy as np
import jax
import jax.numpy as jnp
from jax import lax
from jax.experimental import pallas as pl
from jax.experimental.pallas import tpu as pltpu

F32 = jnp.float32
BF16 = jnp.bfloat16

EPS = 1e-6
LOG2E = 1.4426950408889634
ROPE_THETA = 10000.0

MLA_HEADS = 8
MLA_NOPE = 128
MLA_ROPE = 64
MLA_V = 128
MLA_QK = MLA_NOPE + MLA_ROPE
DIFF_HEADS = 8
DIFF_HEAD_DIM = 64
LANE = 128
QK_PAD = 256
ALIBI_LANES = 9
ALIBI_BASE = (DIFF_HEAD_DIM, 0)

VMEM_LIMIT = 48 * 1024 * 1024


def _rms(xf, g):
    ms = jnp.mean(xf * xf, axis=-1, keepdims=True)
    return xf * lax.rsqrt(ms + EPS) * g


def _dot(a, b):
    return jnp.dot(a, b, preferred_element_type=F32)


def _dot_nt(a, b):
    return lax.dot_general(a, b, (((1,), (1,)), ((), ())), preferred_element_type=F32)


def _inproj_body(rs_ref, x_ref, g_ref, wn_ref, wt_ref, on_ref, ot_ref, h_ref, *, n_normal):
    j = pl.program_id(1)

    @pl.when(j == 0)
    def _():
        h_ref[...] = _rms(x_ref[...], g_ref[...]).astype(BF16)

    @pl.when(j < n_normal)
    def _():
        on_ref[...] = _dot(h_ref[...], wn_ref[...]).astype(BF16)

    @pl.when(j >= n_normal)
    def _():
        ot_ref[...] = (_dot_nt(wt_ref[...], h_ref[...]) * rs_ref[j - n_normal]).astype(BF16)


def _inproj(x2, g_pre, w_n, w_t, row_scale, tm, tn):
    m, d = x2.shape
    n_normal = w_n.shape[1] // tn
    n_trans = w_t.shape[0] // tn
    last_n = n_normal - 1
    col_n = lambda j: jnp.minimum(j, last_n)
    row_t = lambda j: jnp.maximum(j - n_normal, 0)
    return pl.pallas_call(
        functools.partial(_inproj_body, n_normal=n_normal),
        grid=(m // tm, n_normal + n_trans),
        in_specs=[
            pl.BlockSpec(memory_space=pltpu.SMEM),
            pl.BlockSpec((tm, d), lambda i, j: (i, 0)),
            pl.BlockSpec((1, d), lambda i, j: (0, 0)),
            pl.BlockSpec((d, tn), lambda i, j: (0, col_n(j))),
            pl.BlockSpec((tn, d), lambda i, j: (row_t(j), 0)),
        ],
        out_specs=[
            pl.BlockSpec((tm, tn), lambda i, j: (i, col_n(j))),
            pl.BlockSpec((tn, tm), lambda i, j: (row_t(j), i)),
        ],
        out_shape=[
            jax.ShapeDtypeStruct((m, w_n.shape[1]), BF16),
            jax.ShapeDtypeStruct((w_t.shape[0], m), BF16),
        ],
        scratch_shapes=[pltpu.VMEM((tm, d), BF16)],
        compiler_params=pltpu.CompilerParams(
            dimension_semantics=("arbitrary", "arbitrary"),
            vmem_limit_bytes=VMEM_LIMIT),
        name="inproj",
    )(row_scale, x2, g_pre, w_n, w_t)


def _latent_body(x_ref, pos_ref, freq_ref, gpre_ref, wlat_ref, gq_ref, wqt_ref,
                 gkv_ref, wk_ref, wvt_ref, qt_ref, k_ref, vt_ref, *, q_scale):
    h = _rms(x_ref[...], gpre_ref[...]).astype(BF16)
    lat = _dot(h, wlat_ref[...])
    ang = pos_ref[...] * freq_ref[...]
    cos = jnp.cos(ang)
    sin = jnp.sin(ang)
    seg = lax.broadcasted_iota(jnp.int32, (1, LANE), 1) // (MLA_ROPE // 2)
    fq = jnp.where(seg == 1, -sin, jnp.where(seg == 3, sin, cos)) * q_scale
    sk = jnp.where(seg < 2, -sin, sin)
    k_ext = (lat[:, 768:896] * cos + lat[:, 896:1024] * sk).astype(BF16)

    c_q = _rms(lat[:, :512], gq_ref[...]).astype(BF16)
    qft = _dot_nt(wqt_ref[...], c_q)
    fqt = fq.T
    c_kv = _rms(lat[:, 512:768], gkv_ref[...]).astype(BF16)
    kf = _dot(c_kv, wk_ref[...])
    vt_ref[...] = _dot_nt(wvt_ref[...], c_kv).astype(BF16)
    for hd in range(MLA_HEADS):
        o = hd * QK_PAD
        qt_ref[o:o + LANE, :] = (qft[o:o + LANE, :] * q_scale).astype(BF16)
        qt_ref[o + LANE:o + QK_PAD, :] = (qft[o + LANE:o + QK_PAD, :] * fqt).astype(BF16)
        k_ref[:, o:o + LANE] = kf[:, hd * LANE:(hd + 1) * LANE].astype(BF16)
        k_ref[:, o + LANE:o + QK_PAD] = k_ext


def _latent(x2, pos_col, freq, g_pre, w_lat, g_q, w_qt, g_kv, w_k, w_vt, tm, q_scale):
    m, d = x2.shape
    nq = MLA_HEADS * QK_PAD
    nv = MLA_HEADS * MLA_V
    full = lambda a: pl.BlockSpec(a.shape, lambda i: (0,) * a.ndim)
    return pl.pallas_call(
        functools.partial(_latent_body, q_scale=q_scale),
        grid=(m // tm,),
        in_specs=[
            pl.BlockSpec((tm, d), lambda i: (i, 0)),
            pl.BlockSpec((tm, 1), lambda i: (i, 0)),
            full(freq), full(g_pre), full(w_lat), full(g_q), full(w_qt), full(g_kv), full(w_k),
            full(w_vt),
        ],
        out_specs=[
            pl.BlockSpec((nq, tm), lambda i: (0, i)),
            pl.BlockSpec((tm, nq), lambda i: (i, 0)),
            pl.BlockSpec((nv, tm), lambda i: (0, i)),
        ],
        out_shape=[
            jax.ShapeDtypeStruct((nq, m), BF16),
            jax.ShapeDtypeStruct((m, nq), BF16),
            jax.ShapeDtypeStruct((nv, m), BF16),
        ],
        compiler_params=pltpu.CompilerParams(
            dimension_semantics=("arbitrary",),
            vmem_limit_bytes=VMEM_LIMIT),
        name="latent",
    )(x2, pos_col, freq, g_pre, w_lat, g_q, w_qt, g_kv, w_k, w_vt)


def _blk_slice(idx, blk):
    if isinstance(idx, int):
        return slice(idx * blk, (idx + 1) * blk)
    return pl.ds(pl.multiple_of(idx * blk, blk), blk)


def _offdiag_pairs(nblk):
    return [(i, t) for i in range(1, nblk) for t in range(i)]


def _pair_table(nblk):
    return jnp.asarray(np.array(_offdiag_pairs(nblk), dtype=np.int32).T)


def _init_mask_bias(mb_ref, blk):
    kv = lax.broadcasted_iota(jnp.int32, (blk, blk), 0)
    qi = lax.broadcasted_iota(jnp.int32, (blk, blk), 1)
    mb_ref[...] = jnp.where(kv <= qi, 0.0, -jnp.inf).astype(F32)


def _run_pipeline(stage_a, stage_b_diag, stage_c_diag, stage_b, stage_c, tab_ref, nblk):
    pairs = _offdiag_pairs(nblk)
    npairs = len(pairs)
    stage_a(0, 0)
    stage_b_diag(0)
    stage_a(1, 1)

    def diag_loop(f, carry):
        stage_c_diag(f - 1)
        stage_b_diag(f)
        stage_a(f + 1, f + 1)
        return carry

    lax.fori_loop(1, nblk - 1, diag_loop, 0, unroll=2)

    stage_c_diag(nblk - 2)
    stage_b_diag(nblk - 1)
    stage_a(*pairs[0])
    stage_c_diag(nblk - 1)
    stage_b(*pairs[0])
    stage_a(*pairs[1])

    def off_loop(f, carry):
        stage_c(tab_ref[0, f - 1], tab_ref[1, f - 1])
        stage_b(tab_ref[0, f], tab_ref[1, f])
        stage_a(tab_ref[0, f + 1], tab_ref[1, f + 1])
        return carry

    lax.fori_loop(1, npairs - 1, off_loop, 0, unroll=2)

    stage_c(*pairs[npairs - 2])
    stage_b(*pairs[npairs - 1])
    stage_c(*pairs[npairs - 1])


def _mla_body(tab_ref, qt_ref, k_ref, vt_ref, o_ref, mb_ref, s_ref, p_ref, al_ref,
              m_ref, l_ref, acc_ref, *, blk, nblk):
    _init_mask_bias(mb_ref, blk)

    def stage_a(i, t):
        s_ref[...] = _dot(k_ref[_blk_slice(t, blk), :], qt_ref[:, _blk_slice(i, blk)])

    def stage_b_diag(i):
        s = s_ref[...] + mb_ref[...]
        m = jnp.max(s, axis=0, keepdims=True)
        p = jnp.exp2(s - m)
        m_ref[i] = m
        l_ref[i] = jnp.sum(p, axis=0, keepdims=True)
        p_ref[...] = p.astype(BF16)

    def stage_c_diag(i):
        acc_ref[i] = _dot(vt_ref[:, _blk_slice(i, blk)], p_ref[...])

    def stage_b(i, t):
        s = s_ref[...]
        m_prev = m_ref[i]
        m_new = jnp.maximum(m_prev, jnp.max(s, axis=0, keepdims=True))
        alpha = jnp.exp2(m_prev - m_new)
        p = jnp.exp2(s - m_new)
        l_ref[i] = alpha * l_ref[i] + jnp.sum(p, axis=0, keepdims=True)
        m_ref[i] = m_new
        al_ref[...] = alpha
        p_ref[...] = p.astype(BF16)

    def stage_c(i, t):
        acc_ref[i] = al_ref[...] * acc_ref[i] + _dot(vt_ref[:, _blk_slice(t, blk)], p_ref[...])

    _run_pipeline(stage_a, stage_b_diag, stage_c_diag, stage_b, stage_c, tab_ref, nblk)

    for i in range(nblk):
        o_t = acc_ref[i] * (1.0 / l_ref[i])
        o_ref[_blk_slice(i, blk), :] = o_t.T.astype(BF16)


def _mla_attention(qt, k, vt, batch, seq, blk):
    nblk = seq // blk
    assert nblk >= 3
    return pl.pallas_call(
        functools.partial(_mla_body, blk=blk, nblk=nblk),
        grid=(batch, MLA_HEADS),
        in_specs=[
            pl.BlockSpec(memory_space=pltpu.SMEM),
            pl.BlockSpec((QK_PAD, seq), lambda b, h: (h, b)),
            pl.BlockSpec((seq, QK_PAD), lambda b, h: (b, h)),
            pl.BlockSpec((MLA_V, seq), lambda b, h: (h, b)),
        ],
        out_specs=pl.BlockSpec((seq, MLA_V), lambda b, h: (b, h)),
        out_shape=jax.ShapeDtypeStruct((batch * seq, MLA_HEADS * MLA_V), BF16),
        scratch_shapes=[
            pltpu.VMEM((blk, blk), F32),
            pltpu.VMEM((blk, blk), F32),
            pltpu.VMEM((blk, blk), BF16),
            pltpu.VMEM((1, blk), F32),
            pltpu.VMEM((nblk, 1, blk), F32),
            pltpu.VMEM((nblk, 1, blk), F32),
            pltpu.VMEM((nblk, MLA_V, blk), F32),
        ],
        compiler_params=pltpu.CompilerParams(
            dimension_semantics=("arbitrary", "arbitrary"),
            vmem_limit_bytes=VMEM_LIMIT),
        name="mla_attn",
    )(_pair_table(nblk), qt, k, vt)


def _alibi_lane_tables(slopes_l2e):
    c1 = slopes_l2e.astype(BF16).astype(F32)
    c2 = (slopes_l2e - c1).astype(BF16).astype(F32)
    c3 = (slopes_l2e - c1 - c2).astype(BF16).astype(F32)
    cw = (jnp.stack([c1, c2, c3], axis=1)[:, :, None]
          * jnp.asarray([1.0, 256.0, 65536.0], F32)[None, None, :]).reshape(-1, ALIBI_LANES)
    heads = slopes_l2e.shape[0]
    ktab = jnp.zeros((heads, 2, LANE), F32)
    for x, base in enumerate(ALIBI_BASE):
        ktab = ktab.at[:, x, base:base + ALIBI_LANES].set(-cw)
    qcol = jnp.zeros((heads, DIFF_HEAD_DIM), F32).at[:, ALIBI_LANES:2 * ALIBI_LANES].set(cw)
    return ktab.astype(BF16), qcol.reshape(heads, DIFF_HEAD_DIM, 1)


def _diff_body(tab_ref, qt_ref, k_ref, vt_ref, pc_ref, pr_ref, shr_ref, shc_ref, kt_ref, qc_ref,
               lq1_ref, lk1_ref, lq2_ref, lk2_ref, gsub_ref, o_ref, mb_ref, dig_ref, digt_ref,
               qm_ref, km_ref, s_ref, p_ref, al_ref, m_ref, l_ref, acc_ref, *, blk, nblk, lam_init):
    _init_mask_bias(mb_ref, blk)
    half = DIFF_HEAD_DIM

    @pl.when(pl.program_id(1) == 0)
    def _():
        pcol = pc_ref[...]
        prel = jnp.broadcast_to(pcol - pcol[0:1, :], dig_ref.shape)
        shift = jnp.broadcast_to(shr_ref[...], dig_ref.shape)
        dig_ref[...] = (lax.shift_right_logical(prel, shift) & 255).astype(F32).astype(BF16)
        prow = pr_ref[0]
        prel_t = jnp.broadcast_to(prow - prow[:, 0:1], digt_ref.shape)
        shift_t = jnp.broadcast_to(shc_ref[...], digt_ref.shape)
        digt_ref[...] = (lax.shift_right_logical(prel_t, shift_t) & 255).astype(F32).astype(BF16)

    row = lax.broadcasted_iota(jnp.int32, (half, 1), 0)
    alibi_q = jnp.where(row < ALIBI_LANES, digt_ref[...],
                        jnp.broadcast_to(qc_ref[0], digt_ref.shape).astype(BF16))
    qm_ref[0, :half, :] = qt_ref[:half, :]
    qm_ref[0, half:, :] = alibi_q
    qm_ref[1, :half, :] = alibi_q
    qm_ref[1, half:, :] = qt_ref[half:, :]
    lane = lax.broadcasted_iota(jnp.int32, (1, LANE), 1)
    dig = dig_ref[...]
    k = k_ref[...]
    for x, base in enumerate(ALIBI_BASE):
        own = (lane < half) if x == 0 else (lane >= half)
        k_digit = (lane >= base + ALIBI_LANES) & (lane < base + 2 * ALIBI_LANES)
        km_ref[x] = jnp.where(own, k, jnp.where(k_digit, dig, kt_ref[0, x:x + 1, :]))

    def stage_a(i, t):
        for x in range(2):
            s_ref[x] = _dot(km_ref[x, _blk_slice(t, blk), :], qm_ref[x, :, _blk_slice(i, blk)])

    def stage_b_diag(i):
        nb = mb_ref[...]
        for x in range(2):
            s = s_ref[x] + nb
            m = jnp.max(s, axis=0, keepdims=True)
            p = jnp.exp2(s - m)
            m_ref[x, i] = m
            l_ref[x, i] = jnp.sum(p, axis=0, keepdims=True)
            p_ref[x] = p.astype(BF16)

    def stage_c_diag(i):
        vt = vt_ref[:, _blk_slice(i, blk)]
        for x in range(2):
            acc_ref[x, i] = _dot(vt, p_ref[x])

    def stage_b(i, t):
        for x in range(2):
            s = s_ref[x]
            m_prev = m_ref[x, i]
            m_new = jnp.maximum(m_prev, jnp.max(s, axis=0, keepdims=True))
            alpha = jnp.exp2(m_prev - m_new)
            p = jnp.exp2(s - m_new)
            l_ref[x, i] = alpha * l_ref[x, i] + jnp.sum(p, axis=0, keepdims=True)
            m_ref[x, i] = m_new
            al_ref[x] = alpha
            p_ref[x] = p.astype(BF16)

    def stage_c(i, t):
        vt = vt_ref[:, _blk_slice(t, blk)]
        for x in range(2):
            acc_ref[x, i] = al_ref[x] * acc_ref[x, i] + _dot(vt, p_ref[x])

    _run_pipeline(stage_a, stage_b_diag, stage_c_diag, stage_b, stage_c, tab_ref, nblk)

    lam = (jnp.exp(jnp.sum(lq1_ref[...] * lk1_ref[...], axis=-1, keepdims=True))
           - jnp.exp(jnp.sum(lq2_ref[...] * lk2_ref[...], axis=-1, keepdims=True))
           + lam_init)
    for i in range(nblk):
        o_t = (acc_ref[0, i] * (1.0 / l_ref[0, i])
               - lam * (acc_ref[1, i] * (1.0 / l_ref[1, i])))
        o = o_t.T
        o_ref[_blk_slice(i, blk), :] = (_rms(o, gsub_ref[...]) * (1.0 - lam_init)).astype(BF16)


def _diff_attention(big_n, big_t, pos_icol, pos_irow, slopes_l2e, lq1, lk1, lq2, lk2, g_sub,
                    batch, seq, blk, lam_init):
    nblk = seq // blk
    assert nblk >= 3
    width = DIFF_HEADS * 2 * DIFF_HEAD_DIM
    cb = width // LANE
    k_off = 1 * cb
    q_off, v_off = 0, 1 * cb
    k_tab, q_col = _alibi_lane_tables(slopes_l2e)
    digit_k = np.arange(LANE) % DIFF_HEAD_DIM % 3
    shift_row = jnp.asarray((digit_k * 8).reshape(1, LANE), jnp.int32)
    shift_col = jnp.asarray((digit_k[:DIFF_HEAD_DIM] * 8).reshape(DIFF_HEAD_DIM, 1), jnp.int32)
    small = lambda a: pl.BlockSpec(a.shape, lambda b, h: (0,) * a.ndim)
    return pl.pallas_call(
        functools.partial(_diff_body, blk=blk, nblk=nblk, lam_init=lam_init),
        grid=(batch, DIFF_HEADS),
        in_specs=[
            pl.BlockSpec(memory_space=pltpu.SMEM),
            pl.BlockSpec((LANE, seq), lambda b, h: (q_off + h, b)),
            pl.BlockSpec((seq, LANE), lambda b, h: (b, k_off + h)),
            pl.BlockSpec((LANE, seq), lambda b, h: (v_off + h, b)),
            pl.BlockSpec((seq, 1), lambda b, h: (b, 0)),
            pl.BlockSpec((1, 1, seq), lambda b, h: (b, 0, 0)),
            small(shift_row), small(shift_col),
            pl.BlockSpec((1, 2, LANE), lambda b, h: (h, 0, 0)),
            pl.BlockSpec((1, DIFF_HEAD_DIM, 1), lambda b, h: (h, 0, 0)),
            small(lq1), small(lk1), small(lq2), small(lk2), small(g_sub),
        ],
        out_specs=pl.BlockSpec((seq, LANE), lambda b, h: (b, h)),
        out_shape=jax.ShapeDtypeStruct((batch * seq, width), BF16),
        scratch_shapes=[
            pltpu.VMEM((blk, blk), F32),
            pltpu.VMEM((seq, LANE), BF16),
            pltpu.VMEM((DIFF_HEAD_DIM, seq), BF16),
            pltpu.VMEM((2, LANE, seq), BF16),
            pltpu.VMEM((2, seq, LANE), BF16),
            pltpu.VMEM((2, blk, blk), F32),
            pltpu.VMEM((2, blk, blk), BF16),
            pltpu.VMEM((2, 1, blk), F32),
            pltpu.VMEM((2, nblk, 1, blk), F32),
            pltpu.VMEM((2, nblk, 1, blk), F32),
            pltpu.VMEM((2, nblk, LANE, blk), F32),
        ],
        compiler_params=pltpu.CompilerParams(
            dimension_semantics=("arbitrary", "arbitrary"),
            vmem_limit_bytes=VMEM_LIMIT),
        name="diff_attn",
    )(_pair_table(nblk), big_t, big_n, big_t, pos_icol, pos_irow, shift_row, shift_col, k_tab, q_col,
      lq1, lk1, lq2, lk2, g_sub)


def _outproj_body(x_ref, om_ref, od_ref, gm_ref, gd_ref, w_ref, gpost_ref, o_ref):
    gm = gm_ref[...].astype(F32)
    gd = gd_ref[...].astype(F32)
    mm = (om_ref[...].astype(F32) * (gm * jax.nn.sigmoid(gm))).astype(BF16)
    md = (od_ref[...].astype(F32) * (gd * jax.nn.sigmoid(gd))).astype(BF16)
    half = mm.shape[1]
    y = _dot(mm, w_ref[:half, :]) + _dot(md, w_ref[half:, :])
    o_ref[...] = x_ref[...] + _rms(y, gpost_ref[...])


def _outproj(x2, o_mla, o_diff, big, w_out, g_post, tm):
    m, d = x2.shape
    half = o_mla.shape[1]
    gate_diff_blk = (big.shape[1] - half) // half
    return pl.pallas_call(
        _outproj_body,
        grid=(m // tm,),
        in_specs=[
            pl.BlockSpec((tm, d), lambda i: (i, 0)),
            pl.BlockSpec((tm, half), lambda i: (i, 0)),
            pl.BlockSpec((tm, half), lambda i: (i, 0)),
            pl.BlockSpec((tm, half), lambda i: (i, 0)),
            pl.BlockSpec((tm, half), lambda i: (i, gate_diff_blk)),
            pl.BlockSpec(w_out.shape, lambda i: (0, 0)),
            pl.BlockSpec((1, d), lambda i: (0, 0)),
        ],
        out_specs=pl.BlockSpec((tm, d), lambda i: (i, 0)),
        out_shape=jax.ShapeDtypeStruct((m, d), F32),
        compiler_params=pltpu.CompilerParams(
            dimension_semantics=("arbitrary",),
            vmem_limit_bytes=VMEM_LIMIT),
        name="outproj",
    )(x2, o_mla, o_diff, big, big, w_out, g_post)


def kernel(x, positions, g_pre, w_in, g_q_a, w_q_b, g_kv_a, w_kv_b, lambda_q1, lambda_k1,
           lambda_q2, lambda_k2, g_diff_sub, w_out, g_post):
    batch, seq, d = x.shape
    depth = g_pre.shape[0]
    q_rank = w_q_b.shape[1]
    kv_rank = w_kv_b.shape[1]
    half_rope = MLA_ROPE // 2
    lat_end = q_rank + kv_rank
    pe_end = lat_end + MLA_ROPE

    pos_f = positions.astype(F32)
    pos_col = pos_f.reshape(batch * seq, 1)
    pos_icol = positions.astype(jnp.int32).reshape(batch * seq, 1)
    pos_irow = positions.astype(jnp.int32).reshape(batch, 1, seq)
    freqs = 1.0 / (ROPE_THETA ** (jnp.arange(0, MLA_ROPE, 2, dtype=F32) / MLA_ROPE))
    freq = jnp.tile(freqs, 4).reshape(1, LANE)
    slopes = 2.0 ** (-8.0 * (jnp.arange(DIFF_HEADS, dtype=F32) + 1.0) / DIFF_HEADS)
    slopes_l2e = slopes * LOG2E

    mla_scale = MLA_QK ** -0.5 * LOG2E
    diff_scale = DIFF_HEAD_DIM ** -0.5 * LOG2E
    width = DIFF_HEADS * 2 * DIFF_HEAD_DIM
    g0 = pe_end
    gate_mla_c, dq_c, dk_c, dv_c, gate_diff_c = (slice(g0 + n * width, g0 + (n + 1) * width) for n in range(5))
    inproj_tn = 512
    row_scale = jnp.asarray([diff_scale] * (width // inproj_tn) + [1.0] * (width // inproj_tn), F32)

    x2 = x.reshape(batch * seq, d)
    for l in range(depth):
        lam_init = 0.8 - 0.6 * math.exp(-0.3 * l)
        wi = w_in[l]
        u1 = wi[:, lat_end:lat_end + half_rope]
        u2 = wi[:, lat_end + half_rope:pe_end]
        w_lat = jnp.concatenate([wi[:, :lat_end], u1, u1, u2, u2, u2, u2, u1, u1], axis=1).astype(BF16)
        w_n = jnp.concatenate([wi[:, gate_mla_c], wi[:, dk_c], wi[:, gate_diff_c]], axis=1).astype(BF16)
        w_t = jnp.concatenate([wi[:, dq_c], wi[:, dv_c]], axis=1).T.astype(BF16)
        wq = w_q_b[l]
        t1 = wq[:, :, MLA_NOPE:MLA_NOPE + half_rope]
        t2 = wq[:, :, MLA_NOPE + half_rope:]
        w_q = jnp.concatenate([wq[:, :, :MLA_NOPE], t1, t2, t2, t1], axis=-1)
        w_qt = w_q.reshape(q_rank, MLA_HEADS * QK_PAD).T.astype(BF16)
        wkv = w_kv_b[l]
        w_k = wkv[:, :, :MLA_NOPE].reshape(kv_rank, -1).astype(BF16)
        w_vt = wkv[:, :, MLA_NOPE:].reshape(kv_rank, -1).T.astype(BF16)
        gp = g_pre[l].reshape(1, d)

        big_n, big_t = _inproj(x2, gp, w_n, w_t, row_scale, tm=1024, tn=inproj_tn)
        qt, k, vt = _latent(x2, pos_col, freq, gp, w_lat, g_q_a[l].reshape(1, -1), w_qt,
                            g_kv_a[l].reshape(1, -1), w_k, w_vt, tm=512, q_scale=mla_scale)
        o_mla = _mla_attention(qt, k, vt, batch, seq, blk=512)
        o_diff = _diff_attention(
            big_n, big_t, pos_icol, pos_irow, slopes_l2e,
            lambda_q1[l].reshape(1, -1), lambda_k1[l].reshape(1, -1),
            lambda_q2[l].reshape(1, -1), lambda_k2[l].reshape(1, -1),
            g_diff_sub[l].reshape(1, -1), batch, seq, blk=512, lam_init=lam_init)
        x2 = _outproj(x2, o_mla, o_diff, big_n, w_out[l].astype(BF16), g_post[l].reshape(1, d), tm=512)
    return x2.reshape(batch, seq, d)
```

```python
import functools
import math

import numpy as np
import jax
import jax.numpy as jnp
from jax import lax
from jax.experimental import pallas as pl
from jax.experimental.pallas import tpu as pltpu

F32 = jnp.float32
BF16 = jnp.bfloat16

EPS = 1e-6
LOG2E = 1.4426950408889634
ROPE_THETA = 10000.0

MLA_HEADS = 8
MLA_NOPE = 128
MLA_ROPE = 64
MLA_V = 128
MLA_QK = MLA_NOPE + MLA_ROPE
DIFF_HEADS = 8
DIFF_HEAD_DIM = 64
LANE = 128
QK_PAD = 256
OFF_UNROLL = 2
ALIBI_LANES = 9
ALIBI_BASE = (DIFF_HEAD_DIM, 0)

VMEM_LIMIT = 56 * 1024 * 1024


def _rms(xf, g):
    ms = jnp.mean(xf * xf, axis=-1, keepdims=True)
    return xf * lax.rsqrt(ms + EPS) * g


def _dot(a, b):
    return jnp.dot(a, b, preferred_element_type=F32)


def _dot_nt(a, b):
    return lax.dot_general(a, b, (((1,), (1,)), ((), ())), preferred_element_type=F32)


def _inproj_body(rs_ref, x_ref, g_ref, wn_ref, wt_ref, on_ref, ot_ref, h_ref, *, n_normal):
    j = pl.program_id(1)

    @pl.when(j == 0)
    def _():
        h_ref[...] = _rms(x_ref[...], g_ref[...]).astype(BF16)

    @pl.when(j < n_normal)
    def _():
        on_ref[...] = _dot(h_ref[...], wn_ref[...]).astype(BF16)

    @pl.when(j >= n_normal)
    def _():
        ot_ref[...] = (_dot(h_ref[...], wt_ref[...]) * rs_ref[j - n_normal]).T.astype(BF16)


def _inproj(x2, g_pre, w_n, w_t, row_scale, tm, tn):
    m, d = x2.shape
    n_normal = w_n.shape[1] // tn
    n_trans = w_t.shape[1] // tn
    last_n = n_normal - 1
    col_n = lambda j: jnp.minimum(j, last_n)
    row_t = lambda j: jnp.maximum(j - n_normal, 0)
    return pl.pallas_call(
        functools.partial(_inproj_body, n_normal=n_normal),
        grid=(m // tm, n_normal + n_trans),
        in_specs=[
            pl.BlockSpec(memory_space=pltpu.SMEM),
            pl.BlockSpec((tm, d), lambda i, j: (i, 0)),
            pl.BlockSpec((1, d), lambda i, j: (0, 0)),
            pl.BlockSpec((d, tn), lambda i, j: (0, col_n(j))),
            pl.BlockSpec((d, tn), lambda i, j: (0, row_t(j))),
        ],
        out_specs=[
            pl.BlockSpec((tm, tn), lambda i, j: (i, col_n(j))),
            pl.BlockSpec((tn, tm), lambda i, j: (row_t(j), i)),
        ],
        out_shape=[
            jax.ShapeDtypeStruct((m, w_n.shape[1]), BF16),
            jax.ShapeDtypeStruct((w_t.shape[1], m), BF16),
        ],
        scratch_shapes=[pltpu.VMEM((tm, d), BF16)],
        compiler_params=pltpu.CompilerParams(
            dimension_semantics=("arbitrary", "arbitrary"),
            vmem_limit_bytes=VMEM_LIMIT),
        name="inproj",
    )(row_scale, x2, g_pre, w_n, w_t)


def _latent_body(x_ref, pos_ref, freq_ref, gpre_ref, wlat_ref, gq_ref, wqt_ref,
                 gkv_ref, wk_ref, wvt_ref, qt_ref, k_ref, vt_ref, *, q_scale):
    h = _rms(x_ref[...], gpre_ref[...]).astype(BF16)
    lat = _dot(h, wlat_ref[...])
    ang = pos_ref[...] * freq_ref[...]
    cos = jnp.cos(ang)
    sin = jnp.sin(ang)
    seg = lax.broadcasted_iota(jnp.int32, (1, LANE), 1) // (MLA_ROPE // 2)
    fq = jnp.where(seg == 1, -sin, jnp.where(seg == 3, sin, cos)) * q_scale
    sk = jnp.where(seg < 2, -sin, sin)
    k_ext = (lat[:, 768:896] * cos + lat[:, 896:1024] * sk).astype(BF16)

    c_q = _rms(lat[:, :512], gq_ref[...]).astype(BF16)
    qft = _dot_nt(wqt_ref[...], c_q)
    fqt = fq.T
    c_kv = _rms(lat[:, 512:768], gkv_ref[...]).astype(BF16)
    kf = _dot(c_kv, wk_ref[...])
    vt_ref[...] = _dot_nt(wvt_ref[...], c_kv).astype(BF16)
    for hd in range(MLA_HEADS):
        o = hd * QK_PAD
        qt_ref[o:o + LANE, :] = (qft[o:o + LANE, :] * q_scale).astype(BF16)
        qt_ref[o + LANE:o + QK_PAD, :] = (qft[o + LANE:o + QK_PAD, :] * fqt).astype(BF16)
        k_ref[:, o:o + LANE] = kf[:, hd * LANE:(hd + 1) * LANE].astype(BF16)
        k_ref[:, o + LANE:o + QK_PAD] = k_ext


def _latent(x2, pos_col, freq, g_pre, w_lat, g_q, w_qt, g_kv, w_k, w_vt, tm, q_scale):
    m, d = x2.shape
    nq = MLA_HEADS * QK_PAD
    nv = MLA_HEADS * MLA_V
    full = lambda a: pl.BlockSpec(a.shape, lambda i: (0,) * a.ndim)
    return pl.pallas_call(
        functools.partial(_latent_body, q_scale=q_scale),
        grid=(m // tm,),
        in_specs=[
            pl.BlockSpec((tm, d), lambda i: (i, 0)),
            pl.BlockSpec((tm, 1), lambda i: (i, 0)),
            full(freq), full(g_pre), full(w_lat), full(g_q), full(w_qt), full(g_kv), full(w_k),
            full(w_vt),
        ],
        out_specs=[
            pl.BlockSpec((nq, tm), lambda i: (0, i)),
            pl.BlockSpec((tm, nq), lambda i: (i, 0)),
            pl.BlockSpec((nv, tm), lambda i: (0, i)),
        ],
        out_shape=[
            jax.ShapeDtypeStruct((nq, m), BF16),
            jax.ShapeDtypeStruct((m, nq), BF16),
            jax.ShapeDtypeStruct((nv, m), BF16),
        ],
        compiler_params=pltpu.CompilerParams(
            dimension_semantics=("arbitrary",),
            vmem_limit_bytes=VMEM_LIMIT),
        name="latent",
    )(x2, pos_col, freq, g_pre, w_lat, g_q, w_qt, g_kv, w_k, w_vt)


def _blk_slice(idx, blk):
    if isinstance(idx, int):
        return slice(idx * blk, (idx + 1) * blk)
    return pl.ds(pl.multiple_of(idx * blk, blk), blk)


def _offdiag_pairs(nblk):
    return [(i, t) for i in range(1, nblk) for t in range(i)]


def _pair_table(nblk):
    return jnp.asarray(np.array(_offdiag_pairs(nblk), dtype=np.int32).T)


def _init_mask_bias(mb_ref, blk):
    kv = lax.broadcasted_iota(jnp.int32, (blk, blk), 0)
    qi = lax.broadcasted_iota(jnp.int32, (blk, blk), 1)
    mb_ref[...] = jnp.where(kv <= qi, 0.0, -jnp.inf).astype(F32)


def _run_pipeline(stage_a, stage_b_diag, stage_c_diag, stage_b, stage_c, tab_ref, nblk):
    pairs = _offdiag_pairs(nblk)
    npairs = len(pairs)
    stage_a(0, 0)
    stage_b_diag(0)
    stage_a(1, 1)

    def diag_loop(f, carry):
        stage_c_diag(f - 1)
        stage_b_diag(f)
        stage_a(f + 1, f + 1)
        return carry

    lax.fori_loop(1, nblk - 1, diag_loop, 0, unroll=2)

    stage_c_diag(nblk - 2)
    stage_b_diag(nblk - 1)
    stage_a(*pairs[0])
    stage_c_diag(nblk - 1)
    stage_b(*pairs[0])
    stage_a(*pairs[1])

    def off_loop(f, carry):
        stage_c(tab_ref[0, f - 1], tab_ref[1, f - 1])
        stage_b(tab_ref[0, f], tab_ref[1, f])
        stage_a(tab_ref[0, f + 1], tab_ref[1, f + 1])
        return carry

    main = (npairs - 2) // OFF_UNROLL * OFF_UNROLL
    lax.fori_loop(1, 1 + main, off_loop, 0, unroll=OFF_UNROLL)
    for f in range(1 + main, npairs - 1):
        stage_c(*pairs[f - 1])
        stage_b(*pairs[f])
        stage_a(*pairs[f + 1])

    stage_c(*pairs[npairs - 2])
    stage_b(*pairs[npairs - 1])
    stage_c(*pairs[npairs - 1])


def _mla_stages(qt_ref, k_ref, vt_ref, o_ref, mb_ref, s_ref, p_ref, al_ref, m_ref, l_ref, acc_ref,
                blk, nblk):
    def stage_a(i, t):
        s_ref[...] = _dot(k_ref[_blk_slice(t, blk), :], qt_ref[:, _blk_slice(i, blk)])

    def stage_b_diag(i):
        s = s_ref[...] + mb_ref[...]
        m = jnp.max(s, axis=0, keepdims=True)
        p = jnp.exp2(s - m)
        m_ref[i] = m
        l_ref[i] = jnp.sum(p, axis=0, keepdims=True)
        p_ref[...] = p.astype(BF16)

    def stage_c_diag(i):
        acc_ref[i] = _dot(vt_ref[:, _blk_slice(i, blk)], p_ref[...])

    def stage_b(i, t):
        s = s_ref[...]
        m_prev = m_ref[i]
        m_new = jnp.maximum(m_prev, jnp.max(s, axis=0, keepdims=True))
        alpha = jnp.exp2(m_prev - m_new)
        p = jnp.exp2(s - m_new)
        l_ref[i] = alpha * l_ref[i] + jnp.sum(p, axis=0, keepdims=True)
        m_ref[i] = m_new
        al_ref[...] = alpha
        p_ref[...] = p.astype(BF16)

    def stage_c(i, t):
        acc_ref[i] = al_ref[...] * acc_ref[i] + _dot(vt_ref[:, _blk_slice(t, blk)], p_ref[...])

    def finalize():
        for i in range(nblk):
            o_t = acc_ref[i] * (1.0 / l_ref[i])
            o_ref[_blk_slice(i, blk), :] = o_t.T.astype(BF16)

    return (stage_a, stage_b_diag, stage_c_diag, stage_b, stage_c), finalize


def _mla_scratch(blk, nblk):
    return [
        pltpu.VMEM((blk, blk), F32),
        pltpu.VMEM((blk, blk), BF16),
        pltpu.VMEM((1, blk), F32),
        pltpu.VMEM((nblk, 1, blk), F32),
        pltpu.VMEM((nblk, 1, blk), F32),
        pltpu.VMEM((nblk, MLA_V, blk), F32),
    ]


def _alibi_lane_tables(slopes_l2e):
    c1 = slopes_l2e.astype(BF16).astype(F32)
    c2 = (slopes_l2e - c1).astype(BF16).astype(F32)
    c3 = (slopes_l2e - c1 - c2).astype(BF16).astype(F32)
    cw = (jnp.stack([c1, c2, c3], axis=1)[:, :, None]
          * jnp.asarray([1.0, 256.0, 65536.0], F32)[None, None, :]).reshape(-1, ALIBI_LANES)
    heads = slopes_l2e.shape[0]
    ktab = jnp.zeros((heads, 2, LANE), F32)
    for x, base in enumerate(ALIBI_BASE):
        ktab = ktab.at[:, x, base:base + ALIBI_LANES].set(-cw)
    qcol = jnp.zeros((heads, DIFF_HEAD_DIM), F32).at[:, ALIBI_LANES:2 * ALIBI_LANES].set(cw)
    return ktab.astype(BF16), qcol.reshape(heads, DIFF_HEAD_DIM, 1)


def _diff_stages(qt_ref, k_ref, vt_ref, pc_ref, pr_ref, shr_ref, shc_ref, kt_ref, qc_ref,
                 lq1_ref, lk1_ref, lq2_ref, lk2_ref, gsub_ref, o_ref, mb_ref, dig_ref, digt_ref,
                 qm_ref, km_ref, s_ref, p_ref, al_ref, m_ref, l_ref, acc_ref, blk, nblk, lam_init):
    half = DIFF_HEAD_DIM

    @pl.when(pl.program_id(1) == 0)
    def _():
        pcol = pc_ref[...]
        prel = jnp.broadcast_to(pcol - pcol[0:1, :], dig_ref.shape)
        shift = jnp.broadcast_to(shr_ref[...], dig_ref.shape)
        dig_ref[...] = (lax.shift_right_logical(prel, shift) & 255).astype(F32).astype(BF16)
        prow = pr_ref[0]
        prel_t = jnp.broadcast_to(prow - prow[:, 0:1], digt_ref.shape)
        shift_t = jnp.broadcast_to(shc_ref[...], digt_ref.shape)
        digt_ref[...] = (lax.shift_right_logical(prel_t, shift_t) & 255).astype(F32).astype(BF16)

    row = lax.broadcasted_iota(jnp.int32, (half, 1), 0)
    alibi_q = jnp.where(row < ALIBI_LANES, digt_ref[...],
                        jnp.broadcast_to(qc_ref[0], digt_ref.shape).astype(BF16))
    qm_ref[0, :half, :] = qt_ref[:half, :]
    qm_ref[0, half:, :] = alibi_q
    qm_ref[1, :half, :] = alibi_q
    qm_ref[1, half:, :] = qt_ref[half:, :]
    lane = lax.broadcasted_iota(jnp.int32, (1, LANE), 1)
    dig = dig_ref[...]
    k = k_ref[...]
    for x, base in enumerate(ALIBI_BASE):
        own = (lane < half) if x == 0 else (lane >= half)
        k_digit = (lane >= base + ALIBI_LANES) & (lane < base + 2 * ALIBI_LANES)
        km_ref[x] = jnp.where(own, k, jnp.where(k_digit, dig, kt_ref[0, x:x + 1, :]))

    def stage_a(i, t):
        for x in range(2):
            s_ref[x] = _dot(km_ref[x, _blk_slice(t, blk), :], qm_ref[x, :, _blk_slice(i, blk)])

    def stage_b_diag(i):
        nb = mb_ref[...]
        for x in range(2):
            s = s_ref[x] + nb
            m = jnp.max(s, axis=0, keepdims=True)
            p = jnp.exp2(s - m)
            m_ref[x, i] = m
            l_ref[x, i] = jnp.sum(p, axis=0, keepdims=True)
            p_ref[x] = p.astype(BF16)

    def stage_c_diag(i):
        vt = vt_ref[:, _blk_slice(i, blk)]
        for x in range(2):
            acc_ref[x, i] = _dot(vt, p_ref[x])

    def stage_b(i, t):
        for x in range(2):
            s = s_ref[x]
            m_prev = m_ref[x, i]
            m_new = jnp.maximum(m_prev, jnp.max(s, axis=0, keepdims=True))
            alpha = jnp.exp2(m_prev - m_new)
            p = jnp.exp2(s - m_new)
            l_ref[x, i] = alpha * l_ref[x, i] + jnp.sum(p, axis=0, keepdims=True)
            m_ref[x, i] = m_new
            al_ref[x] = alpha
            p_ref[x] = p.astype(BF16)

    def stage_c(i, t):
        vt = vt_ref[:, _blk_slice(t, blk)]
        for x in range(2):
            acc_ref[x, i] = al_ref[x] * acc_ref[x, i] + _dot(vt, p_ref[x])

    def finalize():
        lam = (jnp.exp(jnp.sum(lq1_ref[...] * lk1_ref[...], axis=-1, keepdims=True))
               - jnp.exp(jnp.sum(lq2_ref[...] * lk2_ref[...], axis=-1, keepdims=True))
               + lam_init)
        for i in range(nblk):
            o_t = (acc_ref[0, i] * (1.0 / l_ref[0, i])
                   - lam * (acc_ref[1, i] * (1.0 / l_ref[1, i])))
            o = o_t.T
            o_ref[_blk_slice(i, blk), :] = (_rms(o, gsub_ref[...]) * (1.0 - lam_init)).astype(BF16)

    return (stage_a, stage_b_diag, stage_c_diag, stage_b, stage_c), finalize


def _diff_scratch(seq, blk, nblk):
    return [
        pltpu.VMEM((seq, LANE), BF16),
        pltpu.VMEM((DIFF_HEAD_DIM, seq), BF16),
        pltpu.VMEM((2, LANE, seq), BF16),
        pltpu.VMEM((2, seq, LANE), BF16),
        pltpu.VMEM((2, blk, blk), F32),
        pltpu.VMEM((2, blk, blk), BF16),
        pltpu.VMEM((2, 1, blk), F32),
        pltpu.VMEM((2, nblk, 1, blk), F32),
        pltpu.VMEM((2, nblk, 1, blk), F32),
        pltpu.VMEM((2, nblk, LANE, blk), F32),
    ]


N_MLA_IN, N_DIFF_IN = 3, 14
N_MLA_SCRATCH, N_DIFF_SCRATCH = 6, 10


def _attention_body(tab_ref, *refs, blk, nblk, lam_init):
    mla_in, refs = refs[:N_MLA_IN], refs[N_MLA_IN:]
    diff_in, refs = refs[:N_DIFF_IN], refs[N_DIFF_IN:]
    (o_mla_ref, o_diff_ref, mb_ref), refs = refs[:3], refs[3:]
    mla_scr, diff_scr = refs[:N_MLA_SCRATCH], refs[N_MLA_SCRATCH:]
    assert len(diff_scr) == N_DIFF_SCRATCH

    @pl.when((pl.program_id(0) == 0) & (pl.program_id(1) == 0))
    def _():
        _init_mask_bias(mb_ref, blk)

    mla, mla_fin = _mla_stages(*mla_in, o_mla_ref, mb_ref, *mla_scr, blk, nblk)
    diff, diff_fin = _diff_stages(*diff_in, o_diff_ref, mb_ref, *diff_scr, blk, nblk, lam_init)

    def both(f, g):
        def stage(*idx):
            f(*idx)
            g(*idx)
        return stage

    _run_pipeline(*(both(f, g) for f, g in zip(mla, diff)), tab_ref, nblk)
    mla_fin()
    diff_fin()


def _attention(qt, k, vt, big_n, big_t, pos_icol, pos_irow, slopes_l2e, lq1, lk1, lq2, lk2, g_sub,
               batch, seq, blk, lam_init):
    assert MLA_HEADS == DIFF_HEADS
    nblk = seq // blk
    assert nblk >= 3
    width = DIFF_HEADS * 2 * DIFF_HEAD_DIM
    cb = width // LANE
    k_off = 1 * cb
    q_off, v_off = 0, 1 * cb
    k_tab, q_col = _alibi_lane_tables(slopes_l2e)
    digit_k = np.arange(LANE) % DIFF_HEAD_DIM % 3
    shift_row = jnp.asarray((digit_k * 8).reshape(1, LANE), jnp.int32)
    shift_col = jnp.asarray((digit_k[:DIFF_HEAD_DIM] * 8).reshape(DIFF_HEAD_DIM, 1), jnp.int32)
    small = lambda a: pl.BlockSpec(a.shape, lambda b, h: (0,) * a.ndim)
    mla_specs = [
        pl.BlockSpec((QK_PAD, seq), lambda b, h: (h, b)),
        pl.BlockSpec((seq, QK_PAD), lambda b, h: (b, h)),
        pl.BlockSpec((MLA_V, seq), lambda b, h: (h, b)),
    ]
    diff_specs = [
        pl.BlockSpec((LANE, seq), lambda b, h: (q_off + h, b)),
        pl.BlockSpec((seq, LANE), lambda b, h: (b, k_off + h)),
        pl.BlockSpec((LANE, seq), lambda b, h: (v_off + h, b)),
        pl.BlockSpec((seq, 1), lambda b, h: (b, 0)),
        pl.BlockSpec((1, 1, seq), lambda b, h: (b, 0, 0)),
        small(shift_row), small(shift_col),
        pl.BlockSpec((1, 2, LANE), lambda b, h: (h, 0, 0)),
        pl.BlockSpec((1, DIFF_HEAD_DIM, 1), lambda b, h: (h, 0, 0)),
        small(lq1), small(lk1), small(lq2), small(lk2), small(g_sub),
    ]
    assert len(mla_specs) == N_MLA_IN and len(diff_specs) == N_DIFF_IN
    out_spec = pl.BlockSpec((seq, LANE), lambda b, h: (b, h))
    return pl.pallas_call(
        functools.partial(_attention_body, blk=blk, nblk=nblk, lam_init=lam_init),
        grid=(batch, MLA_HEADS),
        in_specs=[pl.BlockSpec(memory_space=pltpu.SMEM)] + mla_specs + diff_specs,
        out_specs=[out_spec, out_spec],
        out_shape=[jax.ShapeDtypeStruct((batch * seq, MLA_HEADS * MLA_V), BF16),
                   jax.ShapeDtypeStruct((batch * seq, width), BF16)],
        scratch_shapes=([pltpu.VMEM((blk, blk), F32)]
                        + _mla_scratch(blk, nblk) + _diff_scratch(seq, blk, nblk)),
        compiler_params=pltpu.CompilerParams(
            dimension_semantics=("arbitrary", "arbitrary"),
            vmem_limit_bytes=VMEM_LIMIT),
        name="attention",
    )(_pair_table(nblk), qt, k, vt, big_t, big_n, big_t, pos_icol, pos_irow, shift_row, shift_col,
      k_tab, q_col, lq1, lk1, lq2, lk2, g_sub)


def _outproj_body(x_ref, om_ref, od_ref, gm_ref, gd_ref, w_ref, gpost_ref, o_ref):
    gm = gm_ref[...].astype(F32)
    gd = gd_ref[...].astype(F32)
    mm = (om_ref[...].astype(F32) * (gm * jax.nn.sigmoid(gm))).astype(BF16)
    md = (od_ref[...].astype(F32) * (gd * jax.nn.sigmoid(gd))).astype(BF16)
    half = mm.shape[1]
    y = _dot(mm, w_ref[:half, :]) + _dot(md, w_ref[half:, :])
    o_ref[...] = x_ref[...] + _rms(y, gpost_ref[...])


def _outproj(x2, o_mla, o_diff, big, w_out, g_post, tm):
    m, d = x2.shape
    half = o_mla.shape[1]
    gate_diff_blk = (big.shape[1] - half) // half
    return pl.pallas_call(
        _outproj_body,
        grid=(m // tm,),
        in_specs=[
            pl.BlockSpec((tm, d), lambda i: (i, 0)),
            pl.BlockSpec((tm, half), lambda i: (i, 0)),
            pl.BlockSpec((tm, half), lambda i: (i, 0)),
            pl.BlockSpec((tm, half), lambda i: (i, 0)),
            pl.BlockSpec((tm, half), lambda i: (i, gate_diff_blk)),
            pl.BlockSpec(w_out.shape, lambda i: (0, 0)),
            pl.BlockSpec((1, d), lambda i: (0, 0)),
        ],
        out_specs=pl.BlockSpec((tm, d), lambda i: (i, 0)),
        out_shape=jax.ShapeDtypeStruct((m, d), F32),
        compiler_params=pltpu.CompilerParams(
            dimension_semantics=("arbitrary",),
            vmem_limit_bytes=VMEM_LIMIT),
        name="outproj",
    )(x2, o_mla, o_diff, big, big, w_out, g_post)


def kernel(x, positions, g_pre, w_in, g_q_a, w_q_b, g_kv_a, w_kv_b, lambda_q1, lambda_k1,
           lambda_q2, lambda_k2, g_diff_sub, w_out, g_post):
    batch, seq, d = x.shape
    depth = g_pre.shape[0]
    q_rank = w_q_b.shape[1]
    kv_rank = w_kv_b.shape[1]
    half_rope = MLA_ROPE // 2
    lat_end = q_rank + kv_rank
    pe_end = lat_end + MLA_ROPE

    pos_f = positions.astype(F32)
    pos_col = pos_f.reshape(batch * seq, 1)
    pos_icol = positions.astype(jnp.int32).reshape(batch * seq, 1)
    pos_irow = positions.astype(jnp.int32).reshape(batch, 1, seq)
    freqs = 1.0 / (ROPE_THETA ** (jnp.arange(0, MLA_ROPE, 2, dtype=F32) / MLA_ROPE))
    freq = jnp.tile(freqs, 4).reshape(1, LANE)
    slopes = 2.0 ** (-8.0 * (jnp.arange(DIFF_HEADS, dtype=F32) + 1.0) / DIFF_HEADS)
    slopes_l2e = slopes * LOG2E

    mla_scale = MLA_QK ** -0.5 * LOG2E
    diff_scale = DIFF_HEAD_DIM ** -0.5 * LOG2E
    width = DIFF_HEADS * 2 * DIFF_HEAD_DIM
    g0 = pe_end
    gate_mla_c, dq_c, dk_c, dv_c, gate_diff_c = (slice(g0 + n * width, g0 + (n + 1) * width) for n in range(5))
    inproj_tn = 512
    row_scale = jnp.asarray([diff_scale] * (width // inproj_tn) + [1.0] * (width // inproj_tn), F32)

    x2 = x.reshape(batch * seq, d)
    for l in range(depth):
        lam_init = 0.8 - 0.6 * math.exp(-0.3 * l)
        wi = w_in[l]
        u1 = wi[:, lat_end:lat_end + half_rope]
        u2 = wi[:, lat_end + half_rope:pe_end]
        w_lat = jnp.concatenate([wi[:, :lat_end], u1, u1, u2, u2, u2, u2, u1, u1], axis=1).astype(BF16)
        w_n = jnp.concatenate([wi[:, gate_mla_c], wi[:, dk_c], wi[:, gate_diff_c]], axis=1).astype(BF16)
        w_t = jnp.concatenate([wi[:, dq_c], wi[:, dv_c]], axis=1).astype(BF16)
        wq = w_q_b[l]
        t1 = wq[:, :, MLA_NOPE:MLA_NOPE + half_rope]
        t2 = wq[:, :, MLA_NOPE + half_rope:]
        w_q = jnp.concatenate([wq[:, :, :MLA_NOPE], t1, t2, t2, t1], axis=-1)
        w_qt = w_q.reshape(q_rank, MLA_HEADS * QK_PAD).T.astype(BF16)
        wkv = w_kv_b[l]
        w_k = wkv[:, :, :MLA_NOPE].reshape(kv_rank, -1).astype(BF16)
        w_vt = wkv[:, :, MLA_NOPE:].reshape(kv_rank, -1).T.astype(BF16)
        gp = g_pre[l].reshape(1, d)

        big_n, big_t = _inproj(x2, gp, w_n, w_t, row_scale, tm=1024, tn=inproj_tn)
        qt, k, vt = _latent(x2, pos_col, freq, gp, w_lat, g_q_a[l].reshape(1, -1), w_qt,
                            g_kv_a[l].reshape(1, -1), w_k, w_vt, tm=512, q_scale=mla_scale)
        o_mla, o_diff = _attention(
            qt, k, vt, big_n, big_t, pos_icol, pos_irow, slopes_l2e,
            lambda_q1[l].reshape(1, -1), lambda_k1[l].reshape(1, -1),
            lambda_q2[l].reshape(1, -1), lambda_k2[l].reshape(1, -1),
            g_diff_sub[l].reshape(1, -1), batch, seq, blk=512, lam_init=lam_init)
        x2 = _outproj(x2, o_mla, o_diff, big_n, w_out[l].astype(BF16), g_post[l].reshape(1, d), tm=512)
    return x2.reshape(batch, seq, d)
```

```python
import functools
import math

import numpy as np
import jax
import jax.numpy as jnp
from jax import lax
from jax.experimental import pallas as pl
from jax.experimental.pallas import tpu as pltpu

F32 = jnp.float32
BF16 = jnp.bfloat16

EPS = 1e-6
LOG2E = 1.4426950408889634
ROPE_THETA = 10000.0

MLA_HEADS = 8
MLA_NOPE = 128
MLA_ROPE = 64
MLA_V = 128
MLA_QK = MLA_NOPE + MLA_ROPE
DIFF_HEADS = 8
DIFF_HEAD_DIM = 64
LANE = 128
QK_PAD = 256
ALIBI_LANES = 9
ALIBI_BASE = (DIFF_HEAD_DIM, 0)

VMEM_LIMIT = 56 * 1024 * 1024


def _rms(xf, g):
    ms = jnp.mean(xf * xf, axis=-1, keepdims=True)
    return xf * lax.rsqrt(ms + EPS) * g


def _dot(a, b):
    return jnp.dot(a, b, preferred_element_type=F32)


def _dot_nt(a, b):
    return lax.dot_general(a, b, (((1,), (1,)), ((), ())), preferred_element_type=F32)


def _inproj_body(rs_ref, x_ref, g_ref, wn_ref, wt_ref, on_ref, ot_ref, *, tn):
    h = _rms(x_ref[...], g_ref[...]).astype(BF16)
    for c in range(wn_ref.shape[1] // tn):
        cols = slice(c * tn, (c + 1) * tn)
        on_ref[:, cols] = _dot(h, wn_ref[:, cols]).astype(BF16)
    for c in range(wt_ref.shape[1] // tn):
        cols = slice(c * tn, (c + 1) * tn)
        ot_ref[cols, :] = (_dot(h, wt_ref[:, cols]) * rs_ref[c]).T.astype(BF16)


def _inproj(x2, g_pre, w_n, w_t, row_scale, tm, tn):
    m, d = x2.shape
    resident = lambda a: pl.BlockSpec(a.shape, lambda i: (0,) * a.ndim, pipeline_mode=pl.Buffered(1))
    return pl.pallas_call(
        functools.partial(_inproj_body, tn=tn),
        grid=(m // tm,),
        in_specs=[
            pl.BlockSpec(memory_space=pltpu.SMEM),
            pl.BlockSpec((tm, d), lambda i: (i, 0)),
            pl.BlockSpec((1, d), lambda i: (0, 0)),
            resident(w_n),
            resident(w_t),
        ],
        out_specs=[
            pl.BlockSpec((tm, w_n.shape[1]), lambda i: (i, 0)),
            pl.BlockSpec((w_t.shape[1], tm), lambda i: (0, i)),
        ],
        out_shape=[
            jax.ShapeDtypeStruct((m, w_n.shape[1]), BF16),
            jax.ShapeDtypeStruct((w_t.shape[1], m), BF16),
        ],
        compiler_params=pltpu.CompilerParams(
            dimension_semantics=("arbitrary",),
            vmem_limit_bytes=VMEM_LIMIT),
        name="inproj",
    )(row_scale, x2, g_pre, w_n, w_t)


def _latent_body(x_ref, pos_ref, freq_ref, gpre_ref, wlat_ref, gq_ref, wqt_ref,
                 gkv_ref, wk_ref, wvt_ref, qt_ref, k_ref, vt_ref, *, q_scale):
    h = _rms(x_ref[...], gpre_ref[...]).astype(BF16)
    lat = _dot(h, wlat_ref[...])
    ang = pos_ref[...].astype(F32) * freq_ref[...]
    cos = jnp.cos(ang)
    sin = jnp.sin(ang)
    seg = lax.broadcasted_iota(jnp.int32, (1, LANE), 1) // (MLA_ROPE // 2)
    fq = jnp.where(seg == 1, -sin, jnp.where(seg == 3, sin, cos)) * q_scale
    sk = jnp.where(seg < 2, -sin, sin)
    k_ext = (lat[:, 768:896] * cos + lat[:, 896:1024] * sk).astype(BF16)

    c_q = _rms(lat[:, :512], gq_ref[...]).astype(BF16)
    qft = _dot_nt(wqt_ref[...], c_q)
    fqt = fq.T
    c_kv = _rms(lat[:, 512:768], gkv_ref[...]).astype(BF16)
    kf = _dot(c_kv, wk_ref[...])
    vt_ref[...] = _dot_nt(wvt_ref[...], c_kv).astype(BF16)
    for hd in range(MLA_HEADS):
        o = hd * QK_PAD
        qt_ref[o:o + LANE, :] = (qft[o:o + LANE, :] * q_scale).astype(BF16)
        qt_ref[o + LANE:o + QK_PAD, :] = (qft[o + LANE:o + QK_PAD, :] * fqt).astype(BF16)
        k_ref[:, o:o + LANE] = kf[:, hd * LANE:(hd + 1) * LANE].astype(BF16)
        k_ref[:, o + LANE:o + QK_PAD] = k_ext


def _latent(x2, pos_col, freq, g_pre, w_lat, g_q, w_qt, g_kv, w_k, w_vt, tm, q_scale):
    m, d = x2.shape
    nq = MLA_HEADS * QK_PAD
    nv = MLA_HEADS * MLA_V
    full = lambda a: pl.BlockSpec(a.shape, lambda i: (0,) * a.ndim)
    return pl.pallas_call(
        functools.partial(_latent_body, q_scale=q_scale),
        grid=(m // tm,),
        in_specs=[
            pl.BlockSpec((tm, d), lambda i: (i, 0)),
            pl.BlockSpec((tm, 1), lambda i: (i, 0)),
            full(freq), full(g_pre), full(w_lat), full(g_q), full(w_qt), full(g_kv), full(w_k),
            full(w_vt),
        ],
        out_specs=[
            pl.BlockSpec((nq, tm), lambda i: (0, i)),
            pl.BlockSpec((tm, nq), lambda i: (i, 0)),
            pl.BlockSpec((nv, tm), lambda i: (0, i)),
        ],
        out_shape=[
            jax.ShapeDtypeStruct((nq, m), BF16),
            jax.ShapeDtypeStruct((m, nq), BF16),
            jax.ShapeDtypeStruct((nv, m), BF16),
        ],
        compiler_params=pltpu.CompilerParams(
            dimension_semantics=("arbitrary",),
            vmem_limit_bytes=VMEM_LIMIT),
        name="latent",
    )(x2, pos_col, freq, g_pre, w_lat, g_q, w_qt, g_kv, w_k, w_vt)


def _blk_slice(idx, blk):
    if isinstance(idx, int):
        return slice(idx * blk, (idx + 1) * blk)
    return pl.ds(pl.multiple_of(idx * blk, blk), blk)


def _offdiag_pairs(nblk):
    return [(i, t) for i in range(1, nblk) for t in range(i)]


def _pair_table(nblk):
    return jnp.asarray(np.array(_offdiag_pairs(nblk), dtype=np.int32).T)


def _init_mask_bias(mb_ref, blk):
    kv = lax.broadcasted_iota(jnp.int32, (blk, blk), 0)
    qi = lax.broadcasted_iota(jnp.int32, (blk, blk), 1)
    mb_ref[...] = jnp.where(kv <= qi, 0.0, -jnp.inf).astype(F32)


def _run_pipeline(heads, tab_ref, nblk):
    A, B_DIAG, C_DIAG, B, C = range(5)
    pairs = _offdiag_pairs(nblk)
    npairs = len(pairs)
    assert (nblk - 2) % 2 == 0 and (npairs - 2) % 2 == 0

    def step(c=None, b=None, a=None):
        for stages in heads:
            if c is not None:
                stages[c[0]](*c[1])
            if b is not None:
                stages[b[0]](*b[1])
            if a is not None:
                stages[A](*a)

    step(a=(0, 0))
    step(b=(B_DIAG, (0,)), a=(1, 1))

    def diag_loop(u, carry):
        for d in (1, 2):
            g = d + 2 * u
            step(c=(C_DIAG, (g - 1,)), b=(B_DIAG, (g,)), a=(g + 1, g + 1))
        return carry

    lax.fori_loop(0, (nblk - 2) // 2, diag_loop, 0)

    step(c=(C_DIAG, (nblk - 2,)), b=(B_DIAG, (nblk - 1,)), a=pairs[0])
    step(c=(C_DIAG, (nblk - 1,)), b=(B, pairs[0]), a=pairs[1])

    def off_loop(u, carry):
        pair = lambda f: (tab_ref[0, f], tab_ref[1, f])
        for d in (1, 2):
            f = d + 2 * u
            step(c=(C, pair(f - 1)), b=(B, pair(f)), a=pair(f + 1))
        return carry

    lax.fori_loop(0, (npairs - 2) // 2, off_loop, 0)

    step(c=(C, pairs[npairs - 2]), b=(B, pairs[npairs - 1]))
    step(c=(C, pairs[npairs - 1]))


def _mla_stages(qt_ref, k_ref, vt_ref, o_ref, mb_ref, s_ref, p_ref, al_ref, m_ref, l_ref, acc_ref,
                blk, nblk):
    def stage_a(i, t):
        s_ref[...] = _dot(k_ref[_blk_slice(t, blk), :], qt_ref[:, _blk_slice(i, blk)])

    def stage_b_diag(i):
        s = s_ref[...] + mb_ref[...]
        m = jnp.max(s, axis=0, keepdims=True)
        p = jnp.exp2(s - m)
        m_ref[i] = m
        l_ref[i] = jnp.sum(p, axis=0, keepdims=True)
        p_ref[...] = p.astype(BF16)

    def stage_c_diag(i):
        acc_ref[i] = _dot(vt_ref[:, _blk_slice(i, blk)], p_ref[...])

    def stage_b(i, t):
        s = s_ref[...]
        m_prev = m_ref[i]
        m_new = jnp.maximum(m_prev, jnp.max(s, axis=0, keepdims=True))
        alpha = jnp.exp2(m_prev - m_new)
        p = jnp.exp2(s - m_new)
        l_ref[i] = alpha * l_ref[i] + jnp.sum(p, axis=0, keepdims=True)
        m_ref[i] = m_new
        al_ref[...] = alpha
        p_ref[...] = p.astype(BF16)

    def stage_c(i, t):
        acc_ref[i] = al_ref[...] * acc_ref[i] + _dot(vt_ref[:, _blk_slice(t, blk)], p_ref[...])

    def finalize():
        for i in range(nblk):
            o_t = acc_ref[i] * (1.0 / l_ref[i])
            o_ref[_blk_slice(i, blk), :] = o_t.T.astype(BF16)

    return (stage_a, stage_b_diag, stage_c_diag, stage_b, stage_c), finalize


def _mla_scratch(blk, nblk):
    return [
        pltpu.VMEM((blk, blk), F32),
        pltpu.VMEM((blk, blk), BF16),
        pltpu.VMEM((1, blk), F32),
        pltpu.VMEM((nblk, 1, blk), F32),
        pltpu.VMEM((nblk, 1, blk), F32),
        pltpu.VMEM((nblk, MLA_V, blk), F32),
    ]


def _alibi_lane_tables(slopes_l2e):
    c1 = slopes_l2e.astype(BF16).astype(F32)
    c2 = (slopes_l2e - c1).astype(BF16).astype(F32)
    c3 = (slopes_l2e - c1 - c2).astype(BF16).astype(F32)
    cw = (jnp.stack([c1, c2, c3], axis=1)[:, :, None]
          * jnp.asarray([1.0, 256.0, 65536.0], F32)[None, None, :]).reshape(-1, ALIBI_LANES)
    heads = slopes_l2e.shape[0]
    ktab = jnp.zeros((heads, 2, LANE), F32)
    for x, base in enumerate(ALIBI_BASE):
        ktab = ktab.at[:, x, base:base + ALIBI_LANES].set(-cw)
    qcol = jnp.zeros((heads, DIFF_HEAD_DIM), F32).at[:, ALIBI_LANES:2 * ALIBI_LANES].set(cw)
    return ktab.astype(BF16), qcol.reshape(heads, DIFF_HEAD_DIM, 1)


def _diff_stages(qt_ref, k_ref, vt_ref, pc_ref, pr_ref, shr_ref, shc_ref, kt_ref, qc_ref,
                 lq1_ref, lk1_ref, lq2_ref, lk2_ref, gsub_ref, o_ref, mb_ref, dig_ref, digt_ref,
                 qm_ref, km_ref, s_ref, p_ref, al_ref, m_ref, l_ref, acc_ref, blk, nblk, lam_init):
    half = DIFF_HEAD_DIM

    @pl.when(pl.program_id(1) == 0)
    def _():
        pcol = pc_ref[...]
        prel = jnp.broadcast_to(pcol - pcol[0:1, :], dig_ref.shape)
        shift = jnp.broadcast_to(shr_ref[...], dig_ref.shape)
        dig_ref[...] = (lax.shift_right_logical(prel, shift) & 255).astype(F32).astype(BF16)
        prow = pr_ref[0]
        prel_t = jnp.broadcast_to(prow - prow[:, 0:1], digt_ref.shape)
        shift_t = jnp.broadcast_to(shc_ref[...], digt_ref.shape)
        digt_ref[...] = (lax.shift_right_logical(prel_t, shift_t) & 255).astype(F32).astype(BF16)

    row = lax.broadcasted_iota(jnp.int32, (half, 1), 0)
    alibi_q = jnp.where(row < ALIBI_LANES, digt_ref[...],
                        jnp.broadcast_to(qc_ref[0], digt_ref.shape).astype(BF16))
    qm_ref[0, :half, :] = qt_ref[:half, :]
    qm_ref[0, half:, :] = alibi_q
    qm_ref[1, :half, :] = alibi_q
    qm_ref[1, half:, :] = qt_ref[half:, :]
    lane = lax.broadcasted_iota(jnp.int32, (1, LANE), 1)
    dig = dig_ref[...]
    k = k_ref[...]
    for x, base in enumerate(ALIBI_BASE):
        own = (lane < half) if x == 0 else (lane >= half)
        k_digit = (lane >= base + ALIBI_LANES) & (lane < base + 2 * ALIBI_LANES)
        km_ref[x] = jnp.where(own, k, jnp.where(k_digit, dig, kt_ref[0, x:x + 1, :]))

    def stage_a(i, t):
        for x in range(2):
            s_ref[x] = _dot(km_ref[x, _blk_slice(t, blk), :], qm_ref[x, :, _blk_slice(i, blk)])

    def stage_b_diag(i):
        nb = mb_ref[...]
        for x in range(2):
            s = s_ref[x] + nb
            m = jnp.max(s, axis=0, keepdims=True)
            p = jnp.exp2(s - m)
            m_ref[x, i] = m
            l_ref[x, i] = jnp.sum(p, axis=0, keepdims=True)
            p_ref[x] = p.astype(BF16)

    def stage_c_diag(i):
        vt = vt_ref[:, _blk_slice(i, blk)]
        for x in range(2):
            acc_ref[x, i] = _dot(vt, p_ref[x])

    def stage_b(i, t):
        for x in range(2):
            s = s_ref[x]
            m_prev = m_ref[x, i]
            m_new = jnp.maximum(m_prev, jnp.max(s, axis=0, keepdims=True))
            alpha = jnp.exp2(m_prev - m_new)
            p = jnp.exp2(s - m_new)
            l_ref[x, i] = alpha * l_ref[x, i] + jnp.sum(p, axis=0, keepdims=True)
            m_ref[x, i] = m_new
            al_ref[x] = alpha
            p_ref[x] = p.astype(BF16)

    def stage_c(i, t):
        vt = vt_ref[:, _blk_slice(t, blk)]
        for x in range(2):
            acc_ref[x, i] = al_ref[x] * acc_ref[x, i] + _dot(vt, p_ref[x])

    def finalize():
        lam = (jnp.exp(jnp.sum(lq1_ref[...] * lk1_ref[...], axis=-1, keepdims=True))
               - jnp.exp(jnp.sum(lq2_ref[...] * lk2_ref[...], axis=-1, keepdims=True))
               + lam_init)
        for i in range(nblk):
            o_t = (acc_ref[0, i] * (1.0 / l_ref[0, i])
                   - lam * (acc_ref[1, i] * (1.0 / l_ref[1, i])))
            o = o_t.T
            o_ref[_blk_slice(i, blk), :] = (_rms(o, gsub_ref[...]) * (1.0 - lam_init)).astype(BF16)

    return (stage_a, stage_b_diag, stage_c_diag, stage_b, stage_c), finalize


def _diff_scratch(seq, blk, nblk):
    return [
        pltpu.VMEM((seq, LANE), BF16),
        pltpu.VMEM((DIFF_HEAD_DIM, seq), BF16),
        pltpu.VMEM((2, LANE, seq), BF16),
        pltpu.VMEM((2, seq, LANE), BF16),
        pltpu.VMEM((2, blk, blk), F32),
        pltpu.VMEM((2, blk, blk), BF16),
        pltpu.VMEM((2, 1, blk), F32),
        pltpu.VMEM((2, nblk, 1, blk), F32),
        pltpu.VMEM((2, nblk, 1, blk), F32),
        pltpu.VMEM((2, nblk, LANE, blk), F32),
    ]


N_MLA_IN, N_DIFF_IN = 3, 14
N_MLA_SCRATCH, N_DIFF_SCRATCH = 6, 10


def _attention_body(tab_ref, *refs, blk, nblk, lam_init):
    mla_in, refs = refs[:N_MLA_IN], refs[N_MLA_IN:]
    diff_in, refs = refs[:N_DIFF_IN], refs[N_DIFF_IN:]
    (o_mla_ref, o_diff_ref, mb_ref), refs = refs[:3], refs[3:]
    mla_scr, diff_scr = refs[:N_MLA_SCRATCH], refs[N_MLA_SCRATCH:]
    assert len(diff_scr) == N_DIFF_SCRATCH

    @pl.when((pl.program_id(0) == 0) & (pl.program_id(1) == 0))
    def _():
        _init_mask_bias(mb_ref, blk)

    mla, mla_fin = _mla_stages(*mla_in, o_mla_ref, mb_ref, *mla_scr, blk, nblk)
    diff, diff_fin = _diff_stages(*diff_in, o_diff_ref, mb_ref, *diff_scr, blk, nblk, lam_init)

    _run_pipeline([mla, diff], tab_ref, nblk)
    mla_fin()
    diff_fin()


def _attention(qt, k, vt, big_n, big_t, pos_icol, pos_irow, slopes_l2e, lq1, lk1, lq2, lk2, g_sub,
               batch, seq, blk, lam_init):
    assert MLA_HEADS == DIFF_HEADS
    nblk = seq // blk
    assert nblk >= 3
    width = DIFF_HEADS * 2 * DIFF_HEAD_DIM
    cb = width // LANE
    k_off = 1 * cb
    q_off, v_off = 0, 1 * cb
    k_tab, q_col = _alibi_lane_tables(slopes_l2e)
    digit_k = np.arange(LANE) % DIFF_HEAD_DIM % 3
    shift_row = jnp.asarray((digit_k * 8).reshape(1, LANE), jnp.int32)
    shift_col = jnp.asarray((digit_k[:DIFF_HEAD_DIM] * 8).reshape(DIFF_HEAD_DIM, 1), jnp.int32)
    small = lambda a: pl.BlockSpec(a.shape, lambda b, h: (0,) * a.ndim)
    mla_specs = [
        pl.BlockSpec((QK_PAD, seq), lambda b, h: (h, b)),
        pl.BlockSpec((seq, QK_PAD), lambda b, h: (b, h)),
        pl.BlockSpec((MLA_V, seq), lambda b, h: (h, b)),
    ]
    diff_specs = [
        pl.BlockSpec((LANE, seq), lambda b, h: (q_off + h, b)),
        pl.BlockSpec((seq, LANE), lambda b, h: (b, k_off + h)),
        pl.BlockSpec((LANE, seq), lambda b, h: (v_off + h, b)),
        pl.BlockSpec((seq, 1), lambda b, h: (b, 0)),
        pl.BlockSpec((1, 1, seq), lambda b, h: (b, 0, 0)),
        small(shift_row), small(shift_col),
        pl.BlockSpec((1, 2, LANE), lambda b, h: (h, 0, 0)),
        pl.BlockSpec((1, DIFF_HEAD_DIM, 1), lambda b, h: (h, 0, 0)),
        small(lq1), small(lk1), small(lq2), small(lk2), small(g_sub),
    ]
    assert len(mla_specs) == N_MLA_IN and len(diff_specs) == N_DIFF_IN
    out_spec = pl.BlockSpec((seq, LANE), lambda b, h: (b, h))
    return pl.pallas_call(
        functools.partial(_attention_body, blk=blk, nblk=nblk, lam_init=lam_init),
        grid=(batch, MLA_HEADS),
        in_specs=[pl.BlockSpec(memory_space=pltpu.SMEM)] + mla_specs + diff_specs,
        out_specs=[out_spec, out_spec],
        out_shape=[jax.ShapeDtypeStruct((batch * seq, MLA_HEADS * MLA_V), BF16),
                   jax.ShapeDtypeStruct((batch * seq, width), BF16)],
        scratch_shapes=([pltpu.VMEM((blk, blk), F32)]
                        + _mla_scratch(blk, nblk) + _diff_scratch(seq, blk, nblk)),
        compiler_params=pltpu.CompilerParams(
            dimension_semantics=("arbitrary", "arbitrary"),
            vmem_limit_bytes=VMEM_LIMIT),
        name="attention",
    )(_pair_table(nblk), qt, k, vt, big_t, big_n, big_t, pos_icol, pos_irow, shift_row, shift_col,
      k_tab, q_col, lq1, lk1, lq2, lk2, g_sub)


def _outproj_body(x_ref, om_ref, od_ref, gm_ref, gd_ref, w_ref, gpost_ref, o_ref):
    gm = gm_ref[...].astype(F32)
    gd = gd_ref[...].astype(F32)
    mm = (om_ref[...].astype(F32) * (gm * jax.nn.sigmoid(gm))).astype(BF16)
    md = (od_ref[...].astype(F32) * (gd * jax.nn.sigmoid(gd))).astype(BF16)
    half = mm.shape[1]
    y = _dot(mm, w_ref[:half, :]) + _dot(md, w_ref[half:, :])
    o_ref[...] = x_ref[...] + _rms(y, gpost_ref[...])


def _outproj(x2, o_mla, o_diff, big, w_out, g_post, tm):
    m, d = x2.shape
    half = o_mla.shape[1]
    gate_diff_blk = (big.shape[1] - half) // half
    return pl.pallas_call(
        _outproj_body,
        grid=(m // tm,),
        in_specs=[
            pl.BlockSpec((tm, d), lambda i: (i, 0)),
            pl.BlockSpec((tm, half), lambda i: (i, 0)),
            pl.BlockSpec((tm, half), lambda i: (i, 0)),
            pl.BlockSpec((tm, half), lambda i: (i, 0)),
            pl.BlockSpec((tm, half), lambda i: (i, gate_diff_blk)),
            pl.BlockSpec(w_out.shape, lambda i: (0, 0)),
            pl.BlockSpec((1, d), lambda i: (0, 0)),
        ],
        out_specs=pl.BlockSpec((tm, d), lambda i: (i, 0)),
        out_shape=jax.ShapeDtypeStruct((m, d), F32),
        compiler_params=pltpu.CompilerParams(
            dimension_semantics=("arbitrary",),
            vmem_limit_bytes=VMEM_LIMIT),
        name="outproj",
    )(x2, o_mla, o_diff, big, big, w_out, g_post)


def kernel(x, positions, g_pre, w_in, g_q_a, w_q_b, g_kv_a, w_kv_b, lambda_q1, lambda_k1,
           lambda_q2, lambda_k2, g_diff_sub, w_out, g_post):
    batch, seq, d = x.shape
    depth = g_pre.shape[0]
    q_rank = w_q_b.shape[1]
    kv_rank = w_kv_b.shape[1]
    half_rope = MLA_ROPE // 2
    lat_end = q_rank + kv_rank
    pe_end = lat_end + MLA_ROPE

    pos_icol = positions.astype(jnp.int32).reshape(batch * seq, 1)
    pos_irow = positions.astype(jnp.int32).reshape(batch, 1, seq)
    freqs = 1.0 / (ROPE_THETA ** (jnp.arange(0, MLA_ROPE, 2, dtype=F32) / MLA_ROPE))
    freq = jnp.tile(freqs, 4).reshape(1, LANE)
    slopes = 2.0 ** (-8.0 * (jnp.arange(DIFF_HEADS, dtype=F32) + 1.0) / DIFF_HEADS)
    slopes_l2e = slopes * LOG2E

    mla_scale = MLA_QK ** -0.5 * LOG2E
    diff_scale = DIFF_HEAD_DIM ** -0.5 * LOG2E
    width = DIFF_HEADS * 2 * DIFF_HEAD_DIM
    g0 = pe_end
    gate_mla_c, dq_c, dk_c, dv_c, gate_diff_c = (slice(g0 + n * width, g0 + (n + 1) * width) for n in range(5))
    inproj_tn = 512
    row_scale = jnp.asarray([diff_scale] * (width // inproj_tn) + [1.0] * (width // inproj_tn), F32)

    x2 = x.reshape(batch * seq, d)
    for l in range(depth):
        lam_init = 0.8 - 0.6 * math.exp(-0.3 * l)
        wi = w_in[l].astype(BF16)
        u1 = wi[:, lat_end:lat_end + half_rope]
        u2 = wi[:, lat_end + half_rope:pe_end]
        w_lat = jnp.concatenate([wi[:, :lat_end], u1, u1, u2, u2, u2, u2, u1, u1], axis=1)
        w_n = jnp.concatenate([wi[:, gate_mla_c], wi[:, dk_c], wi[:, gate_diff_c]], axis=1)
        w_t = jnp.concatenate([wi[:, dq_c], wi[:, dv_c]], axis=1)
        wq = w_q_b[l]
        t1 = wq[:, :, MLA_NOPE:MLA_NOPE + half_rope]
        t2 = wq[:, :, MLA_NOPE + half_rope:]
        w_q = jnp.concatenate([wq[:, :, :MLA_NOPE], t1, t2, t2, t1], axis=-1)
        w_qt = w_q.reshape(q_rank, MLA_HEADS * QK_PAD).T.astype(BF16)
        wkv = w_kv_b[l]
        w_k = wkv[:, :, :MLA_NOPE].reshape(kv_rank, -1).astype(BF16)
        w_vt = wkv[:, :, MLA_NOPE:].reshape(kv_rank, -1).T.astype(BF16)
        gp = g_pre[l].reshape(1, d)

        big_n, big_t = _inproj(x2, gp, w_n, w_t, row_scale, tm=512, tn=inproj_tn)
        qt, k, vt = _latent(x2, pos_icol, freq, gp, w_lat, g_q_a[l].reshape(1, -1), w_qt,
                            g_kv_a[l].reshape(1, -1), w_k, w_vt, tm=512, q_scale=mla_scale)
        o_mla, o_diff = _attention(
            qt, k, vt, big_n, big_t, pos_icol, pos_irow, slopes_l2e,
            lambda_q1[l].reshape(1, -1), lambda_k1[l].reshape(1, -1),
            lambda_q2[l].reshape(1, -1), lambda_k2[l].reshape(1, -1),
            g_diff_sub[l].reshape(1, -1), batch, seq, blk=512, lam_init=lam_init)
        x2 = _outproj(x2, o_mla, o_diff, big_n, w_out[l].astype(BF16), g_post[l].reshape(1, d), tm=512)
    return x2.reshape(batch, seq, d)
```

```python
import functools
import math

import numpy as np
import jax
import jax.numpy as jnp
from jax import lax
from jax.experimental import pallas as pl
from jax.experimental.pallas import tpu as pltpu

F32 = jnp.float32
BF16 = jnp.bfloat16

EPS = 1e-6
LOG2E = 1.4426950408889634
ROPE_THETA = 10000.0

MLA_HEADS = 8
MLA_NOPE = 128
MLA_ROPE = 64
MLA_V = 128
MLA_QK = MLA_NOPE + MLA_ROPE
DIFF_HEADS = 8
DIFF_HEAD_DIM = 64
LANE = 128
QK_PAD = 256
ALIBI_LANES = 9
ALIBI_BASE = (DIFF_HEAD_DIM, 0)

VMEM_LIMIT = 56 * 1024 * 1024


def _rms(xf, g):
    ms = jnp.mean(xf * xf, axis=-1, keepdims=True)
    return xf * lax.rsqrt(ms + EPS) * g


def _dot(a, b):
    return jnp.dot(a, b, preferred_element_type=F32)


def _dot_nt(a, b):
    return lax.dot_general(a, b, (((1,), (1,)), ((), ())), preferred_element_type=F32)


def _regroup_body(w_ref, lat_ref, n_ref, t_ref, *, lat_end, half_rope, width):
    pe_end = lat_end + 2 * half_rope
    u1 = w_ref[:, lat_end:lat_end + half_rope]
    u2 = w_ref[:, lat_end + half_rope:pe_end]
    lat_ref[...] = jnp.concatenate([w_ref[:, :lat_end], u1, u1, u2, u2, u2, u2, u1, u1],
                                   axis=1).astype(BF16)
    group = lambda n: w_ref[:, pe_end + n * width:pe_end + (n + 1) * width]
    for dst, src in enumerate((0, 2, 4)):
        n_ref[:, dst * width:(dst + 1) * width] = group(src).astype(BF16)
    for dst, src in enumerate((1, 3)):
        t_ref[:, dst * width:(dst + 1) * width] = group(src).astype(BF16)


def _regroup_w_in(wi, lat_end, half_rope, width, rows):
    d, n_in = wi.shape
    n_lat = lat_end + 8 * half_rope
    return pl.pallas_call(
        functools.partial(_regroup_body, lat_end=lat_end, half_rope=half_rope, width=width),
        grid=(d // rows,),
        in_specs=[pl.BlockSpec((rows, n_in), lambda i: (i, 0))],
        out_specs=[
            pl.BlockSpec((rows, n_lat), lambda i: (i, 0)),
            pl.BlockSpec((rows, 3 * width), lambda i: (i, 0)),
            pl.BlockSpec((rows, 2 * width), lambda i: (i, 0)),
        ],
        out_shape=[
            jax.ShapeDtypeStruct((d, n_lat), BF16),
            jax.ShapeDtypeStruct((d, 3 * width), BF16),
            jax.ShapeDtypeStruct((d, 2 * width), BF16),
        ],
        compiler_params=pltpu.CompilerParams(
            dimension_semantics=("arbitrary",),
            vmem_limit_bytes=VMEM_LIMIT),
        name="regroup_w_in",
    )(wi)


def _inproj_body(rs_ref, x_ref, g_ref, wn_ref, wt_ref, on_ref, ot_ref, *, tn):
    h = _rms(x_ref[...], g_ref[...]).astype(BF16)
    for c in range(wn_ref.shape[1] // tn):
        cols = slice(c * tn, (c + 1) * tn)
        on_ref[:, cols] = _dot(h, wn_ref[:, cols]).astype(BF16)
    for c in range(wt_ref.shape[1] // tn):
        cols = slice(c * tn, (c + 1) * tn)
        ot_ref[cols, :] = (_dot(h, wt_ref[:, cols]) * rs_ref[c]).T.astype(BF16)


def _inproj(x2, g_pre, w_n, w_t, row_scale, tm, tn):
    m, d = x2.shape
    resident = lambda a: pl.BlockSpec(a.shape, lambda i: (0,) * a.ndim, pipeline_mode=pl.Buffered(1))
    return pl.pallas_call(
        functools.partial(_inproj_body, tn=tn),
        grid=(m // tm,),
        in_specs=[
            pl.BlockSpec(memory_space=pltpu.SMEM),
            pl.BlockSpec((tm, d), lambda i: (i, 0)),
            pl.BlockSpec((1, d), lambda i: (0, 0)),
            resident(w_n),
            resident(w_t),
        ],
        out_specs=[
            pl.BlockSpec((tm, w_n.shape[1]), lambda i: (i, 0)),
            pl.BlockSpec((w_t.shape[1], tm), lambda i: (0, i)),
        ],
        out_shape=[
            jax.ShapeDtypeStruct((m, w_n.shape[1]), BF16),
            jax.ShapeDtypeStruct((w_t.shape[1], m), BF16),
        ],
        compiler_params=pltpu.CompilerParams(
            dimension_semantics=("arbitrary",),
            vmem_limit_bytes=VMEM_LIMIT),
        name="inproj",
    )(row_scale, x2, g_pre, w_n, w_t)


def _latent_body(x_ref, pos_ref, freq_ref, gpre_ref, wlat_ref, gq_ref, wqt_ref,
                 gkv_ref, wk_ref, wvt_ref, qt_ref, k_ref, vt_ref, *, q_scale):
    h = _rms(x_ref[...], gpre_ref[...]).astype(BF16)
    lat = _dot(h, wlat_ref[...])
    ang = pos_ref[...].astype(F32) * freq_ref[...]
    cos = jnp.cos(ang)
    sin = jnp.sin(ang)
    seg = lax.broadcasted_iota(jnp.int32, (1, LANE), 1) // (MLA_ROPE // 2)
    fq = jnp.where(seg == 1, -sin, jnp.where(seg == 3, sin, cos)) * q_scale
    sk = jnp.where(seg < 2, -sin, sin)
    k_ext = (lat[:, 768:896] * cos + lat[:, 896:1024] * sk).astype(BF16)

    c_q = _rms(lat[:, :512], gq_ref[...]).astype(BF16)
    qft = _dot_nt(wqt_ref[...], c_q)
    fqt = fq.T
    c_kv = _rms(lat[:, 512:768], gkv_ref[...]).astype(BF16)
    kf = _dot(c_kv, wk_ref[...])
    vt_ref[...] = _dot_nt(wvt_ref[...], c_kv).astype(BF16)
    for hd in range(MLA_HEADS):
        o = hd * QK_PAD
        qt_ref[o:o + LANE, :] = (qft[o:o + LANE, :] * q_scale).astype(BF16)
        qt_ref[o + LANE:o + QK_PAD, :] = (qft[o + LANE:o + QK_PAD, :] * fqt).astype(BF16)
        k_ref[:, o:o + LANE] = kf[:, hd * LANE:(hd + 1) * LANE].astype(BF16)
        k_ref[:, o + LANE:o + QK_PAD] = k_ext


def _latent(x2, pos_col, freq, g_pre, w_lat, g_q, w_qt, g_kv, w_k, w_vt, tm, q_scale):
    m, d = x2.shape
    nq = MLA_HEADS * QK_PAD
    nv = MLA_HEADS * MLA_V
    full = lambda a: pl.BlockSpec(a.shape, lambda i: (0,) * a.ndim)
    return pl.pallas_call(
        functools.partial(_latent_body, q_scale=q_scale),
        grid=(m // tm,),
        in_specs=[
            pl.BlockSpec((tm, d), lambda i: (i, 0)),
            pl.BlockSpec((tm, 1), lambda i: (i, 0)),
            full(freq), full(g_pre), full(w_lat), full(g_q), full(w_qt), full(g_kv), full(w_k),
            full(w_vt),
        ],
        out_specs=[
            pl.BlockSpec((nq, tm), lambda i: (0, i)),
            pl.BlockSpec((tm, nq), lambda i: (i, 0)),
            pl.BlockSpec((nv, tm), lambda i: (0, i)),
        ],
        out_shape=[
            jax.ShapeDtypeStruct((nq, m), BF16),
            jax.ShapeDtypeStruct((m, nq), BF16),
            jax.ShapeDtypeStruct((nv, m), BF16),
        ],
        compiler_params=pltpu.CompilerParams(
            dimension_semantics=("arbitrary",),
            vmem_limit_bytes=VMEM_LIMIT),
        name="latent",
    )(x2, pos_col, freq, g_pre, w_lat, g_q, w_qt, g_kv, w_k, w_vt)


def _blk_slice(idx, blk):
    if isinstance(idx, int):
        return slice(idx * blk, (idx + 1) * blk)
    return pl.ds(pl.multiple_of(idx * blk, blk), blk)


def _offdiag_pairs(nblk):
    return [(i, t) for i in range(1, nblk) for t in range(i)]


def _pair_table(nblk):
    return jnp.asarray(np.array(_offdiag_pairs(nblk), dtype=np.int32).T)


def _init_mask_bias(mb_ref, blk):
    kv = lax.broadcasted_iota(jnp.int32, (blk, blk), 0)
    qi = lax.broadcasted_iota(jnp.int32, (blk, blk), 1)
    mb_ref[...] = jnp.where(kv <= qi, 0.0, -jnp.inf).astype(F32)


def _run_pipeline(heads, tab_ref, nblk):
    A, B_DIAG, C_DIAG, B, C = range(5)
    pairs = _offdiag_pairs(nblk)
    npairs = len(pairs)
    assert (nblk - 2) % 2 == 0 and (npairs - 2) % 2 == 0

    def step(c=None, b=None, a=None):
        for call in (c, b, (A, a) if a is not None else None):
            if call is not None:
                for stages in heads:
                    stages[call[0]](*call[1])

    step(a=(0, 0))
    step(b=(B_DIAG, (0,)), a=(1, 1))

    def diag_loop(u, carry):
        for d in (1, 2):
            g = d + 2 * u
            step(c=(C_DIAG, (g - 1,)), b=(B_DIAG, (g,)), a=(g + 1, g + 1))
        return carry

    lax.fori_loop(0, (nblk - 2) // 2, diag_loop, 0)

    step(c=(C_DIAG, (nblk - 2,)), b=(B_DIAG, (nblk - 1,)), a=pairs[0])
    step(c=(C_DIAG, (nblk - 1,)), b=(B, pairs[0]), a=pairs[1])

    def off_loop(u, carry):
        pair = lambda f: (tab_ref[0, f], tab_ref[1, f])
        for d in (1, 2):
            f = d + 2 * u
            step(c=(C, pair(f - 1)), b=(B, pair(f)), a=pair(f + 1))
        return carry

    lax.fori_loop(0, (npairs - 2) // 2, off_loop, 0)

    step(c=(C, pairs[npairs - 2]), b=(B, pairs[npairs - 1]))
    step(c=(C, pairs[npairs - 1]))


def _mla_stages(qt_ref, k_ref, vt_ref, o_ref, mb_ref, s_ref, p_ref, al_ref, m_ref, l_ref, acc_ref,
                blk, nblk):
    def stage_a(i, t):
        s_ref[...] = _dot(k_ref[_blk_slice(t, blk), :], qt_ref[:, _blk_slice(i, blk)])

    def stage_b_diag(i):
        s = s_ref[...] + mb_ref[...]
        m = jnp.max(s, axis=0, keepdims=True)
        p = jnp.exp2(s - m)
        m_ref[i] = m
        l_ref[i] = jnp.sum(p, axis=0, keepdims=True)
        p_ref[...] = p.astype(BF16)

    def stage_c_diag(i):
        acc_ref[i] = _dot(vt_ref[:, _blk_slice(i, blk)], p_ref[...])

    def stage_b(i, t):
        s = s_ref[...]
        m_prev = m_ref[i]
        m_new = jnp.maximum(m_prev, jnp.max(s, axis=0, keepdims=True))
        alpha = jnp.exp2(m_prev - m_new)
        p = jnp.exp2(s - m_new)
        l_ref[i] = alpha * l_ref[i] + jnp.sum(p, axis=0, keepdims=True)
        m_ref[i] = m_new
        al_ref[...] = alpha
        p_ref[...] = p.astype(BF16)

    def stage_c(i, t):
        acc_ref[i] = al_ref[...] * acc_ref[i] + _dot(vt_ref[:, _blk_slice(t, blk)], p_ref[...])

    def finalize():
        for i in range(nblk):
            o_t = acc_ref[i] * (1.0 / l_ref[i])
            o_ref[_blk_slice(i, blk), :] = o_t.T.astype(BF16)

    return (stage_a, stage_b_diag, stage_c_diag, stage_b, stage_c), finalize


def _mla_scratch(blk, nblk):
    return [
        pltpu.VMEM((blk, blk), F32),
        pltpu.VMEM((blk, blk), BF16),
        pltpu.VMEM((1, blk), F32),
        pltpu.VMEM((nblk, 1, blk), F32),
        pltpu.VMEM((nblk, 1, blk), F32),
        pltpu.VMEM((nblk, MLA_V, blk), F32),
    ]


def _alibi_lane_tables(slopes_l2e):
    c1 = slopes_l2e.astype(BF16).astype(F32)
    c2 = (slopes_l2e - c1).astype(BF16).astype(F32)
    c3 = (slopes_l2e - c1 - c2).astype(BF16).astype(F32)
    cw = (jnp.stack([c1, c2, c3], axis=1)[:, :, None]
          * jnp.asarray([1.0, 256.0, 65536.0], F32)[None, None, :]).reshape(-1, ALIBI_LANES)
    heads = slopes_l2e.shape[0]
    ktab = jnp.zeros((heads, 2, LANE), F32)
    for x, base in enumerate(ALIBI_BASE):
        ktab = ktab.at[:, x, base:base + ALIBI_LANES].set(-cw)
    qcol = jnp.zeros((heads, DIFF_HEAD_DIM), F32).at[:, ALIBI_LANES:2 * ALIBI_LANES].set(cw)
    return ktab.astype(BF16), qcol.reshape(heads, DIFF_HEAD_DIM, 1)


def _diff_stages(qt_ref, k_ref, vt_ref, pc_ref, pr_ref, shr_ref, shc_ref, kt_ref, qc_ref,
                 lq1_ref, lk1_ref, lq2_ref, lk2_ref, gsub_ref, o_ref, mb_ref, dig_ref, digt_ref,
                 qm_ref, km_ref, s_ref, p_ref, al_ref, m_ref, l_ref, acc_ref, blk, nblk, lam_init):
    half = DIFF_HEAD_DIM

    @pl.when(pl.program_id(1) == 0)
    def _():
        pcol = pc_ref[...]
        prel = jnp.broadcast_to(pcol - pcol[0:1, :], dig_ref.shape)
        shift = jnp.broadcast_to(shr_ref[...], dig_ref.shape)
        dig_ref[...] = (lax.shift_right_logical(prel, shift) & 255).astype(F32).astype(BF16)
        prow = pr_ref[0]
        prel_t = jnp.broadcast_to(prow - prow[:, 0:1], digt_ref.shape)
        shift_t = jnp.broadcast_to(shc_ref[...], digt_ref.shape)
        digt_ref[...] = (lax.shift_right_logical(prel_t, shift_t) & 255).astype(F32).astype(BF16)

    row = lax.broadcasted_iota(jnp.int32, (half, 1), 0)
    alibi_q = jnp.where(row < ALIBI_LANES, digt_ref[...],
                        jnp.broadcast_to(qc_ref[0], digt_ref.shape).astype(BF16))
    qm_ref[0, :half, :] = qt_ref[:half, :]
    qm_ref[0, half:, :] = alibi_q
    qm_ref[1, :half, :] = alibi_q
    qm_ref[1, half:, :] = qt_ref[half:, :]
    lane = lax.broadcasted_iota(jnp.int32, (1, LANE), 1)
    dig = dig_ref[...]
    k = k_ref[...]
    for x, base in enumerate(ALIBI_BASE):
        own = (lane < half) if x == 0 else (lane >= half)
        k_digit = (lane >= base + ALIBI_LANES) & (lane < base + 2 * ALIBI_LANES)
        km_ref[x] = jnp.where(own, k, jnp.where(k_digit, dig, kt_ref[0, x:x + 1, :]))

    def stage_a(i, t):
        for x in range(2):
            s_ref[x] = _dot(km_ref[x, _blk_slice(t, blk), :], qm_ref[x, :, _blk_slice(i, blk)])

    def stage_b_diag(i):
        nb = mb_ref[...]
        for x in range(2):
            s = s_ref[x] + nb
            m = jnp.max(s, axis=0, keepdims=True)
            p = jnp.exp2(s - m)
            m_ref[x, i] = m
            l_ref[x, i] = jnp.sum(p, axis=0, keepdims=True)
            p_ref[x] = p.astype(BF16)

    def stage_c_diag(i):
        vt = vt_ref[:, _blk_slice(i, blk)]
        for x in range(2):
            acc_ref[x, i] = _dot(vt, p_ref[x])

    def stage_b(i, t):
        for x in range(2):
            s = s_ref[x]
            m_prev = m_ref[x, i]
            m_new = jnp.maximum(m_prev, jnp.max(s, axis=0, keepdims=True))
            alpha = jnp.exp2(m_prev - m_new)
            p = jnp.exp2(s - m_new)
            l_ref[x, i] = alpha * l_ref[x, i] + jnp.sum(p, axis=0, keepdims=True)
            m_ref[x, i] = m_new
            al_ref[x] = alpha
            p_ref[x] = p.astype(BF16)

    def stage_c(i, t):
        vt = vt_ref[:, _blk_slice(t, blk)]
        for x in range(2):
            acc_ref[x, i] = al_ref[x] * acc_ref[x, i] + _dot(vt, p_ref[x])

    def finalize():
        lam = (jnp.exp(jnp.sum(lq1_ref[...] * lk1_ref[...], axis=-1, keepdims=True))
               - jnp.exp(jnp.sum(lq2_ref[...] * lk2_ref[...], axis=-1, keepdims=True))
               + lam_init)
        for i in range(nblk):
            o_t = (acc_ref[0, i] * (1.0 / l_ref[0, i])
                   - lam * (acc_ref[1, i] * (1.0 / l_ref[1, i])))
            o = o_t.T
            o_ref[_blk_slice(i, blk), :] = (_rms(o, gsub_ref[...]) * (1.0 - lam_init)).astype(BF16)

    return (stage_a, stage_b_diag, stage_c_diag, stage_b, stage_c), finalize


def _diff_scratch(seq, blk, nblk):
    return [
        pltpu.VMEM((seq, LANE), BF16),
        pltpu.VMEM((DIFF_HEAD_DIM, seq), BF16),
        pltpu.VMEM((2, LANE, seq), BF16),
        pltpu.VMEM((2, seq, LANE), BF16),
        pltpu.VMEM((2, blk, blk), F32),
        pltpu.VMEM((2, blk, blk), BF16),
        pltpu.VMEM((2, 1, blk), F32),
        pltpu.VMEM((2, nblk, 1, blk), F32),
        pltpu.VMEM((2, nblk, 1, blk), F32),
        pltpu.VMEM((2, nblk, LANE, blk), F32),
    ]


N_MLA_IN, N_DIFF_IN = 3, 14
N_MLA_SCRATCH, N_DIFF_SCRATCH = 6, 10


def _attention_body(tab_ref, *refs, blk, nblk, lam_init):
    mla_in, refs = refs[:N_MLA_IN], refs[N_MLA_IN:]
    diff_in, refs = refs[:N_DIFF_IN], refs[N_DIFF_IN:]
    (o_mla_ref, o_diff_ref, mb_ref), refs = refs[:3], refs[3:]
    mla_scr, diff_scr = refs[:N_MLA_SCRATCH], refs[N_MLA_SCRATCH:]
    assert len(diff_scr) == N_DIFF_SCRATCH

    @pl.when((pl.program_id(0) == 0) & (pl.program_id(1) == 0))
    def _():
        _init_mask_bias(mb_ref, blk)

    mla, mla_fin = _mla_stages(*mla_in, o_mla_ref, mb_ref, *mla_scr, blk, nblk)
    diff, diff_fin = _diff_stages(*diff_in, o_diff_ref, mb_ref, *diff_scr, blk, nblk, lam_init)

    _run_pipeline([diff, mla], tab_ref, nblk)
    mla_fin()
    diff_fin()


def _attention(qt, k, vt, big_n, big_t, pos_icol, pos_irow, slopes_l2e, lq1, lk1, lq2, lk2, g_sub,
               batch, seq, blk, lam_init):
    assert MLA_HEADS == DIFF_HEADS
    nblk = seq // blk
    assert nblk >= 3
    width = DIFF_HEADS * 2 * DIFF_HEAD_DIM
    cb = width // LANE
    k_off = 1 * cb
    q_off, v_off = 0, 1 * cb
    k_tab, q_col = _alibi_lane_tables(slopes_l2e)
    digit_k = np.arange(LANE) % DIFF_HEAD_DIM % 3
    shift_row = jnp.asarray((digit_k * 8).reshape(1, LANE), jnp.int32)
    shift_col = jnp.asarray((digit_k[:DIFF_HEAD_DIM] * 8).reshape(DIFF_HEAD_DIM, 1), jnp.int32)
    small = lambda a: pl.BlockSpec(a.shape, lambda b, h: (0,) * a.ndim)
    mla_specs = [
        pl.BlockSpec((QK_PAD, seq), lambda b, h: (h, b)),
        pl.BlockSpec((seq, QK_PAD), lambda b, h: (b, h)),
        pl.BlockSpec((MLA_V, seq), lambda b, h: (h, b)),
    ]
    diff_specs = [
        pl.BlockSpec((LANE, seq), lambda b, h: (q_off + h, b)),
        pl.BlockSpec((seq, LANE), lambda b, h: (b, k_off + h)),
        pl.BlockSpec((LANE, seq), lambda b, h: (v_off + h, b)),
        pl.BlockSpec((seq, 1), lambda b, h: (b, 0)),
        pl.BlockSpec((1, 1, seq), lambda b, h: (b, 0, 0)),
        small(shift_row), small(shift_col),
        pl.BlockSpec((1, 2, LANE), lambda b, h: (h, 0, 0)),
        pl.BlockSpec((1, DIFF_HEAD_DIM, 1), lambda b, h: (h, 0, 0)),
        small(lq1), small(lk1), small(lq2), small(lk2), small(g_sub),
    ]
    assert len(mla_specs) == N_MLA_IN and len(diff_specs) == N_DIFF_IN
    out_spec = pl.BlockSpec((seq, LANE), lambda b, h: (b, h))
    return pl.pallas_call(
        functools.partial(_attention_body, blk=blk, nblk=nblk, lam_init=lam_init),
        grid=(batch, MLA_HEADS),
        in_specs=[pl.BlockSpec(memory_space=pltpu.SMEM)] + mla_specs + diff_specs,
        out_specs=[out_spec, out_spec],
        out_shape=[jax.ShapeDtypeStruct((batch * seq, MLA_HEADS * MLA_V), BF16),
                   jax.ShapeDtypeStruct((batch * seq, width), BF16)],
        scratch_shapes=([pltpu.VMEM((blk, blk), F32)]
                        + _mla_scratch(blk, nblk) + _diff_scratch(seq, blk, nblk)),
        compiler_params=pltpu.CompilerParams(
            dimension_semantics=("arbitrary", "arbitrary"),
            vmem_limit_bytes=VMEM_LIMIT),
        name="attention",
    )(_pair_table(nblk), qt, k, vt, big_t, big_n, big_t, pos_icol, pos_irow, shift_row, shift_col,
      k_tab, q_col, lq1, lk1, lq2, lk2, g_sub)


def _outproj_body(x_ref, om_ref, od_ref, gm_ref, gd_ref, w_ref, gpost_ref, o_ref):
    gm = gm_ref[...].astype(F32)
    gd = gd_ref[...].astype(F32)
    mm = (om_ref[...].astype(F32) * (gm * jax.nn.sigmoid(gm))).astype(BF16)
    md = (od_ref[...].astype(F32) * (gd * jax.nn.sigmoid(gd))).astype(BF16)
    half = mm.shape[1]
    y = _dot(mm, w_ref[:half, :]) + _dot(md, w_ref[half:, :])
    o_ref[...] = x_ref[...] + _rms(y, gpost_ref[...])


def _outproj(x2, o_mla, o_diff, big, w_out, g_post, tm):
    m, d = x2.shape
    half = o_mla.shape[1]
    gate_diff_blk = (big.shape[1] - half) // half
    return pl.pallas_call(
        _outproj_body,
        grid=(m // tm,),
        in_specs=[
            pl.BlockSpec((tm, d), lambda i: (i, 0)),
            pl.BlockSpec((tm, half), lambda i: (i, 0)),
            pl.BlockSpec((tm, half), lambda i: (i, 0)),
            pl.BlockSpec((tm, half), lambda i: (i, 0)),
            pl.BlockSpec((tm, half), lambda i: (i, gate_diff_blk)),
            pl.BlockSpec(w_out.shape, lambda i: (0, 0)),
            pl.BlockSpec((1, d), lambda i: (0, 0)),
        ],
        out_specs=pl.BlockSpec((tm, d), lambda i: (i, 0)),
        out_shape=jax.ShapeDtypeStruct((m, d), F32),
        compiler_params=pltpu.CompilerParams(
            dimension_semantics=("arbitrary",),
            vmem_limit_bytes=VMEM_LIMIT),
        name="outproj",
    )(x2, o_mla, o_diff, big, big, w_out, g_post)


def kernel(x, positions, g_pre, w_in, g_q_a, w_q_b, g_kv_a, w_kv_b, lambda_q1, lambda_k1,
           lambda_q2, lambda_k2, g_diff_sub, w_out, g_post):
    batch, seq, d = x.shape
    depth = g_pre.shape[0]
    q_rank = w_q_b.shape[1]
    kv_rank = w_kv_b.shape[1]
    half_rope = MLA_ROPE // 2
    lat_end = q_rank + kv_rank
    pe_end = lat_end + MLA_ROPE

    pos_icol = positions.astype(jnp.int32).reshape(batch * seq, 1)
    pos_irow = positions.astype(jnp.int32).reshape(batch, 1, seq)
    freqs = 1.0 / (ROPE_THETA ** (jnp.arange(0, MLA_ROPE, 2, dtype=F32) / MLA_ROPE))
    freq = jnp.tile(freqs, 4).reshape(1, LANE)
    slopes = 2.0 ** (-8.0 * (jnp.arange(DIFF_HEADS, dtype=F32) + 1.0) / DIFF_HEADS)
    slopes_l2e = slopes * LOG2E

    mla_scale = MLA_QK ** -0.5 * LOG2E
    diff_scale = DIFF_HEAD_DIM ** -0.5 * LOG2E
    width = DIFF_HEADS * 2 * DIFF_HEAD_DIM
    inproj_tn = 512
    row_scale = jnp.asarray([diff_scale] * (width // inproj_tn) + [1.0] * (width // inproj_tn), F32)

    x2 = x.reshape(batch * seq, d)
    for l in range(depth):
        lam_init = 0.8 - 0.6 * math.exp(-0.3 * l)
        w_lat, w_n, w_t = _regroup_w_in(w_in[l], lat_end, half_rope, width, rows=256)
        wq = w_q_b[l]
        t1 = wq[:, :, MLA_NOPE:MLA_NOPE + half_rope]
        t2 = wq[:, :, MLA_NOPE + half_rope:]
        w_q = jnp.concatenate([wq[:, :, :MLA_NOPE], t1, t2, t2, t1], axis=-1)
        w_qt = w_q.reshape(q_rank, MLA_HEADS * QK_PAD).T.astype(BF16)
        wkv = w_kv_b[l]
        w_k = wkv[:, :, :MLA_NOPE].reshape(kv_rank, -1).astype(BF16)
        w_vt = wkv[:, :, MLA_NOPE:].reshape(kv_rank, -1).T.astype(BF16)
        gp = g_pre[l].reshape(1, d)

        big_n, big_t = _inproj(x2, gp, w_n, w_t, row_scale, tm=512, tn=inproj_tn)
        qt, k, vt = _latent(x2, pos_icol, freq, gp, w_lat, g_q_a[l].reshape(1, -1), w_qt,
                            g_kv_a[l].reshape(1, -1), w_k, w_vt, tm=512, q_scale=mla_scale)
        o_mla, o_diff = _attention(
            qt, k, vt, big_n, big_t, pos_icol, pos_irow, slopes_l2e,
            lambda_q1[l].reshape(1, -1), lambda_k1[l].reshape(1, -1),
            lambda_q2[l].reshape(1, -1), lambda_k2[l].reshape(1, -1),
            g_diff_sub[l].reshape(1, -1), batch, seq, blk=512, lam_init=lam_init)
        x2 = _outproj(x2, o_mla, o_diff, big_n, w_out[l].astype(BF16), g_post[l].reshape(1, d), tm=512)
    return x2.reshape(batch, seq, d)
```

```python
import functools
import math

import numpy as np
import jax
import jax.numpy as jnp
from jax import lax
from jax.experimental import pallas as pl
from jax.experimental.pallas import tpu as pltpu

F32 = jnp.float32
BF16 = jnp.bfloat16

EPS = 1e-6
LOG2E = 1.4426950408889634
ROPE_THETA = 10000.0

MLA_HEADS = 8
MLA_NOPE = 128
MLA_ROPE = 64
MLA_V = 128
MLA_QK = MLA_NOPE + MLA_ROPE
DIFF_HEADS = 8
DIFF_HEAD_DIM = 64
LANE = 128
QK_PAD = 256
ALIBI_LANES = 9
ALIBI_BASE = (DIFF_HEAD_DIM, 0)

VMEM_LIMIT = 56 * 1024 * 1024


def _rms(xf, g):
    ms = jnp.mean(xf * xf, axis=-1, keepdims=True)
    return xf * lax.rsqrt(ms + EPS) * g


def _dot(a, b):
    return jnp.dot(a, b, preferred_element_type=F32)


def _dot_nt(a, b):
    return lax.dot_general(a, b, (((1,), (1,)), ((), ())), preferred_element_type=F32)


def _regroup_body(w_ref, lat_ref, n_ref, t_ref, *, lat_end, half_rope, width):
    pe_end = lat_end + 2 * half_rope
    u1 = w_ref[0, :,lat_end:lat_end + half_rope]
    u2 = w_ref[0, :,lat_end + half_rope:pe_end]
    lat_ref[...] = jnp.concatenate([w_ref[0, :,:lat_end], u1, u1, u2, u2, u2, u2, u1, u1],
                                   axis=1).astype(BF16)
    group = lambda n: w_ref[0, :,pe_end + n * width:pe_end + (n + 1) * width]
    for dst, src in enumerate((0, 2, 4)):
        n_ref[:, dst * width:(dst + 1) * width] = group(src).astype(BF16)
    for dst, src in enumerate((1, 3)):
        t_ref[:, dst * width:(dst + 1) * width] = group(src).astype(BF16)


def _regroup_w_in(w_in, layer, lat_end, half_rope, width, rows):
    _, d, n_in = w_in.shape
    n_lat = lat_end + 8 * half_rope
    return pl.pallas_call(
        functools.partial(_regroup_body, lat_end=lat_end, half_rope=half_rope, width=width),
        grid=(d // rows,),
        in_specs=[pl.BlockSpec((1, rows, n_in), lambda i: (layer, i, 0))],
        out_specs=[
            pl.BlockSpec((rows, n_lat), lambda i: (i, 0)),
            pl.BlockSpec((rows, 3 * width), lambda i: (i, 0)),
            pl.BlockSpec((rows, 2 * width), lambda i: (i, 0)),
        ],
        out_shape=[
            jax.ShapeDtypeStruct((d, n_lat), BF16),
            jax.ShapeDtypeStruct((d, 3 * width), BF16),
            jax.ShapeDtypeStruct((d, 2 * width), BF16),
        ],
        compiler_params=pltpu.CompilerParams(
            dimension_semantics=("arbitrary",),
            vmem_limit_bytes=VMEM_LIMIT),
        name="regroup_w_in",
    )(w_in)


def _inproj_body(rs_ref, x_ref, g_ref, wn_ref, wt_ref, on_ref, ot_ref, *, tn):
    h = _rms(x_ref[...], g_ref[...]).astype(BF16)
    for c in range(wn_ref.shape[1] // tn):
        cols = slice(c * tn, (c + 1) * tn)
        on_ref[:, cols] = _dot(h, wn_ref[:, cols]).astype(BF16)
    for c in range(wt_ref.shape[1] // tn):
        cols = slice(c * tn, (c + 1) * tn)
        ot_ref[cols, :] = (_dot(h, wt_ref[:, cols]) * rs_ref[c]).T.astype(BF16)


def _inproj(x2, g_pre, w_n, w_t, row_scale, tm, tn):
    m, d = x2.shape
    resident = lambda a: pl.BlockSpec(a.shape, lambda i: (0,) * a.ndim, pipeline_mode=pl.Buffered(1))
    return pl.pallas_call(
        functools.partial(_inproj_body, tn=tn),
        grid=(m // tm,),
        in_specs=[
            pl.BlockSpec(memory_space=pltpu.SMEM),
            pl.BlockSpec((tm, d), lambda i: (i, 0)),
            pl.BlockSpec((1, d), lambda i: (0, 0)),
            resident(w_n),
            resident(w_t),
        ],
        out_specs=[
            pl.BlockSpec((tm, w_n.shape[1]), lambda i: (i, 0)),
            pl.BlockSpec((w_t.shape[1], tm), lambda i: (0, i)),
        ],
        out_shape=[
            jax.ShapeDtypeStruct((m, w_n.shape[1]), BF16),
            jax.ShapeDtypeStruct((w_t.shape[1], m), BF16),
        ],
        compiler_params=pltpu.CompilerParams(
            dimension_semantics=("arbitrary",),
            vmem_limit_bytes=VMEM_LIMIT),
        name="inproj",
    )(row_scale, x2, g_pre, w_n, w_t)


def _latent_body(x_ref, pos_ref, freq_ref, gpre_ref, wlat_ref, gq_ref, wqt_ref,
                 gkv_ref, wk_ref, wvt_ref, qt_ref, k_ref, vt_ref, *, q_scale):
    h = _rms(x_ref[...], gpre_ref[...]).astype(BF16)
    lat = _dot(h, wlat_ref[...])
    ang = pos_ref[...].astype(F32) * freq_ref[...]
    cos = jnp.cos(ang)
    sin = jnp.sin(ang)
    seg = lax.broadcasted_iota(jnp.int32, (1, LANE), 1) // (MLA_ROPE // 2)
    fq = jnp.where(seg == 1, -sin, jnp.where(seg == 3, sin, cos)) * q_scale
    sk = jnp.where(seg < 2, -sin, sin)
    k_ext = (lat[:, 768:896] * cos + lat[:, 896:1024] * sk).astype(BF16)

    c_q = _rms(lat[:, :512], gq_ref[...]).astype(BF16)
    qft = _dot_nt(wqt_ref[...], c_q)
    fqt = fq.T
    c_kv = _rms(lat[:, 512:768], gkv_ref[...]).astype(BF16)
    kf = _dot(c_kv, wk_ref[...])
    vt_ref[...] = _dot_nt(wvt_ref[...], c_kv).astype(BF16)
    for hd in range(MLA_HEADS):
        o = hd * QK_PAD
        qt_ref[o:o + LANE, :] = (qft[o:o + LANE, :] * q_scale).astype(BF16)
        qt_ref[o + LANE:o + QK_PAD, :] = (qft[o + LANE:o + QK_PAD, :] * fqt).astype(BF16)
        k_ref[:, o:o + LANE] = kf[:, hd * LANE:(hd + 1) * LANE].astype(BF16)
        k_ref[:, o + LANE:o + QK_PAD] = k_ext


def _latent(x2, pos_col, freq, g_pre, w_lat, g_q, w_qt, g_kv, w_k, w_vt, tm, q_scale):
    m, d = x2.shape
    nq = MLA_HEADS * QK_PAD
    nv = MLA_HEADS * MLA_V
    full = lambda a: pl.BlockSpec(a.shape, lambda i: (0,) * a.ndim)
    return pl.pallas_call(
        functools.partial(_latent_body, q_scale=q_scale),
        grid=(m // tm,),
        in_specs=[
            pl.BlockSpec((tm, d), lambda i: (i, 0)),
            pl.BlockSpec((tm, 1), lambda i: (i, 0)),
            full(freq), full(g_pre), full(w_lat), full(g_q), full(w_qt), full(g_kv), full(w_k),
            full(w_vt),
        ],
        out_specs=[
            pl.BlockSpec((nq, tm), lambda i: (0, i)),
            pl.BlockSpec((tm, nq), lambda i: (i, 0)),
            pl.BlockSpec((nv, tm), lambda i: (0, i)),
        ],
        out_shape=[
            jax.ShapeDtypeStruct((nq, m), BF16),
            jax.ShapeDtypeStruct((m, nq), BF16),
            jax.ShapeDtypeStruct((nv, m), BF16),
        ],
        compiler_params=pltpu.CompilerParams(
            dimension_semantics=("arbitrary",),
            vmem_limit_bytes=VMEM_LIMIT),
        name="latent",
    )(x2, pos_col, freq, g_pre, w_lat, g_q, w_qt, g_kv, w_k, w_vt)


def _blk_slice(idx, blk):
    if isinstance(idx, int):
        return slice(idx * blk, (idx + 1) * blk)
    return pl.ds(pl.multiple_of(idx * blk, blk), blk)


def _offdiag_pairs(nblk):
    return [(i, t) for i in range(1, nblk) for t in range(i)]


def _pair_table(nblk):
    return jnp.asarray(np.array(_offdiag_pairs(nblk), dtype=np.int32).T)


def _init_mask_bias(mb_ref, blk):
    kv = lax.broadcasted_iota(jnp.int32, (blk, blk), 0)
    qi = lax.broadcasted_iota(jnp.int32, (blk, blk), 1)
    mb_ref[...] = jnp.where(kv <= qi, 0.0, -jnp.inf).astype(F32)


def _run_pipeline(heads, tab_ref, nblk):
    A, B_DIAG, C_DIAG, B, C = range(5)
    pairs = _offdiag_pairs(nblk)
    npairs = len(pairs)
    assert (nblk - 2) % 2 == 0 and (npairs - 2) % 2 == 0

    def step(c=None, b=None, a=None):
        for call in (c, b, (A, a) if a is not None else None):
            if call is not None:
                for stages in heads:
                    stages[call[0]](*call[1])

    step(a=(0, 0))
    step(b=(B_DIAG, (0,)), a=(1, 1))

    def diag_loop(u, carry):
        for d in (1, 2):
            g = d + 2 * u
            step(c=(C_DIAG, (g - 1,)), b=(B_DIAG, (g,)), a=(g + 1, g + 1))
        return carry

    lax.fori_loop(0, (nblk - 2) // 2, diag_loop, 0)

    step(c=(C_DIAG, (nblk - 2,)), b=(B_DIAG, (nblk - 1,)), a=pairs[0])
    step(c=(C_DIAG, (nblk - 1,)), b=(B, pairs[0]), a=pairs[1])

    def off_loop(u, carry):
        pair = lambda f: (tab_ref[0, f], tab_ref[1, f])
        for d in (1, 2):
            f = d + 2 * u
            step(c=(C, pair(f - 1)), b=(B, pair(f)), a=pair(f + 1))
        return carry

    lax.fori_loop(0, (npairs - 2) // 2, off_loop, 0)

    step(c=(C, pairs[npairs - 2]), b=(B, pairs[npairs - 1]))
    step(c=(C, pairs[npairs - 1]))


def _mla_stages(qt_ref, k_ref, vt_ref, o_ref, mb_ref, s_ref, p_ref, al_ref, m_ref, l_ref, acc_ref,
                blk, nblk):
    def stage_a(i, t):
        s_ref[...] = _dot(k_ref[_blk_slice(t, blk), :], qt_ref[:, _blk_slice(i, blk)])

    def stage_b_diag(i):
        s = s_ref[...] + mb_ref[...]
        m = jnp.max(s, axis=0, keepdims=True)
        p = jnp.exp2(s - m)
        m_ref[i] = m
        l_ref[i] = jnp.sum(p, axis=0, keepdims=True)
        p_ref[...] = p.astype(BF16)

    def stage_c_diag(i):
        acc_ref[i] = _dot(vt_ref[:, _blk_slice(i, blk)], p_ref[...])

    def stage_b(i, t):
        s = s_ref[...]
        m_prev = m_ref[i]
        m_new = jnp.maximum(m_prev, jnp.max(s, axis=0, keepdims=True))
        alpha = jnp.exp2(m_prev - m_new)
        p = jnp.exp2(s - m_new)
        l_ref[i] = alpha * l_ref[i] + jnp.sum(p, axis=0, keepdims=True)
        m_ref[i] = m_new
        al_ref[...] = alpha
        p_ref[...] = p.astype(BF16)

    def stage_c(i, t):
        acc_ref[i] = al_ref[...] * acc_ref[i] + _dot(vt_ref[:, _blk_slice(t, blk)], p_ref[...])

    def finalize():
        for i in range(nblk):
            o_t = acc_ref[i] * (1.0 / l_ref[i])
            o_ref[_blk_slice(i, blk), :] = o_t.T.astype(BF16)

    return (stage_a, stage_b_diag, stage_c_diag, stage_b, stage_c), finalize


def _mla_scratch(blk, nblk):
    return [
        pltpu.VMEM((blk, blk), F32),
        pltpu.VMEM((blk, blk), BF16),
        pltpu.VMEM((1, blk), F32),
        pltpu.VMEM((nblk, 1, blk), F32),
        pltpu.VMEM((nblk, 1, blk), F32),
        pltpu.VMEM((nblk, MLA_V, blk), F32),
    ]


def _alibi_lane_tables(slopes_l2e):
    c1 = slopes_l2e.astype(BF16).astype(F32)
    c2 = (slopes_l2e - c1).astype(BF16).astype(F32)
    c3 = (slopes_l2e - c1 - c2).astype(BF16).astype(F32)
    cw = (jnp.stack([c1, c2, c3], axis=1)[:, :, None]
          * jnp.asarray([1.0, 256.0, 65536.0], F32)[None, None, :]).reshape(-1, ALIBI_LANES)
    heads = slopes_l2e.shape[0]
    ktab = jnp.zeros((heads, 2, LANE), F32)
    for x, base in enumerate(ALIBI_BASE):
        ktab = ktab.at[:, x, base:base + ALIBI_LANES].set(-cw)
    qcol = jnp.zeros((heads, DIFF_HEAD_DIM), F32).at[:, ALIBI_LANES:2 * ALIBI_LANES].set(cw)
    return ktab.astype(BF16), qcol.reshape(heads, DIFF_HEAD_DIM, 1)


def _diff_stages(qt_ref, k_ref, vt_ref, pc_ref, pr_ref, shr_ref, shc_ref, kt_ref, qc_ref,
                 lq1_ref, lk1_ref, lq2_ref, lk2_ref, gsub_ref, o_ref, mb_ref, dig_ref, digt_ref,
                 qm_ref, km_ref, s_ref, p_ref, al_ref, m_ref, l_ref, acc_ref, blk, nblk, lam_init):
    half = DIFF_HEAD_DIM

    @pl.when(pl.program_id(1) == 0)
    def _():
        pcol = pc_ref[...]
        prel = jnp.broadcast_to(pcol - pcol[0:1, :], dig_ref.shape)
        shift = jnp.broadcast_to(shr_ref[...], dig_ref.shape)
        dig_ref[...] = (lax.shift_right_logical(prel, shift) & 255).astype(F32).astype(BF16)
        prow = pr_ref[0]
        prel_t = jnp.broadcast_to(prow - prow[:, 0:1], digt_ref.shape)
        shift_t = jnp.broadcast_to(shc_ref[...], digt_ref.shape)
        digt_ref[...] = (lax.shift_right_logical(prel_t, shift_t) & 255).astype(F32).astype(BF16)

    row = lax.broadcasted_iota(jnp.int32, (half, 1), 0)
    alibi_q = jnp.where(row < ALIBI_LANES, digt_ref[...],
                        jnp.broadcast_to(qc_ref[0], digt_ref.shape).astype(BF16))
    qm_ref[0, :half, :] = qt_ref[:half, :]
    qm_ref[0, half:, :] = alibi_q
    qm_ref[1, :half, :] = alibi_q
    qm_ref[1, half:, :] = qt_ref[half:, :]
    lane = lax.broadcasted_iota(jnp.int32, (1, LANE), 1)
    dig = dig_ref[...]
    k = k_ref[...]
    for x, base in enumerate(ALIBI_BASE):
        own = (lane < half) if x == 0 else (lane >= half)
        k_digit = (lane >= base + ALIBI_LANES) & (lane < base + 2 * ALIBI_LANES)
        km_ref[x] = jnp.where(own, k, jnp.where(k_digit, dig, kt_ref[0, x:x + 1, :]))

    def stage_a(i, t):
        for x in range(2):
            s_ref[x] = _dot(km_ref[x, _blk_slice(t, blk), :], qm_ref[x, :, _blk_slice(i, blk)])

    def stage_b_diag(i):
        nb = mb_ref[...]
        for x in range(2):
            s = s_ref[x] + nb
            m = jnp.max(s, axis=0, keepdims=True)
            p = jnp.exp2(s - m)
            m_ref[x, i] = m
            l_ref[x, i] = jnp.sum(p, axis=0, keepdims=True)
            p_ref[x] = p.astype(BF16)

    def stage_c_diag(i):
        vt = vt_ref[:, _blk_slice(i, blk)]
        for x in range(2):
            acc_ref[x, i] = _dot(vt, p_ref[x])

    def stage_b(i, t):
        for x in range(2):
            s = s_ref[x]
            m_prev = m_ref[x, i]
            m_new = jnp.maximum(m_prev, jnp.max(s, axis=0, keepdims=True))
            alpha = jnp.exp2(m_prev - m_new)
            p = jnp.exp2(s - m_new)
            l_ref[x, i] = alpha * l_ref[x, i] + jnp.sum(p, axis=0, keepdims=True)
            m_ref[x, i] = m_new
            al_ref[x] = alpha
            p_ref[x] = p.astype(BF16)

    def stage_c(i, t):
        vt = vt_ref[:, _blk_slice(t, blk)]
        for x in range(2):
            acc_ref[x, i] = al_ref[x] * acc_ref[x, i] + _dot(vt, p_ref[x])

    def finalize():
        lam = (jnp.exp(jnp.sum(lq1_ref[...] * lk1_ref[...], axis=-1, keepdims=True))
               - jnp.exp(jnp.sum(lq2_ref[...] * lk2_ref[...], axis=-1, keepdims=True))
               + lam_init)
        for i in range(nblk):
            o_t = (acc_ref[0, i] * (1.0 / l_ref[0, i])
                   - lam * (acc_ref[1, i] * (1.0 / l_ref[1, i])))
            o = o_t.T
            o_ref[_blk_slice(i, blk), :] = (_rms(o, gsub_ref[...]) * (1.0 - lam_init)).astype(BF16)

    return (stage_a, stage_b_diag, stage_c_diag, stage_b, stage_c), finalize


def _diff_scratch(seq, blk, nblk):
    return [
        pltpu.VMEM((seq, LANE), BF16),
        pltpu.VMEM((DIFF_HEAD_DIM, seq), BF16),
        pltpu.VMEM((2, LANE, seq), BF16),
        pltpu.VMEM((2, seq, LANE), BF16),
        pltpu.VMEM((2, blk, blk), F32),
        pltpu.VMEM((2, blk, blk), BF16),
        pltpu.VMEM((2, 1, blk), F32),
        pltpu.VMEM((2, nblk, 1, blk), F32),
        pltpu.VMEM((2, nblk, 1, blk), F32),
        pltpu.VMEM((2, nblk, LANE, blk), F32),
    ]


N_MLA_IN, N_DIFF_IN = 3, 14
N_MLA_SCRATCH, N_DIFF_SCRATCH = 6, 10


def _attention_body(tab_ref, *refs, blk, nblk, lam_init):
    mla_in, refs = refs[:N_MLA_IN], refs[N_MLA_IN:]
    diff_in, refs = refs[:N_DIFF_IN], refs[N_DIFF_IN:]
    (o_mla_ref, o_diff_ref, mb_ref), refs = refs[:3], refs[3:]
    mla_scr, diff_scr = refs[:N_MLA_SCRATCH], refs[N_MLA_SCRATCH:]
    assert len(diff_scr) == N_DIFF_SCRATCH

    @pl.when((pl.program_id(0) == 0) & (pl.program_id(1) == 0))
    def _():
        _init_mask_bias(mb_ref, blk)

    mla, mla_fin = _mla_stages(*mla_in, o_mla_ref, mb_ref, *mla_scr, blk, nblk)
    diff, diff_fin = _diff_stages(*diff_in, o_diff_ref, mb_ref, *diff_scr, blk, nblk, lam_init)

    _run_pipeline([mla, diff], tab_ref, nblk)
    mla_fin()
    diff_fin()


def _attention(qt, k, vt, big_n, big_t, pos_icol, pos_irow, slopes_l2e, lq1, lk1, lq2, lk2, g_sub,
               batch, seq, blk, lam_init):
    assert MLA_HEADS == DIFF_HEADS
    nblk = seq // blk
    assert nblk >= 3
    width = DIFF_HEADS * 2 * DIFF_HEAD_DIM
    cb = width // LANE
    k_off = 1 * cb
    q_off, v_off = 0, 1 * cb
    k_tab, q_col = _alibi_lane_tables(slopes_l2e)
    digit_k = np.arange(LANE) % DIFF_HEAD_DIM % 3
    shift_row = jnp.asarray((digit_k * 8).reshape(1, LANE), jnp.int32)
    shift_col = jnp.asarray((digit_k[:DIFF_HEAD_DIM] * 8).reshape(DIFF_HEAD_DIM, 1), jnp.int32)
    small = lambda a: pl.BlockSpec(a.shape, lambda b, h: (0,) * a.ndim)
    mla_specs = [
        pl.BlockSpec((QK_PAD, seq), lambda b, h: (h, b)),
        pl.BlockSpec((seq, QK_PAD), lambda b, h: (b, h)),
        pl.BlockSpec((MLA_V, seq), lambda b, h: (h, b)),
    ]
    diff_specs = [
        pl.BlockSpec((LANE, seq), lambda b, h: (q_off + h, b)),
        pl.BlockSpec((seq, LANE), lambda b, h: (b, k_off + h)),
        pl.BlockSpec((LANE, seq), lambda b, h: (v_off + h, b)),
        pl.BlockSpec((seq, 1), lambda b, h: (b, 0)),
        pl.BlockSpec((1, 1, seq), lambda b, h: (b, 0, 0)),
        small(shift_row), small(shift_col),
        pl.BlockSpec((1, 2, LANE), lambda b, h: (h, 0, 0)),
        pl.BlockSpec((1, DIFF_HEAD_DIM, 1), lambda b, h: (h, 0, 0)),
        small(lq1), small(lk1), small(lq2), small(lk2), small(g_sub),
    ]
    assert len(mla_specs) == N_MLA_IN and len(diff_specs) == N_DIFF_IN
    out_spec = pl.BlockSpec((seq, LANE), lambda b, h: (b, h))
    return pl.pallas_call(
        functools.partial(_attention_body, blk=blk, nblk=nblk, lam_init=lam_init),
        grid=(batch, MLA_HEADS),
        in_specs=[pl.BlockSpec(memory_space=pltpu.SMEM)] + mla_specs + diff_specs,
        out_specs=[out_spec, out_spec],
        out_shape=[jax.ShapeDtypeStruct((batch * seq, MLA_HEADS * MLA_V), BF16),
                   jax.ShapeDtypeStruct((batch * seq, width), BF16)],
        scratch_shapes=([pltpu.VMEM((blk, blk), F32)]
                        + _mla_scratch(blk, nblk) + _diff_scratch(seq, blk, nblk)),
        compiler_params=pltpu.CompilerParams(
            dimension_semantics=("arbitrary", "arbitrary"),
            vmem_limit_bytes=VMEM_LIMIT),
        name="attention",
    )(_pair_table(nblk), qt, k, vt, big_t, big_n, big_t, pos_icol, pos_irow, shift_row, shift_col,
      k_tab, q_col, lq1, lk1, lq2, lk2, g_sub)


def _outproj_body(x_ref, om_ref, od_ref, gm_ref, gd_ref, w_ref, gpost_ref, o_ref):
    gm = gm_ref[...].astype(F32)
    gd = gd_ref[...].astype(F32)
    mm = (om_ref[...].astype(F32) * (gm * jax.nn.sigmoid(gm))).astype(BF16)
    md = (od_ref[...].astype(F32) * (gd * jax.nn.sigmoid(gd))).astype(BF16)
    half = mm.shape[1]
    y = _dot(mm, w_ref[:half, :]) + _dot(md, w_ref[half:, :])
    o_ref[...] = x_ref[...] + _rms(y, gpost_ref[...])


def _outproj(x2, o_mla, o_diff, big, w_out, g_post, tm):
    m, d = x2.shape
    half = o_mla.shape[1]
    gate_diff_blk = (big.shape[1] - half) // half
    return pl.pallas_call(
        _outproj_body,
        grid=(m // tm,),
        in_specs=[
            pl.BlockSpec((tm, d), lambda i: (i, 0)),
            pl.BlockSpec((tm, half), lambda i: (i, 0)),
            pl.BlockSpec((tm, half), lambda i: (i, 0)),
            pl.BlockSpec((tm, half), lambda i: (i, 0)),
            pl.BlockSpec((tm, half), lambda i: (i, gate_diff_blk)),
            pl.BlockSpec(w_out.shape, lambda i: (0, 0)),
            pl.BlockSpec((1, d), lambda i: (0, 0)),
        ],
        out_specs=pl.BlockSpec((tm, d), lambda i: (i, 0)),
        out_shape=jax.ShapeDtypeStruct((m, d), F32),
        compiler_params=pltpu.CompilerParams(
            dimension_semantics=("arbitrary",),
            vmem_limit_bytes=VMEM_LIMIT),
        name="outproj",
    )(x2, o_mla, o_diff, big, big, w_out, g_post)


def kernel(x, positions, g_pre, w_in, g_q_a, w_q_b, g_kv_a, w_kv_b, lambda_q1, lambda_k1,
           lambda_q2, lambda_k2, g_diff_sub, w_out, g_post):
    batch, seq, d = x.shape
    depth = g_pre.shape[0]
    q_rank = w_q_b.shape[1]
    kv_rank = w_kv_b.shape[1]
    half_rope = MLA_ROPE // 2
    lat_end = q_rank + kv_rank
    pe_end = lat_end + MLA_ROPE

    pos_icol = positions.astype(jnp.int32).reshape(batch * seq, 1)
    pos_irow = positions.astype(jnp.int32).reshape(batch, 1, seq)
    freqs = 1.0 / (ROPE_THETA ** (jnp.arange(0, MLA_ROPE, 2, dtype=F32) / MLA_ROPE))
    freq = jnp.tile(freqs, 4).reshape(1, LANE)
    slopes = 2.0 ** (-8.0 * (jnp.arange(DIFF_HEADS, dtype=F32) + 1.0) / DIFF_HEADS)
    slopes_l2e = slopes * LOG2E

    mla_scale = MLA_QK ** -0.5 * LOG2E
    diff_scale = DIFF_HEAD_DIM ** -0.5 * LOG2E
    width = DIFF_HEADS * 2 * DIFF_HEAD_DIM
    inproj_tn = 512
    row_scale = jnp.asarray([diff_scale] * (width // inproj_tn) + [1.0] * (width // inproj_tn), F32)

    x2 = x.reshape(batch * seq, d)
    for l in range(depth):
        lam_init = 0.8 - 0.6 * math.exp(-0.3 * l)
        w_lat, w_n, w_t = _regroup_w_in(w_in, l, lat_end, half_rope, width, rows=256)
        wq = w_q_b[l]
        t1 = wq[:, :, MLA_NOPE:MLA_NOPE + half_rope]
        t2 = wq[:, :, MLA_NOPE + half_rope:]
        w_q = jnp.concatenate([wq[:, :, :MLA_NOPE], t1, t2, t2, t1], axis=-1)
        w_qt = w_q.reshape(q_rank, MLA_HEADS * QK_PAD).T.astype(BF16)
        wkv = w_kv_b[l]
        w_k = wkv[:, :, :MLA_NOPE].reshape(kv_rank, -1).astype(BF16)
        w_vt = wkv[:, :, MLA_NOPE:].reshape(kv_rank, -1).T.astype(BF16)
        gp = g_pre[l].reshape(1, d)

        big_n, big_t = _inproj(x2, gp, w_n, w_t, row_scale, tm=512, tn=inproj_tn)
        qt, k, vt = _latent(x2, pos_icol, freq, gp, w_lat, g_q_a[l].reshape(1, -1), w_qt,
                            g_kv_a[l].reshape(1, -1), w_k, w_vt, tm=512, q_scale=mla_scale)
        o_mla, o_diff = _attention(
            qt, k, vt, big_n, big_t, pos_icol, pos_irow, slopes_l2e,
            lambda_q1[l].reshape(1, -1), lambda_k1[l].reshape(1, -1),
            lambda_q2[l].reshape(1, -1), lambda_k2[l].reshape(1, -1),
            g_diff_sub[l].reshape(1, -1), batch, seq, blk=512, lam_init=lam_init)
        x2 = _outproj(x2, o_mla, o_diff, big_n, w_out[l].astype(BF16), g_post[l].reshape(1, d), tm=512)
    return x2.reshape(batch, seq, d)
```

```python
import functools
import math

import numpy as np
import jax
import jax.numpy as jnp
from jax import lax
from jax.experimental import pallas as pl
from jax.experimental.pallas import tpu as pltpu

F32 = jnp.float32
BF16 = jnp.bfloat16

EPS = 1e-6
LOG2E = 1.4426950408889634
ROPE_THETA = 10000.0

MLA_HEADS = 8
MLA_NOPE = 128
MLA_ROPE = 64
MLA_V = 128
MLA_QK = MLA_NOPE + MLA_ROPE
DIFF_HEADS = 8
DIFF_HEAD_DIM = 64
LANE = 128
QK_PAD = 256
ALIBI_LANES = 9
ALIBI_BASE = (DIFF_HEAD_DIM, 0)

VMEM_LIMIT = 56 * 1024 * 1024


def _rms(xf, g):
    ms = jnp.mean(xf * xf, axis=-1, keepdims=True)
    return xf * lax.rsqrt(ms + EPS) * g


def _dot(a, b):
    return jnp.dot(a, b, preferred_element_type=F32)


def _dot_nt(a, b):
    return lax.dot_general(a, b, (((1,), (1,)), ((), ())), preferred_element_type=F32)


def _regroup_body(wt_ref, lat_ref, n_ref, t_ref, *, lat_end, half_rope, width):
    pe_end = lat_end + 2 * half_rope
    cols = lambda a, b: wt_ref[0, a:b, :]
    u1 = cols(lat_end, lat_end + half_rope)
    u2 = cols(lat_end + half_rope, pe_end)
    lat_ref[:, :lat_end] = cols(0, lat_end).T.astype(BF16)
    lat_ref[:, lat_end:] = jnp.concatenate([u1, u1, u2, u2, u2, u2, u1, u1], axis=0).T.astype(BF16)
    group = lambda n: cols(pe_end + n * width, pe_end + (n + 1) * width).T.astype(BF16)
    for dst, src in enumerate((0, 2, 4)):
        n_ref[:, dst * width:(dst + 1) * width] = group(src)
    for dst, src in enumerate((1, 3)):
        t_ref[:, dst * width:(dst + 1) * width] = group(src)


def _regroup_w_in(w_in, layer, lat_end, half_rope, width, rows):
    _, d, n_in = w_in.shape
    n_lat = lat_end + 8 * half_rope
    w_in_t = jnp.swapaxes(w_in, 1, 2)
    return pl.pallas_call(
        functools.partial(_regroup_body, lat_end=lat_end, half_rope=half_rope, width=width),
        grid=(d // rows,),
        in_specs=[pl.BlockSpec((1, n_in, rows), lambda i: (layer, 0, i))],
        out_specs=[
            pl.BlockSpec((rows, n_lat), lambda i: (i, 0)),
            pl.BlockSpec((rows, 3 * width), lambda i: (i, 0)),
            pl.BlockSpec((rows, 2 * width), lambda i: (i, 0)),
        ],
        out_shape=[
            jax.ShapeDtypeStruct((d, n_lat), BF16),
            jax.ShapeDtypeStruct((d, 3 * width), BF16),
            jax.ShapeDtypeStruct((d, 2 * width), BF16),
        ],
        compiler_params=pltpu.CompilerParams(
            dimension_semantics=("arbitrary",),
            vmem_limit_bytes=VMEM_LIMIT),
        name="regroup_w_in",
    )(w_in_t)


def _inproj_body(rs_ref, x_ref, g_ref, wn_ref, wt_ref, on_ref, ot_ref, *, tn):
    h = _rms(x_ref[...], g_ref[...]).astype(BF16)
    for c in range(wn_ref.shape[1] // tn):
        cols = slice(c * tn, (c + 1) * tn)
        on_ref[:, cols] = _dot(h, wn_ref[:, cols]).astype(BF16)
    for c in range(wt_ref.shape[1] // tn):
        cols = slice(c * tn, (c + 1) * tn)
        ot_ref[cols, :] = (_dot(h, wt_ref[:, cols]) * rs_ref[c]).T.astype(BF16)


def _inproj(x2, g_pre, w_n, w_t, row_scale, tm, tn):
    m, d = x2.shape
    resident = lambda a: pl.BlockSpec(a.shape, lambda i: (0,) * a.ndim, pipeline_mode=pl.Buffered(1))
    return pl.pallas_call(
        functools.partial(_inproj_body, tn=tn),
        grid=(m // tm,),
        in_specs=[
            pl.BlockSpec(memory_space=pltpu.SMEM),
            pl.BlockSpec((tm, d), lambda i: (i, 0)),
            pl.BlockSpec((1, d), lambda i: (0, 0)),
            resident(w_n),
            resident(w_t),
        ],
        out_specs=[
            pl.BlockSpec((tm, w_n.shape[1]), lambda i: (i, 0)),
            pl.BlockSpec((w_t.shape[1], tm), lambda i: (0, i)),
        ],
        out_shape=[
            jax.ShapeDtypeStruct((m, w_n.shape[1]), BF16),
            jax.ShapeDtypeStruct((w_t.shape[1], m), BF16),
        ],
        compiler_params=pltpu.CompilerParams(
            dimension_semantics=("arbitrary",),
            vmem_limit_bytes=VMEM_LIMIT),
        name="inproj",
    )(row_scale, x2, g_pre, w_n, w_t)


def _latent_body(x_ref, pos_ref, freq_ref, gpre_ref, wlat_ref, gq_ref, wqt_ref,
                 gkv_ref, wk_ref, wvt_ref, qt_ref, k_ref, vt_ref, *, q_scale):
    h = _rms(x_ref[...], gpre_ref[...]).astype(BF16)
    lat = _dot(h, wlat_ref[...])
    ang = pos_ref[...].astype(F32) * freq_ref[...]
    cos = jnp.cos(ang)
    sin = jnp.sin(ang)
    seg = lax.broadcasted_iota(jnp.int32, (1, LANE), 1) // (MLA_ROPE // 2)
    fq = jnp.where(seg == 1, -sin, jnp.where(seg == 3, sin, cos)) * q_scale
    sk = jnp.where(seg < 2, -sin, sin)
    k_ext = (lat[:, 768:896] * cos + lat[:, 896:1024] * sk).astype(BF16)

    c_q = _rms(lat[:, :512], gq_ref[...]).astype(BF16)
    qft = _dot_nt(wqt_ref[...], c_q)
    fqt = fq.T
    c_kv = _rms(lat[:, 512:768], gkv_ref[...]).astype(BF16)
    kf = _dot(c_kv, wk_ref[...])
    vt_ref[...] = _dot_nt(wvt_ref[...], c_kv).astype(BF16)
    for hd in range(MLA_HEADS):
        o = hd * QK_PAD
        qt_ref[o:o + LANE, :] = (qft[o:o + LANE, :] * q_scale).astype(BF16)
        qt_ref[o + LANE:o + QK_PAD, :] = (qft[o + LANE:o + QK_PAD, :] * fqt).astype(BF16)
        k_ref[:, o:o + LANE] = kf[:, hd * LANE:(hd + 1) * LANE].astype(BF16)
        k_ref[:, o + LANE:o + QK_PAD] = k_ext


def _latent(x2, pos_col, freq, g_pre, w_lat, g_q, w_qt, g_kv, w_k, w_vt, tm, q_scale):
    m, d = x2.shape
    nq = MLA_HEADS * QK_PAD
    nv = MLA_HEADS * MLA_V
    full = lambda a: pl.BlockSpec(a.shape, lambda i: (0,) * a.ndim)
    return pl.pallas_call(
        functools.partial(_latent_body, q_scale=q_scale),
        grid=(m // tm,),
        in_specs=[
            pl.BlockSpec((tm, d), lambda i: (i, 0)),
            pl.BlockSpec((tm, 1), lambda i: (i, 0)),
            full(freq), full(g_pre), full(w_lat), full(g_q), full(w_qt), full(g_kv), full(w_k),
            full(w_vt),
        ],
        out_specs=[
            pl.BlockSpec((nq, tm), lambda i: (0, i)),
            pl.BlockSpec((tm, nq), lambda i: (i, 0)),
            pl.BlockSpec((nv, tm), lambda i: (0, i)),
        ],
        out_shape=[
            jax.ShapeDtypeStruct((nq, m), BF16),
            jax.ShapeDtypeStruct((m, nq), BF16),
            jax.ShapeDtypeStruct((nv, m), BF16),
        ],
        compiler_params=pltpu.CompilerParams(
            dimension_semantics=("arbitrary",),
            vmem_limit_bytes=VMEM_LIMIT),
        name="latent",
    )(x2, pos_col, freq, g_pre, w_lat, g_q, w_qt, g_kv, w_k, w_vt)


def _blk_slice(idx, blk):
    if isinstance(idx, int):
        return slice(idx * blk, (idx + 1) * blk)
    return pl.ds(pl.multiple_of(idx * blk, blk), blk)


def _offdiag_pairs(nblk):
    return [(i, t) for i in range(1, nblk) for t in range(i)]


def _pair_table(nblk):
    return jnp.asarray(np.array(_offdiag_pairs(nblk), dtype=np.int32).T)


def _init_mask_bias(mb_ref, blk):
    kv = lax.broadcasted_iota(jnp.int32, (blk, blk), 0)
    qi = lax.broadcasted_iota(jnp.int32, (blk, blk), 1)
    mb_ref[...] = jnp.where(kv <= qi, 0.0, -jnp.inf).astype(F32)


def _run_pipeline(heads, tab_ref, nblk):
    A, B_DIAG, C_DIAG, B, C = range(5)
    pairs = _offdiag_pairs(nblk)
    npairs = len(pairs)
    assert (nblk - 2) % 2 == 0 and (npairs - 2) % 2 == 0

    def step(c=None, b=None, a=None):
        for call in (c, b, (A, a) if a is not None else None):
            if call is not None:
                for stages in heads:
                    stages[call[0]](*call[1])

    step(a=(0, 0))
    step(b=(B_DIAG, (0,)), a=(1, 1))

    def diag_loop(u, carry):
        for d in (1, 2):
            g = d + 2 * u
            step(c=(C_DIAG, (g - 1,)), b=(B_DIAG, (g,)), a=(g + 1, g + 1))
        return carry

    lax.fori_loop(0, (nblk - 2) // 2, diag_loop, 0)

    step(c=(C_DIAG, (nblk - 2,)), b=(B_DIAG, (nblk - 1,)), a=pairs[0])
    step(c=(C_DIAG, (nblk - 1,)), b=(B, pairs[0]), a=pairs[1])

    def off_loop(u, carry):
        pair = lambda f: (tab_ref[0, f], tab_ref[1, f])
        for d in (1, 2):
            f = d + 2 * u
            step(c=(C, pair(f - 1)), b=(B, pair(f)), a=pair(f + 1))
        return carry

    lax.fori_loop(0, (npairs - 2) // 2, off_loop, 0)

    step(c=(C, pairs[npairs - 2]), b=(B, pairs[npairs - 1]))
    step(c=(C, pairs[npairs - 1]))


def _mla_stages(qt_ref, k_ref, vt_ref, o_ref, mb_ref, s_ref, p_ref, al_ref, m_ref, l_ref, acc_ref,
                blk, nblk):
    def stage_a(i, t):
        s_ref[...] = _dot(k_ref[_blk_slice(t, blk), :], qt_ref[:, _blk_slice(i, blk)])

    def stage_b_diag(i):
        s = s_ref[...] + mb_ref[...]
        m = jnp.max(s, axis=0, keepdims=True)
        p = jnp.exp2(s - m)
        m_ref[i] = m
        l_ref[i] = jnp.sum(p, axis=0, keepdims=True)
        p_ref[...] = p.astype(BF16)

    def stage_c_diag(i):
        acc_ref[i] = _dot(vt_ref[:, _blk_slice(i, blk)], p_ref[...])

    def stage_b(i, t):
        s = s_ref[...]
        m_prev = m_ref[i]
        m_new = jnp.maximum(m_prev, jnp.max(s, axis=0, keepdims=True))
        alpha = jnp.exp2(m_prev - m_new)
        p = jnp.exp2(s - m_new)
        l_ref[i] = alpha * l_ref[i] + jnp.sum(p, axis=0, keepdims=True)
        m_ref[i] = m_new
        al_ref[...] = alpha
        p_ref[...] = p.astype(BF16)

    def stage_c(i, t):
        acc_ref[i] = al_ref[...] * acc_ref[i] + _dot(vt_ref[:, _blk_slice(t, blk)], p_ref[...])

    def finalize():
        for i in range(nblk):
            o_t = acc_ref[i] * (1.0 / l_ref[i])
            o_ref[_blk_slice(i, blk), :] = o_t.T.astype(BF16)

    return (stage_a, stage_b_diag, stage_c_diag, stage_b, stage_c), finalize


def _mla_scratch(blk, nblk):
    return [
        pltpu.VMEM((blk, blk), F32),
        pltpu.VMEM((blk, blk), BF16),
        pltpu.VMEM((1, blk), F32),
        pltpu.VMEM((nblk, 1, blk), F32),
        pltpu.VMEM((nblk, 1, blk), F32),
        pltpu.VMEM((nblk, MLA_V, blk), F32),
    ]


def _alibi_lane_tables(slopes_l2e):
    c1 = slopes_l2e.astype(BF16).astype(F32)
    c2 = (slopes_l2e - c1).astype(BF16).astype(F32)
    c3 = (slopes_l2e - c1 - c2).astype(BF16).astype(F32)
    cw = (jnp.stack([c1, c2, c3], axis=1)[:, :, None]
          * jnp.asarray([1.0, 256.0, 65536.0], F32)[None, None, :]).reshape(-1, ALIBI_LANES)
    heads = slopes_l2e.shape[0]
    ktab = jnp.zeros((heads, 2, LANE), F32)
    for x, base in enumerate(ALIBI_BASE):
        ktab = ktab.at[:, x, base:base + ALIBI_LANES].set(-cw)
    qcol = jnp.zeros((heads, DIFF_HEAD_DIM), F32).at[:, ALIBI_LANES:2 * ALIBI_LANES].set(cw)
    return ktab.astype(BF16), qcol.reshape(heads, DIFF_HEAD_DIM, 1)


def _diff_stages(qt_ref, k_ref, vt_ref, pc_ref, pr_ref, shr_ref, shc_ref, kt_ref, qc_ref,
                 lq1_ref, lk1_ref, lq2_ref, lk2_ref, gsub_ref, o_ref, mb_ref, dig_ref, digt_ref,
                 qm_ref, km_ref, s_ref, p_ref, al_ref, m_ref, l_ref, acc_ref, blk, nblk, lam_init):
    half = DIFF_HEAD_DIM

    @pl.when(pl.program_id(1) == 0)
    def _():
        pcol = pc_ref[...]
        prel = jnp.broadcast_to(pcol - pcol[0:1, :], dig_ref.shape)
        shift = jnp.broadcast_to(shr_ref[...], dig_ref.shape)
        dig_ref[...] = (lax.shift_right_logical(prel, shift) & 255).astype(F32).astype(BF16)
        prow = pr_ref[0]
        prel_t = jnp.broadcast_to(prow - prow[:, 0:1], digt_ref.shape)
        shift_t = jnp.broadcast_to(shc_ref[...], digt_ref.shape)
        digt_ref[...] = (lax.shift_right_logical(prel_t, shift_t) & 255).astype(F32).astype(BF16)

    row = lax.broadcasted_iota(jnp.int32, (half, 1), 0)
    alibi_q = jnp.where(row < ALIBI_LANES, digt_ref[...],
                        jnp.broadcast_to(qc_ref[0], digt_ref.shape).astype(BF16))
    qm_ref[0, :half, :] = qt_ref[:half, :]
    qm_ref[0, half:, :] = alibi_q
    qm_ref[1, :half, :] = alibi_q
    qm_ref[1, half:, :] = qt_ref[half:, :]
    lane = lax.broadcasted_iota(jnp.int32, (1, LANE), 1)
    dig = dig_ref[...]
    k = k_ref[...]
    for x, base in enumerate(ALIBI_BASE):
        own = (lane < half) if x == 0 else (lane >= half)
        k_digit = (lane >= base + ALIBI_LANES) & (lane < base + 2 * ALIBI_LANES)
        km_ref[x] = jnp.where(own, k, jnp.where(k_digit, dig, kt_ref[0, x:x + 1, :]))

    def stage_a(i, t):
        for x in range(2):
            s_ref[x] = _dot(km_ref[x, _blk_slice(t, blk), :], qm_ref[x, :, _blk_slice(i, blk)])

    def stage_b_diag(i):
        nb = mb_ref[...]
        for x in range(2):
            s = s_ref[x] + nb
            m = jnp.max(s, axis=0, keepdims=True)
            p = jnp.exp2(s - m)
            m_ref[x, i] = m
            l_ref[x, i] = jnp.sum(p, axis=0, keepdims=True)
            p_ref[x] = p.astype(BF16)

    def stage_c_diag(i):
        vt = vt_ref[:, _blk_slice(i, blk)]
        for x in range(2):
            acc_ref[x, i] = _dot(vt, p_ref[x])

    def stage_b(i, t):
        for x in range(2):
            s = s_ref[x]
            m_prev = m_ref[x, i]
            m_new = jnp.maximum(m_prev, jnp.max(s, axis=0, keepdims=True))
            alpha = jnp.exp2(m_prev - m_new)
            p = jnp.exp2(s - m_new)
            l_ref[x, i] = alpha * l_ref[x, i] + jnp.sum(p, axis=0, keepdims=True)
            m_ref[x, i] = m_new
            al_ref[x] = alpha
            p_ref[x] = p.astype(BF16)

    def stage_c(i, t):
        vt = vt_ref[:, _blk_slice(t, blk)]
        for x in range(2):
            acc_ref[x, i] = al_ref[x] * acc_ref[x, i] + _dot(vt, p_ref[x])

    def finalize():
        lam = (jnp.exp(jnp.sum(lq1_ref[...] * lk1_ref[...], axis=-1, keepdims=True))
               - jnp.exp(jnp.sum(lq2_ref[...] * lk2_ref[...], axis=-1, keepdims=True))
               + lam_init)
        for i in range(nblk):
            o_t = (acc_ref[0, i] * (1.0 / l_ref[0, i])
                   - lam * (acc_ref[1, i] * (1.0 / l_ref[1, i])))
            o = o_t.T
            o_ref[_blk_slice(i, blk), :] = (_rms(o, gsub_ref[...]) * (1.0 - lam_init)).astype(BF16)

    return (stage_a, stage_b_diag, stage_c_diag, stage_b, stage_c), finalize


def _diff_scratch(seq, blk, nblk):
    return [
        pltpu.VMEM((seq, LANE), BF16),
        pltpu.VMEM((DIFF_HEAD_DIM, seq), BF16),
        pltpu.VMEM((2, LANE, seq), BF16),
        pltpu.VMEM((2, seq, LANE), BF16),
        pltpu.VMEM((2, blk, blk), F32),
        pltpu.VMEM((2, blk, blk), BF16),
        pltpu.VMEM((2, 1, blk), F32),
        pltpu.VMEM((2, nblk, 1, blk), F32),
        pltpu.VMEM((2, nblk, 1, blk), F32),
        pltpu.VMEM((2, nblk, LANE, blk), F32),
    ]


N_MLA_IN, N_DIFF_IN = 3, 14
N_MLA_SCRATCH, N_DIFF_SCRATCH = 6, 10


def _attention_body(tab_ref, *refs, blk, nblk, lam_init):
    mla_in, refs = refs[:N_MLA_IN], refs[N_MLA_IN:]
    diff_in, refs = refs[:N_DIFF_IN], refs[N_DIFF_IN:]
    (o_mla_ref, o_diff_ref, mb_ref), refs = refs[:3], refs[3:]
    mla_scr, diff_scr = refs[:N_MLA_SCRATCH], refs[N_MLA_SCRATCH:]
    assert len(diff_scr) == N_DIFF_SCRATCH

    @pl.when((pl.program_id(0) == 0) & (pl.program_id(1) == 0))
    def _():
        _init_mask_bias(mb_ref, blk)

    mla, mla_fin = _mla_stages(*mla_in, o_mla_ref, mb_ref, *mla_scr, blk, nblk)
    diff, diff_fin = _diff_stages(*diff_in, o_diff_ref, mb_ref, *diff_scr, blk, nblk, lam_init)

    _run_pipeline([mla, diff], tab_ref, nblk)
    mla_fin()
    diff_fin()


def _attention(qt, k, vt, big_n, big_t, pos_icol, pos_irow, slopes_l2e, lq1, lk1, lq2, lk2, g_sub,
               batch, seq, blk, lam_init):
    assert MLA_HEADS == DIFF_HEADS
    nblk = seq // blk
    assert nblk >= 3
    width = DIFF_HEADS * 2 * DIFF_HEAD_DIM
    cb = width // LANE
    k_off = 1 * cb
    q_off, v_off = 0, 1 * cb
    k_tab, q_col = _alibi_lane_tables(slopes_l2e)
    digit_k = np.arange(LANE) % DIFF_HEAD_DIM % 3
    shift_row = jnp.asarray((digit_k * 8).reshape(1, LANE), jnp.int32)
    shift_col = jnp.asarray((digit_k[:DIFF_HEAD_DIM] * 8).reshape(DIFF_HEAD_DIM, 1), jnp.int32)
    small = lambda a: pl.BlockSpec(a.shape, lambda b, h: (0,) * a.ndim)
    mla_specs = [
        pl.BlockSpec((QK_PAD, seq), lambda b, h: (h, b)),
        pl.BlockSpec((seq, QK_PAD), lambda b, h: (b, h)),
        pl.BlockSpec((MLA_V, seq), lambda b, h: (h, b)),
    ]
    diff_specs = [
        pl.BlockSpec((LANE, seq), lambda b, h: (q_off + h, b)),
        pl.BlockSpec((seq, LANE), lambda b, h: (b, k_off + h)),
        pl.BlockSpec((LANE, seq), lambda b, h: (v_off + h, b)),
        pl.BlockSpec((seq, 1), lambda b, h: (b, 0)),
        pl.BlockSpec((1, 1, seq), lambda b, h: (b, 0, 0)),
        small(shift_row), small(shift_col),
        pl.BlockSpec((1, 2, LANE), lambda b, h: (h, 0, 0)),
        pl.BlockSpec((1, DIFF_HEAD_DIM, 1), lambda b, h: (h, 0, 0)),
        small(lq1), small(lk1), small(lq2), small(lk2), small(g_sub),
    ]
    assert len(mla_specs) == N_MLA_IN and len(diff_specs) == N_DIFF_IN
    out_spec = pl.BlockSpec((seq, LANE), lambda b, h: (b, h))
    return pl.pallas_call(
        functools.partial(_attention_body, blk=blk, nblk=nblk, lam_init=lam_init),
        grid=(batch, MLA_HEADS),
        in_specs=[pl.BlockSpec(memory_space=pltpu.SMEM)] + mla_specs + diff_specs,
        out_specs=[out_spec, out_spec],
        out_shape=[jax.ShapeDtypeStruct((batch * seq, MLA_HEADS * MLA_V), BF16),
                   jax.ShapeDtypeStruct((batch * seq, width), BF16)],
        scratch_shapes=([pltpu.VMEM((blk, blk), F32)]
                        + _mla_scratch(blk, nblk) + _diff_scratch(seq, blk, nblk)),
        compiler_params=pltpu.CompilerParams(
            dimension_semantics=("arbitrary", "arbitrary"),
            vmem_limit_bytes=VMEM_LIMIT),
        name="attention",
    )(_pair_table(nblk), qt, k, vt, big_t, big_n, big_t, pos_icol, pos_irow, shift_row, shift_col,
      k_tab, q_col, lq1, lk1, lq2, lk2, g_sub)


def _outproj_body(x_ref, om_ref, od_ref, gm_ref, gd_ref, w_ref, gpost_ref, o_ref):
    gm = gm_ref[...].astype(F32)
    gd = gd_ref[...].astype(F32)
    mm = (om_ref[...].astype(F32) * (gm * jax.nn.sigmoid(gm))).astype(BF16)
    md = (od_ref[...].astype(F32) * (gd * jax.nn.sigmoid(gd))).astype(BF16)
    half = mm.shape[1]
    y = _dot(mm, w_ref[:half, :]) + _dot(md, w_ref[half:, :])
    o_ref[...] = x_ref[...] + _rms(y, gpost_ref[...])


def _outproj(x2, o_mla, o_diff, big, w_out, g_post, tm):
    m, d = x2.shape
    half = o_mla.shape[1]
    gate_diff_blk = (big.shape[1] - half) // half
    return pl.pallas_call(
        _outproj_body,
        grid=(m // tm,),
        in_specs=[
            pl.BlockSpec((tm, d), lambda i: (i, 0)),
            pl.BlockSpec((tm, half), lambda i: (i, 0)),
            pl.BlockSpec((tm, half), lambda i: (i, 0)),
            pl.BlockSpec((tm, half), lambda i: (i, 0)),
            pl.BlockSpec((tm, half), lambda i: (i, gate_diff_blk)),
            pl.BlockSpec(w_out.shape, lambda i: (0, 0)),
            pl.BlockSpec((1, d), lambda i: (0, 0)),
        ],
        out_specs=pl.BlockSpec((tm, d), lambda i: (i, 0)),
        out_shape=jax.ShapeDtypeStruct((m, d), F32),
        compiler_params=pltpu.CompilerParams(
            dimension_semantics=("arbitrary",),
            vmem_limit_bytes=VMEM_LIMIT),
        name="outproj",
    )(x2, o_mla, o_diff, big, big, w_out, g_post)


def kernel(x, positions, g_pre, w_in, g_q_a, w_q_b, g_kv_a, w_kv_b, lambda_q1, lambda_k1,
           lambda_q2, lambda_k2, g_diff_sub, w_out, g_post):
    batch, seq, d = x.shape
    depth = g_pre.shape[0]
    q_rank = w_q_b.shape[1]
    kv_rank = w_kv_b.shape[1]
    half_rope = MLA_ROPE // 2
    lat_end = q_rank + kv_rank
    pe_end = lat_end + MLA_ROPE

    pos_icol = positions.astype(jnp.int32).reshape(batch * seq, 1)
    pos_irow = positions.astype(jnp.int32).reshape(batch, 1, seq)
    freqs = 1.0 / (ROPE_THETA ** (jnp.arange(0, MLA_ROPE, 2, dtype=F32) / MLA_ROPE))
    freq = jnp.tile(freqs, 4).reshape(1, LANE)
    slopes = 2.0 ** (-8.0 * (jnp.arange(DIFF_HEADS, dtype=F32) + 1.0) / DIFF_HEADS)
    slopes_l2e = slopes * LOG2E

    mla_scale = MLA_QK ** -0.5 * LOG2E
    diff_scale = DIFF_HEAD_DIM ** -0.5 * LOG2E
    width = DIFF_HEADS * 2 * DIFF_HEAD_DIM
    inproj_tn = 512
    row_scale = jnp.asarray([diff_scale] * (width // inproj_tn) + [1.0] * (width // inproj_tn), F32)

    x2 = x.reshape(batch * seq, d)
    for l in range(depth):
        lam_init = 0.8 - 0.6 * math.exp(-0.3 * l)
        w_lat, w_n, w_t = _regroup_w_in(w_in, l, lat_end, half_rope, width, rows=256)
        wq = w_q_b[l]
        t1 = wq[:, :, MLA_NOPE:MLA_NOPE + half_rope]
        t2 = wq[:, :, MLA_NOPE + half_rope:]
        w_q = jnp.concatenate([wq[:, :, :MLA_NOPE], t1, t2, t2, t1], axis=-1)
        w_qt = w_q.reshape(q_rank, MLA_HEADS * QK_PAD).T.astype(BF16)
        wkv = w_kv_b[l]
        w_k = wkv[:, :, :MLA_NOPE].reshape(kv_rank, -1).astype(BF16)
        w_vt = wkv[:, :, MLA_NOPE:].reshape(kv_rank, -1).T.astype(BF16)
        gp = g_pre[l].reshape(1, d)

        big_n, big_t = _inproj(x2, gp, w_n, w_t, row_scale, tm=512, tn=inproj_tn)
        qt, k, vt = _latent(x2, pos_icol, freq, gp, w_lat, g_q_a[l].reshape(1, -1), w_qt,
                            g_kv_a[l].reshape(1, -1), w_k, w_vt, tm=512, q_scale=mla_scale)
        o_mla, o_diff = _attention(
            qt, k, vt, big_n, big_t, pos_icol, pos_irow, slopes_l2e,
            lambda_q1[l].reshape(1, -1), lambda_k1[l].reshape(1, -1),
            lambda_q2[l].reshape(1, -1), lambda_k2[l].reshape(1, -1),
            g_diff_sub[l].reshape(1, -1), batch, seq, blk=512, lam_init=lam_init)
        x2 = _outproj(x2, o_mla, o_diff, big_n, w_out[l].astype(BF16), g_post[l].reshape(1, d), tm=512)
    return x2.reshape(batch, seq, d)
```

```python
import functools
import math

import numpy as np
import jax
import jax.numpy as jnp
from jax import lax
from jax.experimental import pallas as pl
from jax.experimental.pallas import tpu as pltpu

F32 = jnp.float32
BF16 = jnp.bfloat16

EPS = 1e-6
LOG2E = 1.4426950408889634
ROPE_THETA = 10000.0

MLA_HEADS = 8
MLA_NOPE = 128
MLA_ROPE = 64
MLA_V = 128
MLA_QK = MLA_NOPE + MLA_ROPE
DIFF_HEADS = 8
DIFF_HEAD_DIM = 64
LANE = 128
QK_PAD = 256
ALIBI_LANES = 9
ALIBI_BASE = (DIFF_HEAD_DIM, 0)

VMEM_LIMIT = 56 * 1024 * 1024


def _rms(xf, g):
    ms = jnp.mean(xf * xf, axis=-1, keepdims=True)
    return xf * lax.rsqrt(ms + EPS) * g


def _dot(a, b):
    return jnp.dot(a, b, preferred_element_type=F32)


def _dot_nt(a, b):
    return lax.dot_general(a, b, (((1,), (1,)), ((), ())), preferred_element_type=F32)


def _regroup_body(wt_ref, lat_ref, n_ref, t_ref, *, lat_end, half_rope, width):
    pe_end = lat_end + 2 * half_rope
    cols = lambda a, b: wt_ref[0, a:b, :]
    u1 = cols(lat_end, lat_end + half_rope)
    u2 = cols(lat_end + half_rope, pe_end)
    lat_ref[:, :lat_end] = cols(0, lat_end).T.astype(BF16)
    lat_ref[:, lat_end:] = jnp.concatenate([u1, u1, u2, u2, u2, u2, u1, u1], axis=0).T.astype(BF16)
    group = lambda n: cols(pe_end + n * width, pe_end + (n + 1) * width).T.astype(BF16)
    for dst, src in enumerate((0, 2, 4)):
        n_ref[:, dst * width:(dst + 1) * width] = group(src)
    for dst, src in enumerate((1, 3)):
        t_ref[:, dst * width:(dst + 1) * width] = group(src)


def _regroup_w_in(w_in, layer, lat_end, half_rope, width, rows):
    _, d, n_in = w_in.shape
    n_lat = lat_end + 8 * half_rope
    w_in_t = jnp.swapaxes(w_in, 1, 2)
    return pl.pallas_call(
        functools.partial(_regroup_body, lat_end=lat_end, half_rope=half_rope, width=width),
        grid=(d // rows,),
        in_specs=[pl.BlockSpec((1, n_in, rows), lambda i: (layer, 0, i))],
        out_specs=[
            pl.BlockSpec((rows, n_lat), lambda i: (i, 0)),
            pl.BlockSpec((rows, 3 * width), lambda i: (i, 0)),
            pl.BlockSpec((rows, 2 * width), lambda i: (i, 0)),
        ],
        out_shape=[
            jax.ShapeDtypeStruct((d, n_lat), BF16),
            jax.ShapeDtypeStruct((d, 3 * width), BF16),
            jax.ShapeDtypeStruct((d, 2 * width), BF16),
        ],
        compiler_params=pltpu.CompilerParams(
            dimension_semantics=("arbitrary",),
            vmem_limit_bytes=VMEM_LIMIT),
        name="regroup_w_in",
    )(w_in_t)


def _inproj_body(rs_ref, x_ref, g_ref, wn_ref, wt_ref, on_ref, ot_ref, *, tn):
    h = _rms(x_ref[...], g_ref[...]).astype(BF16)
    for c in range(wn_ref.shape[1] // tn):
        cols = slice(c * tn, (c + 1) * tn)
        on_ref[:, cols] = _dot(h, wn_ref[:, cols]).astype(BF16)
    for c in range(wt_ref.shape[1] // tn):
        cols = slice(c * tn, (c + 1) * tn)
        ot_ref[cols, :] = (_dot(h, wt_ref[:, cols]) * rs_ref[c]).T.astype(BF16)


def _inproj(x2, g_pre, w_n, w_t, row_scale, tm, tn):
    m, d = x2.shape
    resident = lambda a: pl.BlockSpec(a.shape, lambda i: (0,) * a.ndim, pipeline_mode=pl.Buffered(1))
    return pl.pallas_call(
        functools.partial(_inproj_body, tn=tn),
        grid=(m // tm,),
        in_specs=[
            pl.BlockSpec(memory_space=pltpu.SMEM),
            pl.BlockSpec((tm, d), lambda i: (i, 0)),
            pl.BlockSpec((1, d), lambda i: (0, 0)),
            resident(w_n),
            resident(w_t),
        ],
        out_specs=[
            pl.BlockSpec((tm, w_n.shape[1]), lambda i: (i, 0)),
            pl.BlockSpec((w_t.shape[1], tm), lambda i: (0, i)),
        ],
        out_shape=[
            jax.ShapeDtypeStruct((m, w_n.shape[1]), BF16),
            jax.ShapeDtypeStruct((w_t.shape[1], m), BF16),
        ],
        compiler_params=pltpu.CompilerParams(
            dimension_semantics=("arbitrary",),
            vmem_limit_bytes=VMEM_LIMIT),
        name="inproj",
    )(row_scale, x2, g_pre, w_n, w_t)


def _latent_body(x_ref, pos_ref, freq_ref, gpre_ref, wlat_ref, gq_ref, wqt_ref,
                 gkv_ref, wk_ref, wvt_ref, qt_ref, k_ref, vt_ref, *, q_scale):
    h = _rms(x_ref[...], gpre_ref[...]).astype(BF16)
    lat = _dot(h, wlat_ref[...])
    ang = pos_ref[...].astype(F32) * freq_ref[...]
    cos = jnp.cos(ang)
    sin = jnp.sin(ang)
    seg = lax.broadcasted_iota(jnp.int32, (1, LANE), 1) // (MLA_ROPE // 2)
    fq = jnp.where(seg == 1, -sin, jnp.where(seg == 3, sin, cos)) * q_scale
    sk = jnp.where(seg < 2, -sin, sin)
    k_ext = (lat[:, 768:896] * cos + lat[:, 896:1024] * sk).astype(BF16)

    c_q = _rms(lat[:, :512], gq_ref[...]).astype(BF16)
    qft = _dot_nt(wqt_ref[...], c_q)
    fqt = fq.T
    c_kv = _rms(lat[:, 512:768], gkv_ref[...]).astype(BF16)
    kf = _dot(c_kv, wk_ref[...])
    vt_ref[...] = _dot_nt(wvt_ref[...], c_kv).astype(BF16)
    for hd in range(MLA_HEADS):
        o = hd * QK_PAD
        qt_ref[o:o + LANE, :] = (qft[o:o + LANE, :] * q_scale).astype(BF16)
        qt_ref[o + LANE:o + QK_PAD, :] = (qft[o + LANE:o + QK_PAD, :] * fqt).astype(BF16)
        k_ref[:, o:o + LANE] = kf[:, hd * LANE:(hd + 1) * LANE].astype(BF16)
        k_ref[:, o + LANE:o + QK_PAD] = k_ext


def _latent(x2, pos_col, freq, g_pre, w_lat, g_q, w_qt, g_kv, w_k, w_vt, tm, q_scale):
    m, d = x2.shape
    nq = MLA_HEADS * QK_PAD
    nv = MLA_HEADS * MLA_V
    full = lambda a: pl.BlockSpec(a.shape, lambda i: (0,) * a.ndim)
    return pl.pallas_call(
        functools.partial(_latent_body, q_scale=q_scale),
        grid=(m // tm,),
        in_specs=[
            pl.BlockSpec((tm, d), lambda i: (i, 0)),
            pl.BlockSpec((tm, 1), lambda i: (i, 0)),
            full(freq), full(g_pre), full(w_lat), full(g_q), full(w_qt), full(g_kv), full(w_k),
            full(w_vt),
        ],
        out_specs=[
            pl.BlockSpec((nq, tm), lambda i: (0, i)),
            pl.BlockSpec((tm, nq), lambda i: (i, 0)),
            pl.BlockSpec((nv, tm), lambda i: (0, i)),
        ],
        out_shape=[
            jax.ShapeDtypeStruct((nq, m), BF16),
            jax.ShapeDtypeStruct((m, nq), BF16),
            jax.ShapeDtypeStruct((nv, m), BF16),
        ],
        compiler_params=pltpu.CompilerParams(
            dimension_semantics=("arbitrary",),
            vmem_limit_bytes=VMEM_LIMIT),
        name="latent",
    )(x2, pos_col, freq, g_pre, w_lat, g_q, w_qt, g_kv, w_k, w_vt)


def _blk_slice(idx, blk):
    if isinstance(idx, int):
        return slice(idx * blk, (idx + 1) * blk)
    return pl.ds(pl.multiple_of(idx * blk, blk), blk)


def _offdiag_pairs(nblk):
    return [(i, t) for i in range(1, nblk) for t in range(i)]


def _pair_table(nblk):
    return jnp.asarray(np.array(_offdiag_pairs(nblk), dtype=np.int32).T)


def _half_slices(idx, blk):
    half = blk // 2
    if isinstance(idx, int):
        return slice(idx * blk, idx * blk + half), slice(idx * blk + half, (idx + 1) * blk)
    start = pl.multiple_of(idx * blk, blk)
    return pl.ds(start, half), pl.ds(pl.multiple_of(start + half, half), half)


def _init_mask_bias(mb_ref):
    kv = lax.broadcasted_iota(jnp.int32, mb_ref.shape, 0)
    qi = lax.broadcasted_iota(jnp.int32, mb_ref.shape, 1)
    mb_ref[...] = jnp.where(kv <= qi, 0.0, -jnp.inf).astype(F32)


def _diag_softmax(s_top, s_bot, mb):
    h = s_bot.shape[0]
    a0 = s_top[:, :h] + mb
    a1 = s_top[:, h:]
    b1 = s_bot + mb
    m0 = jnp.max(a0, axis=0, keepdims=True)
    m1 = jnp.maximum(jnp.max(a1, axis=0, keepdims=True), jnp.max(b1, axis=0, keepdims=True))
    p0 = jnp.exp2(a0 - m0)
    pa1 = jnp.exp2(a1 - m1)
    pb1 = jnp.exp2(b1 - m1)
    l0 = jnp.sum(p0, axis=0, keepdims=True)
    l1 = jnp.sum(pa1, axis=0, keepdims=True) + jnp.sum(pb1, axis=0, keepdims=True)
    return (jnp.concatenate([m0, m1], axis=1), jnp.concatenate([l0, l1], axis=1),
            jnp.concatenate([p0, pa1], axis=1), pb1)


def _diag_pv(vt_top, vt_bot, p_top, p_bot):
    h = p_bot.shape[0]
    a = _dot(vt_top, p_top)
    b = _dot(vt_bot, p_bot)
    return jnp.concatenate([a[:, :h], a[:, h:] + b], axis=1)


def _run_pipeline(heads, tab_ref, nblk):
    A, A_DIAG, B_DIAG, C_DIAG, B, C = range(6)
    pairs = _offdiag_pairs(nblk)
    npairs = len(pairs)
    assert (nblk - 2) % 2 == 0 and (npairs - 2) % 2 == 0

    def step(c=None, b=None, a=None):
        for call in (c, b, a):
            if call is not None:
                for stages in heads:
                    stages[call[0]](*call[1])

    step(a=(A_DIAG, (0,)))
    step(b=(B_DIAG, (0,)), a=(A_DIAG, (1,)))

    def diag_loop(u, carry):
        for d in (1, 2):
            g = d + 2 * u
            step(c=(C_DIAG, (g - 1,)), b=(B_DIAG, (g,)), a=(A_DIAG, (g + 1,)))
        return carry

    lax.fori_loop(0, (nblk - 2) // 2, diag_loop, 0)

    step(c=(C_DIAG, (nblk - 2,)), b=(B_DIAG, (nblk - 1,)), a=(A, pairs[0]))
    step(c=(C_DIAG, (nblk - 1,)), b=(B, pairs[0]), a=(A, pairs[1]))

    def off_loop(u, carry):
        pair = lambda f: (tab_ref[0, f], tab_ref[1, f])
        for d in (1, 2):
            f = d + 2 * u
            step(c=(C, pair(f - 1)), b=(B, pair(f)), a=(A, pair(f + 1)))
        return carry

    lax.fori_loop(0, (npairs - 2) // 2, off_loop, 0)

    step(c=(C, pairs[npairs - 2]), b=(B, pairs[npairs - 1]))
    step(c=(C, pairs[npairs - 1]))


def _mla_stages(qt_ref, k_ref, vt_ref, o_ref, mb_ref, s_ref, p_ref, al_ref, m_ref, l_ref, acc_ref,
                blk, nblk):
    half = blk // 2

    def stage_a(i, t):
        s_ref[...] = _dot(k_ref[_blk_slice(t, blk), :], qt_ref[:, _blk_slice(i, blk)])

    def stage_a_diag(i):
        top, bot = _half_slices(i, blk)
        s_ref[:half, :] = _dot(k_ref[top, :], qt_ref[:, _blk_slice(i, blk)])
        s_ref[half:, half:] = _dot(k_ref[bot, :], qt_ref[:, bot])

    def stage_b_diag(i):
        m, l, p_top, p_bot = _diag_softmax(s_ref[:half, :], s_ref[half:, half:], mb_ref[...])
        m_ref[i] = m
        l_ref[i] = l
        p_ref[:half, :] = p_top.astype(BF16)
        p_ref[half:, half:] = p_bot.astype(BF16)

    def stage_c_diag(i):
        top, bot = _half_slices(i, blk)
        acc_ref[i] = _diag_pv(vt_ref[:, top], vt_ref[:, bot], p_ref[:half, :], p_ref[half:, half:])

    def stage_b(i, t):
        s = s_ref[...]
        m_prev = m_ref[i]
        m_new = jnp.maximum(m_prev, jnp.max(s, axis=0, keepdims=True))
        alpha = jnp.exp2(m_prev - m_new)
        p = jnp.exp2(s - m_new)
        l_ref[i] = alpha * l_ref[i] + jnp.sum(p, axis=0, keepdims=True)
        m_ref[i] = m_new
        al_ref[...] = alpha
        p_ref[...] = p.astype(BF16)

    def stage_c(i, t):
        acc_ref[i] = al_ref[...] * acc_ref[i] + _dot(vt_ref[:, _blk_slice(t, blk)], p_ref[...])

    def finalize():
        for i in range(nblk):
            o_t = acc_ref[i] * (1.0 / l_ref[i])
            o_ref[_blk_slice(i, blk), :] = o_t.T.astype(BF16)

    return (stage_a, stage_a_diag, stage_b_diag, stage_c_diag, stage_b, stage_c), finalize


def _mla_scratch(blk, nblk):
    return [
        pltpu.VMEM((blk, blk), F32),
        pltpu.VMEM((blk, blk), BF16),
        pltpu.VMEM((1, blk), F32),
        pltpu.VMEM((nblk, 1, blk), F32),
        pltpu.VMEM((nblk, 1, blk), F32),
        pltpu.VMEM((nblk, MLA_V, blk), F32),
    ]


def _alibi_lane_tables(slopes_l2e):
    c1 = slopes_l2e.astype(BF16).astype(F32)
    c2 = (slopes_l2e - c1).astype(BF16).astype(F32)
    c3 = (slopes_l2e - c1 - c2).astype(BF16).astype(F32)
    cw = (jnp.stack([c1, c2, c3], axis=1)[:, :, None]
          * jnp.asarray([1.0, 256.0, 65536.0], F32)[None, None, :]).reshape(-1, ALIBI_LANES)
    heads = slopes_l2e.shape[0]
    ktab = jnp.zeros((heads, 2, LANE), F32)
    for x, base in enumerate(ALIBI_BASE):
        ktab = ktab.at[:, x, base:base + ALIBI_LANES].set(-cw)
    qcol = jnp.zeros((heads, DIFF_HEAD_DIM), F32).at[:, ALIBI_LANES:2 * ALIBI_LANES].set(cw)
    return ktab.astype(BF16), qcol.reshape(heads, DIFF_HEAD_DIM, 1)


def _diff_stages(qt_ref, k_ref, vt_ref, pc_ref, pr_ref, shr_ref, shc_ref, kt_ref, qc_ref,
                 lq1_ref, lk1_ref, lq2_ref, lk2_ref, gsub_ref, o_ref, mb_ref, dig_ref, digt_ref,
                 qm_ref, km_ref, s_ref, p_ref, al_ref, m_ref, l_ref, acc_ref, blk, nblk, lam_init):
    half = DIFF_HEAD_DIM

    @pl.when(pl.program_id(1) == 0)
    def _():
        pcol = pc_ref[...]
        prel = jnp.broadcast_to(pcol - pcol[0:1, :], dig_ref.shape)
        shift = jnp.broadcast_to(shr_ref[...], dig_ref.shape)
        dig_ref[...] = (lax.shift_right_logical(prel, shift) & 255).astype(F32).astype(BF16)
        prow = pr_ref[0]
        prel_t = jnp.broadcast_to(prow - prow[:, 0:1], digt_ref.shape)
        shift_t = jnp.broadcast_to(shc_ref[...], digt_ref.shape)
        digt_ref[...] = (lax.shift_right_logical(prel_t, shift_t) & 255).astype(F32).astype(BF16)

    row = lax.broadcasted_iota(jnp.int32, (half, 1), 0)
    alibi_q = jnp.where(row < ALIBI_LANES, digt_ref[...],
                        jnp.broadcast_to(qc_ref[0], digt_ref.shape).astype(BF16))
    qm_ref[0, :half, :] = qt_ref[:half, :]
    qm_ref[0, half:, :] = alibi_q
    qm_ref[1, :half, :] = alibi_q
    qm_ref[1, half:, :] = qt_ref[half:, :]
    lane = lax.broadcasted_iota(jnp.int32, (1, LANE), 1)
    dig = dig_ref[...]
    k = k_ref[...]
    for x, base in enumerate(ALIBI_BASE):
        own = (lane < half) if x == 0 else (lane >= half)
        k_digit = (lane >= base + ALIBI_LANES) & (lane < base + 2 * ALIBI_LANES)
        km_ref[x] = jnp.where(own, k, jnp.where(k_digit, dig, kt_ref[0, x:x + 1, :]))

    def stage_a(i, t):
        for x in range(2):
            s_ref[x] = _dot(km_ref[x, _blk_slice(t, blk), :], qm_ref[x, :, _blk_slice(i, blk)])

    hb = blk // 2

    def stage_a_diag(i):
        top, bot = _half_slices(i, blk)
        for x in range(2):
            s_ref[x, :hb, :] = _dot(km_ref[x, top, :], qm_ref[x, :, _blk_slice(i, blk)])
            s_ref[x, hb:, hb:] = _dot(km_ref[x, bot, :], qm_ref[x, :, bot])

    def stage_b_diag(i):
        for x in range(2):
            m, l, p_top, p_bot = _diag_softmax(s_ref[x, :hb, :], s_ref[x, hb:, hb:], mb_ref[...])
            m_ref[x, i] = m
            l_ref[x, i] = l
            p_ref[x, :hb, :] = p_top.astype(BF16)
            p_ref[x, hb:, hb:] = p_bot.astype(BF16)

    def stage_c_diag(i):
        top, bot = _half_slices(i, blk)
        for x in range(2):
            acc_ref[x, i] = _diag_pv(vt_ref[:, top], vt_ref[:, bot], p_ref[x, :hb, :], p_ref[x, hb:, hb:])

    def stage_b(i, t):
        for x in range(2):
            s = s_ref[x]
            m_prev = m_ref[x, i]
            m_new = jnp.maximum(m_prev, jnp.max(s, axis=0, keepdims=True))
            alpha = jnp.exp2(m_prev - m_new)
            p = jnp.exp2(s - m_new)
            l_ref[x, i] = alpha * l_ref[x, i] + jnp.sum(p, axis=0, keepdims=True)
            m_ref[x, i] = m_new
            al_ref[x] = alpha
            p_ref[x] = p.astype(BF16)

    def stage_c(i, t):
        vt = vt_ref[:, _blk_slice(t, blk)]
        for x in range(2):
            acc_ref[x, i] = al_ref[x] * acc_ref[x, i] + _dot(vt, p_ref[x])

    def finalize():
        lam = (jnp.exp(jnp.sum(lq1_ref[...] * lk1_ref[...], axis=-1, keepdims=True))
               - jnp.exp(jnp.sum(lq2_ref[...] * lk2_ref[...], axis=-1, keepdims=True))
               + lam_init)
        for i in range(nblk):
            o_t = (acc_ref[0, i] * (1.0 / l_ref[0, i])
                   - lam * (acc_ref[1, i] * (1.0 / l_ref[1, i])))
            o = o_t.T
            o_ref[_blk_slice(i, blk), :] = (_rms(o, gsub_ref[...]) * (1.0 - lam_init)).astype(BF16)

    return (stage_a, stage_a_diag, stage_b_diag, stage_c_diag, stage_b, stage_c), finalize


def _diff_scratch(seq, blk, nblk):
    return [
        pltpu.VMEM((seq, LANE), BF16),
        pltpu.VMEM((DIFF_HEAD_DIM, seq), BF16),
        pltpu.VMEM((2, LANE, seq), BF16),
        pltpu.VMEM((2, seq, LANE), BF16),
        pltpu.VMEM((2, blk, blk), F32),
        pltpu.VMEM((2, blk, blk), BF16),
        pltpu.VMEM((2, 1, blk), F32),
        pltpu.VMEM((2, nblk, 1, blk), F32),
        pltpu.VMEM((2, nblk, 1, blk), F32),
        pltpu.VMEM((2, nblk, LANE, blk), F32),
    ]


N_MLA_IN, N_DIFF_IN = 3, 14
N_MLA_SCRATCH, N_DIFF_SCRATCH = 6, 10


def _attention_body(tab_ref, *refs, blk, nblk, lam_init):
    mla_in, refs = refs[:N_MLA_IN], refs[N_MLA_IN:]
    diff_in, refs = refs[:N_DIFF_IN], refs[N_DIFF_IN:]
    (o_mla_ref, o_diff_ref, mb_ref), refs = refs[:3], refs[3:]
    mla_scr, diff_scr = refs[:N_MLA_SCRATCH], refs[N_MLA_SCRATCH:]
    assert len(diff_scr) == N_DIFF_SCRATCH

    @pl.when((pl.program_id(0) == 0) & (pl.program_id(1) == 0))
    def _():
        _init_mask_bias(mb_ref)

    mla, mla_fin = _mla_stages(*mla_in, o_mla_ref, mb_ref, *mla_scr, blk, nblk)
    diff, diff_fin = _diff_stages(*diff_in, o_diff_ref, mb_ref, *diff_scr, blk, nblk, lam_init)

    _run_pipeline([mla, diff], tab_ref, nblk)
    mla_fin()
    diff_fin()


def _attention(qt, k, vt, big_n, big_t, pos_icol, pos_irow, slopes_l2e, lq1, lk1, lq2, lk2, g_sub,
               batch, seq, blk, lam_init):
    assert MLA_HEADS == DIFF_HEADS
    nblk = seq // blk
    assert nblk >= 3
    width = DIFF_HEADS * 2 * DIFF_HEAD_DIM
    cb = width // LANE
    k_off = 1 * cb
    q_off, v_off = 0, 1 * cb
    k_tab, q_col = _alibi_lane_tables(slopes_l2e)
    digit_k = np.arange(LANE) % DIFF_HEAD_DIM % 3
    shift_row = jnp.asarray((digit_k * 8).reshape(1, LANE), jnp.int32)
    shift_col = jnp.asarray((digit_k[:DIFF_HEAD_DIM] * 8).reshape(DIFF_HEAD_DIM, 1), jnp.int32)
    small = lambda a: pl.BlockSpec(a.shape, lambda b, h: (0,) * a.ndim)
    mla_specs = [
        pl.BlockSpec((QK_PAD, seq), lambda b, h: (h, b)),
        pl.BlockSpec((seq, QK_PAD), lambda b, h: (b, h)),
        pl.BlockSpec((MLA_V, seq), lambda b, h: (h, b)),
    ]
    diff_specs = [
        pl.BlockSpec((LANE, seq), lambda b, h: (q_off + h, b)),
        pl.BlockSpec((seq, LANE), lambda b, h: (b, k_off + h)),
        pl.BlockSpec((LANE, seq), lambda b, h: (v_off + h, b)),
        pl.BlockSpec((seq, 1), lambda b, h: (b, 0)),
        pl.BlockSpec((1, 1, seq), lambda b, h: (b, 0, 0)),
        small(shift_row), small(shift_col),
        pl.BlockSpec((1, 2, LANE), lambda b, h: (h, 0, 0)),
        pl.BlockSpec((1, DIFF_HEAD_DIM, 1), lambda b, h: (h, 0, 0)),
        small(lq1), small(lk1), small(lq2), small(lk2), small(g_sub),
    ]
    assert len(mla_specs) == N_MLA_IN and len(diff_specs) == N_DIFF_IN
    out_spec = pl.BlockSpec((seq, LANE), lambda b, h: (b, h))
    return pl.pallas_call(
        functools.partial(_attention_body, blk=blk, nblk=nblk, lam_init=lam_init),
        grid=(batch, MLA_HEADS),
        in_specs=[pl.BlockSpec(memory_space=pltpu.SMEM)] + mla_specs + diff_specs,
        out_specs=[out_spec, out_spec],
        out_shape=[jax.ShapeDtypeStruct((batch * seq, MLA_HEADS * MLA_V), BF16),
                   jax.ShapeDtypeStruct((batch * seq, width), BF16)],
        scratch_shapes=([pltpu.VMEM((blk // 2, blk // 2), F32)]
                        + _mla_scratch(blk, nblk) + _diff_scratch(seq, blk, nblk)),
        compiler_params=pltpu.CompilerParams(
            dimension_semantics=("arbitrary", "arbitrary"),
            vmem_limit_bytes=VMEM_LIMIT),
        name="attention",
    )(_pair_table(nblk), qt, k, vt, big_t, big_n, big_t, pos_icol, pos_irow, shift_row, shift_col,
      k_tab, q_col, lq1, lk1, lq2, lk2, g_sub)


def _outproj_body(x_ref, om_ref, od_ref, gm_ref, gd_ref, w_ref, gpost_ref, o_ref):
    gm = gm_ref[...].astype(F32)
    gd = gd_ref[...].astype(F32)
    mm = (om_ref[...].astype(F32) * (gm * jax.nn.sigmoid(gm))).astype(BF16)
    md = (od_ref[...].astype(F32) * (gd * jax.nn.sigmoid(gd))).astype(BF16)
    half = mm.shape[1]
    y = _dot(mm, w_ref[:half, :]) + _dot(md, w_ref[half:, :])
    o_ref[...] = x_ref[...] + _rms(y, gpost_ref[...])


def _outproj(x2, o_mla, o_diff, big, w_out, g_post, tm):
    m, d = x2.shape
    half = o_mla.shape[1]
    gate_diff_blk = (big.shape[1] - half) // half
    return pl.pallas_call(
        _outproj_body,
        grid=(m // tm,),
        in_specs=[
            pl.BlockSpec((tm, d), lambda i: (i, 0)),
            pl.BlockSpec((tm, half), lambda i: (i, 0)),
            pl.BlockSpec((tm, half), lambda i: (i, 0)),
            pl.BlockSpec((tm, half), lambda i: (i, 0)),
            pl.BlockSpec((tm, half), lambda i: (i, gate_diff_blk)),
            pl.BlockSpec(w_out.shape, lambda i: (0, 0)),
            pl.BlockSpec((1, d), lambda i: (0, 0)),
        ],
        out_specs=pl.BlockSpec((tm, d), lambda i: (i, 0)),
        out_shape=jax.ShapeDtypeStruct((m, d), F32),
        compiler_params=pltpu.CompilerParams(
            dimension_semantics=("arbitrary",),
            vmem_limit_bytes=VMEM_LIMIT),
        name="outproj",
    )(x2, o_mla, o_diff, big, big, w_out, g_post)


def kernel(x, positions, g_pre, w_in, g_q_a, w_q_b, g_kv_a, w_kv_b, lambda_q1, lambda_k1,
           lambda_q2, lambda_k2, g_diff_sub, w_out, g_post):
    batch, seq, d = x.shape
    depth = g_pre.shape[0]
    q_rank = w_q_b.shape[1]
    kv_rank = w_kv_b.shape[1]
    half_rope = MLA_ROPE // 2
    lat_end = q_rank + kv_rank
    pe_end = lat_end + MLA_ROPE

    pos_icol = positions.astype(jnp.int32).reshape(batch * seq, 1)
    pos_irow = positions.astype(jnp.int32).reshape(batch, 1, seq)
    freqs = 1.0 / (ROPE_THETA ** (jnp.arange(0, MLA_ROPE, 2, dtype=F32) / MLA_ROPE))
    freq = jnp.tile(freqs, 4).reshape(1, LANE)
    slopes = 2.0 ** (-8.0 * (jnp.arange(DIFF_HEADS, dtype=F32) + 1.0) / DIFF_HEADS)
    slopes_l2e = slopes * LOG2E

    mla_scale = MLA_QK ** -0.5 * LOG2E
    diff_scale = DIFF_HEAD_DIM ** -0.5 * LOG2E
    width = DIFF_HEADS * 2 * DIFF_HEAD_DIM
    inproj_tn = 512
    row_scale = jnp.asarray([diff_scale] * (width // inproj_tn) + [1.0] * (width // inproj_tn), F32)

    x2 = x.reshape(batch * seq, d)
    for l in range(depth):
        lam_init = 0.8 - 0.6 * math.exp(-0.3 * l)
        w_lat, w_n, w_t = _regroup_w_in(w_in, l, lat_end, half_rope, width, rows=256)
        wq = w_q_b[l]
        t1 = wq[:, :, MLA_NOPE:MLA_NOPE + half_rope]
        t2 = wq[:, :, MLA_NOPE + half_rope:]
        w_q = jnp.concatenate([wq[:, :, :MLA_NOPE], t1, t2, t2, t1], axis=-1)
        w_qt = w_q.reshape(q_rank, MLA_HEADS * QK_PAD).T.astype(BF16)
        wkv = w_kv_b[l]
        w_k = wkv[:, :, :MLA_NOPE].reshape(kv_rank, -1).astype(BF16)
        w_vt = wkv[:, :, MLA_NOPE:].reshape(kv_rank, -1).T.astype(BF16)
        gp = g_pre[l].reshape(1, d)

        big_n, big_t = _inproj(x2, gp, w_n, w_t, row_scale, tm=512, tn=inproj_tn)
        qt, k, vt = _latent(x2, pos_icol, freq, gp, w_lat, g_q_a[l].reshape(1, -1), w_qt,
                            g_kv_a[l].reshape(1, -1), w_k, w_vt, tm=512, q_scale=mla_scale)
        o_mla, o_diff = _attention(
            qt, k, vt, big_n, big_t, pos_icol, pos_irow, slopes_l2e,
            lambda_q1[l].reshape(1, -1), lambda_k1[l].reshape(1, -1),
            lambda_q2[l].reshape(1, -1), lambda_k2[l].reshape(1, -1),
            g_diff_sub[l].reshape(1, -1), batch, seq, blk=512, lam_init=lam_init)
        x2 = _outproj(x2, o_mla, o_diff, big_n, w_out[l].astype(BF16), g_post[l].reshape(1, d), tm=512)
    return x2.reshape(batch, seq, d)
```

```python
import functools
import math

import numpy as np
import jax
import jax.numpy as jnp
from jax import lax
from jax.experimental import pallas as pl
from jax.experimental.pallas import tpu as pltpu

F32 = jnp.float32
BF16 = jnp.bfloat16

EPS = 1e-6
LOG2E = 1.4426950408889634
ROPE_THETA = 10000.0

MLA_HEADS = 8
MLA_NOPE = 128
MLA_ROPE = 64
MLA_V = 128
VT_ROWS = MLA_V + 16
MLA_QK = MLA_NOPE + MLA_ROPE
DIFF_HEADS = 8
DIFF_HEAD_DIM = 64
LANE = 128
QK_PAD = 256
ALIBI_LANES = 9
ALIBI_BASE = (DIFF_HEAD_DIM, 0)

VMEM_LIMIT = 56 * 1024 * 1024


def _rms(xf, g):
    ms = jnp.mean(xf * xf, axis=-1, keepdims=True)
    return xf * lax.rsqrt(ms + EPS) * g


def _dot(a, b):
    return jnp.dot(a, b, preferred_element_type=F32)


def _dot_nt(a, b):
    return lax.dot_general(a, b, (((1,), (1,)), ((), ())), preferred_element_type=F32)


def _store_vt_ext(vt_ref, head, vt_head):
    r = head * VT_ROWS
    vt_ref[r:r + MLA_V, :] = vt_head
    vt_ref[r + MLA_V:r + VT_ROWS, :] = jnp.ones((VT_ROWS - MLA_V, vt_head.shape[1]), vt_head.dtype)


def _regroup_body(wt_ref, lat_ref, n_ref, t_ref, *, lat_end, half_rope, width):
    pe_end = lat_end + 2 * half_rope
    cols = lambda a, b: wt_ref[0, a:b, :]
    u1 = cols(lat_end, lat_end + half_rope)
    u2 = cols(lat_end + half_rope, pe_end)
    lat_ref[:, :lat_end] = cols(0, lat_end).T.astype(BF16)
    lat_ref[:, lat_end:] = jnp.concatenate([u1, u1, u2, u2, u2, u2, u1, u1], axis=0).T.astype(BF16)
    group = lambda n: cols(pe_end + n * width, pe_end + (n + 1) * width).T.astype(BF16)
    for dst, src in enumerate((0, 2, 4)):
        n_ref[:, dst * width:(dst + 1) * width] = group(src)
    for dst, src in enumerate((1, 3)):
        t_ref[:, dst * width:(dst + 1) * width] = group(src)


def _regroup_w_in(w_in, layer, lat_end, half_rope, width, rows):
    _, d, n_in = w_in.shape
    n_lat = lat_end + 8 * half_rope
    w_in_t = jnp.swapaxes(w_in, 1, 2)
    return pl.pallas_call(
        functools.partial(_regroup_body, lat_end=lat_end, half_rope=half_rope, width=width),
        grid=(d // rows,),
        in_specs=[pl.BlockSpec((1, n_in, rows), lambda i: (layer, 0, i))],
        out_specs=[
            pl.BlockSpec((rows, n_lat), lambda i: (i, 0)),
            pl.BlockSpec((rows, 3 * width), lambda i: (i, 0)),
            pl.BlockSpec((rows, 2 * width), lambda i: (i, 0)),
        ],
        out_shape=[
            jax.ShapeDtypeStruct((d, n_lat), BF16),
            jax.ShapeDtypeStruct((d, 3 * width), BF16),
            jax.ShapeDtypeStruct((d, 2 * width), BF16),
        ],
        compiler_params=pltpu.CompilerParams(
            dimension_semantics=("arbitrary",),
            vmem_limit_bytes=VMEM_LIMIT),
        name="regroup_w_in",
    )(w_in_t)


def _inproj_body(x_ref, g_ref, wn_ref, wt_ref, on_ref, oq_ref, ov_ref, *, tn, q_scale):
    h = _rms(x_ref[...], g_ref[...]).astype(BF16)
    for c in range(wn_ref.shape[1] // tn):
        cols = slice(c * tn, (c + 1) * tn)
        on_ref[:, cols] = _dot(h, wn_ref[:, cols]).astype(BF16)
    n_q = oq_ref.shape[0] // tn
    heads_per_chunk = tn // MLA_V
    for c in range(wt_ref.shape[1] // tn):
        cols = slice(c * tn, (c + 1) * tn)
        acc = _dot(h, wt_ref[:, cols])
        if c < n_q:
            oq_ref[cols, :] = (acc * q_scale).T.astype(BF16)
        else:
            vt = acc.T.astype(BF16)
            for j in range(heads_per_chunk):
                _store_vt_ext(ov_ref, (c - n_q) * heads_per_chunk + j, vt[j * MLA_V:(j + 1) * MLA_V, :])


def _inproj(x2, g_pre, w_n, w_t, q_scale, tm, tn):
    m, d = x2.shape
    width = w_t.shape[1] // 2
    nv = width // MLA_V * VT_ROWS
    resident = lambda a: pl.BlockSpec(a.shape, lambda i: (0,) * a.ndim, pipeline_mode=pl.Buffered(1))
    return pl.pallas_call(
        functools.partial(_inproj_body, tn=tn, q_scale=q_scale),
        grid=(m // tm,),
        in_specs=[
            pl.BlockSpec((tm, d), lambda i: (i, 0)),
            pl.BlockSpec((1, d), lambda i: (0, 0)),
            resident(w_n),
            resident(w_t),
        ],
        out_specs=[
            pl.BlockSpec((tm, w_n.shape[1]), lambda i: (i, 0)),
            pl.BlockSpec((width, tm), lambda i: (0, i)),
            pl.BlockSpec((nv, tm), lambda i: (0, i)),
        ],
        out_shape=[
            jax.ShapeDtypeStruct((m, w_n.shape[1]), BF16),
            jax.ShapeDtypeStruct((width, m), BF16),
            jax.ShapeDtypeStruct((nv, m), BF16),
        ],
        compiler_params=pltpu.CompilerParams(
            dimension_semantics=("arbitrary",),
            vmem_limit_bytes=VMEM_LIMIT),
        name="inproj",
    )(x2, g_pre, w_n, w_t)


def _latent_body(x_ref, pos_ref, freq_ref, gpre_ref, wlat_ref, gq_ref, wqt_ref,
                 gkv_ref, wk_ref, wvt_ref, qt_ref, k_ref, vt_ref, *, q_scale):
    h = _rms(x_ref[...], gpre_ref[...]).astype(BF16)
    lat = _dot(h, wlat_ref[...])
    ang = pos_ref[...].astype(F32) * freq_ref[...]
    cos = jnp.cos(ang)
    sin = jnp.sin(ang)
    seg = lax.broadcasted_iota(jnp.int32, (1, LANE), 1) // (MLA_ROPE // 2)
    fq = jnp.where(seg == 1, -sin, jnp.where(seg == 3, sin, cos)) * q_scale
    sk = jnp.where(seg < 2, -sin, sin)
    k_ext = (lat[:, 768:896] * cos + lat[:, 896:1024] * sk).astype(BF16)

    c_q = _rms(lat[:, :512], gq_ref[...]).astype(BF16)
    qft = _dot_nt(wqt_ref[...], c_q)
    fqt = fq.T
    c_kv = _rms(lat[:, 512:768], gkv_ref[...]).astype(BF16)
    kf = _dot(c_kv, wk_ref[...])
    vt = _dot_nt(wvt_ref[...], c_kv).astype(BF16)
    for hd in range(MLA_HEADS):
        _store_vt_ext(vt_ref, hd, vt[hd * MLA_V:(hd + 1) * MLA_V, :])
        o = hd * QK_PAD
        qt_ref[o:o + LANE, :] = (qft[o:o + LANE, :] * q_scale).astype(BF16)
        qt_ref[o + LANE:o + QK_PAD, :] = (qft[o + LANE:o + QK_PAD, :] * fqt).astype(BF16)
        k_ref[:, o:o + LANE] = kf[:, hd * LANE:(hd + 1) * LANE].astype(BF16)
        k_ref[:, o + LANE:o + QK_PAD] = k_ext


def _latent(x2, pos_col, freq, g_pre, w_lat, g_q, w_qt, g_kv, w_k, w_vt, tm, q_scale):
    m, d = x2.shape
    nq = MLA_HEADS * QK_PAD
    nv = MLA_HEADS * VT_ROWS
    full = lambda a: pl.BlockSpec(a.shape, lambda i: (0,) * a.ndim)
    return pl.pallas_call(
        functools.partial(_latent_body, q_scale=q_scale),
        grid=(m // tm,),
        in_specs=[
            pl.BlockSpec((tm, d), lambda i: (i, 0)),
            pl.BlockSpec((tm, 1), lambda i: (i, 0)),
            full(freq), full(g_pre), full(w_lat), full(g_q), full(w_qt), full(g_kv), full(w_k),
            full(w_vt),
        ],
        out_specs=[
            pl.BlockSpec((nq, tm), lambda i: (0, i)),
            pl.BlockSpec((tm, nq), lambda i: (i, 0)),
            pl.BlockSpec((nv, tm), lambda i: (0, i)),
        ],
        out_shape=[
            jax.ShapeDtypeStruct((nq, m), BF16),
            jax.ShapeDtypeStruct((m, nq), BF16),
            jax.ShapeDtypeStruct((nv, m), BF16),
        ],
        compiler_params=pltpu.CompilerParams(
            dimension_semantics=("arbitrary",),
            vmem_limit_bytes=VMEM_LIMIT),
        name="latent",
    )(x2, pos_col, freq, g_pre, w_lat, g_q, w_qt, g_kv, w_k, w_vt)


def _blk_slice(idx, blk):
    if isinstance(idx, int):
        return slice(idx * blk, (idx + 1) * blk)
    return pl.ds(pl.multiple_of(idx * blk, blk), blk)


def _offdiag_pairs(nblk):
    return [(i, t) for i in range(1, nblk) for t in range(i)]


def _pair_table(nblk):
    return jnp.asarray(np.array(_offdiag_pairs(nblk), dtype=np.int32).T)


def _half_slices(idx, blk):
    half = blk // 2
    if isinstance(idx, int):
        return slice(idx * blk, idx * blk + half), slice(idx * blk + half, (idx + 1) * blk)
    start = pl.multiple_of(idx * blk, blk)
    return pl.ds(start, half), pl.ds(pl.multiple_of(start + half, half), half)


def _init_mask_bias(mb_ref):
    kv = lax.broadcasted_iota(jnp.int32, mb_ref.shape, 0)
    qi = lax.broadcasted_iota(jnp.int32, mb_ref.shape, 1)
    mb_ref[...] = jnp.where(kv <= qi, 0.0, -jnp.inf).astype(F32)


def _diag_softmax(s_top, s_bot, mb):
    h = s_bot.shape[0]
    a0 = s_top[:, :h] + mb
    a1 = s_top[:, h:]
    b1 = s_bot + mb
    m0 = jnp.max(a0, axis=0, keepdims=True)
    m1 = jnp.maximum(jnp.max(a1, axis=0, keepdims=True), jnp.max(b1, axis=0, keepdims=True))
    p0 = jnp.exp2(a0 - m0)
    pa1 = jnp.exp2(a1 - m1)
    pb1 = jnp.exp2(b1 - m1)
    return jnp.concatenate([m0, m1], axis=1), jnp.concatenate([p0, pa1], axis=1), pb1


def _diag_pv(vt_top, vt_bot, p_top, p_bot):
    h = p_bot.shape[0]
    a = _dot(vt_top, p_top)
    b = _dot(vt_bot, p_bot)
    return jnp.concatenate([a[:, :h], a[:, h:] + b], axis=1)


def _run_pipeline(heads, tab_ref, nblk):
    A, A_DIAG, B_DIAG, C_DIAG, B, C = range(6)
    pairs = _offdiag_pairs(nblk)
    npairs = len(pairs)
    assert (nblk - 2) % 2 == 0 and (npairs - 2) % 2 == 0

    def step(c=None, b=None, a=None):
        for call in (c, b, a):
            if call is not None:
                for stages in heads:
                    stages[call[0]](*call[1])

    step(a=(A_DIAG, (0,)))
    step(b=(B_DIAG, (0,)), a=(A_DIAG, (1,)))

    def diag_loop(u, carry):
        for d in (1, 2):
            g = d + 2 * u
            step(c=(C_DIAG, (g - 1,)), b=(B_DIAG, (g,)), a=(A_DIAG, (g + 1,)))
        return carry

    lax.fori_loop(0, (nblk - 2) // 2, diag_loop, 0)

    step(c=(C_DIAG, (nblk - 2,)), b=(B_DIAG, (nblk - 1,)), a=(A, pairs[0]))
    step(c=(C_DIAG, (nblk - 1,)), b=(B, pairs[0]), a=(A, pairs[1]))

    def off_loop(u, carry):
        pair = lambda f: (tab_ref[0, f], tab_ref[1, f])
        for d in (1, 2):
            f = d + 2 * u
            step(c=(C, pair(f - 1)), b=(B, pair(f)), a=(A, pair(f + 1)))
        return carry

    lax.fori_loop(0, (npairs - 2) // 2, off_loop, 0)

    step(c=(C, pairs[npairs - 2]), b=(B, pairs[npairs - 1]))
    step(c=(C, pairs[npairs - 1]))


def _mla_stages(qt_ref, k_ref, vt_ref, o_ref, mb_ref, s_ref, p_ref, al_ref, m_ref, acc_ref, blk, nblk):
    half = blk // 2

    def stage_a(i, t):
        s_ref[...] = _dot(k_ref[_blk_slice(t, blk), :], qt_ref[:, _blk_slice(i, blk)])

    def stage_a_diag(i):
        top, bot = _half_slices(i, blk)
        s_ref[:half, :] = _dot(k_ref[top, :], qt_ref[:, _blk_slice(i, blk)])
        s_ref[half:, half:] = _dot(k_ref[bot, :], qt_ref[:, bot])

    def stage_b_diag(i):
        m, p_top, p_bot = _diag_softmax(s_ref[:half, :], s_ref[half:, half:], mb_ref[...])
        m_ref[i] = m
        p_ref[:half, :] = p_top.astype(BF16)
        p_ref[half:, half:] = p_bot.astype(BF16)

    def stage_c_diag(i):
        top, bot = _half_slices(i, blk)
        acc_ref[i] = _diag_pv(vt_ref[:, top], vt_ref[:, bot], p_ref[:half, :], p_ref[half:, half:])

    def stage_b(i, t):
        s = s_ref[...]
        m_prev = m_ref[i]
        m_new = jnp.maximum(m_prev, jnp.max(s, axis=0, keepdims=True))
        alpha = jnp.exp2(m_prev - m_new)
        p = jnp.exp2(s - m_new)
        m_ref[i] = m_new
        al_ref[...] = alpha
        p_ref[...] = p.astype(BF16)

    def stage_c(i, t):
        acc_ref[i] = al_ref[...] * acc_ref[i] + _dot(vt_ref[:, _blk_slice(t, blk)], p_ref[...])

    def finalize():
        for i in range(nblk):
            acc = acc_ref[i]
            o_t = acc[:MLA_V] * (1.0 / acc[MLA_V:MLA_V + 1])
            o_ref[_blk_slice(i, blk), :] = o_t.T.astype(BF16)

    return (stage_a, stage_a_diag, stage_b_diag, stage_c_diag, stage_b, stage_c), finalize


def _mla_scratch(blk, nblk):
    return [
        pltpu.VMEM((blk, blk), F32),
        pltpu.VMEM((blk, blk), BF16),
        pltpu.VMEM((1, blk), F32),
        pltpu.VMEM((nblk, 1, blk), F32),
        pltpu.VMEM((nblk, VT_ROWS, blk), F32),
    ]


def _alibi_lane_tables(slopes_l2e):
    c1 = slopes_l2e.astype(BF16).astype(F32)
    c2 = (slopes_l2e - c1).astype(BF16).astype(F32)
    c3 = (slopes_l2e - c1 - c2).astype(BF16).astype(F32)
    cw = (jnp.stack([c1, c2, c3], axis=1)[:, :, None]
          * jnp.asarray([1.0, 256.0, 65536.0], F32)[None, None, :]).reshape(-1, ALIBI_LANES)
    heads = slopes_l2e.shape[0]
    ktab = jnp.zeros((heads, 2, LANE), F32)
    for x, base in enumerate(ALIBI_BASE):
        ktab = ktab.at[:, x, base:base + ALIBI_LANES].set(-cw)
    qcol = jnp.zeros((heads, DIFF_HEAD_DIM), F32).at[:, ALIBI_LANES:2 * ALIBI_LANES].set(cw)
    return ktab.astype(BF16), qcol.reshape(heads, DIFF_HEAD_DIM, 1)


def _diff_stages(qt_ref, k_ref, vt_ref, pc_ref, pr_ref, shr_ref, shc_ref, kt_ref, qc_ref,
                 lq1_ref, lk1_ref, lq2_ref, lk2_ref, gsub_ref, o_ref, mb_ref, dig_ref, digt_ref,
                 qm_ref, km_ref, s_ref, p_ref, al_ref, m_ref, acc_ref, blk, nblk, lam_init):
    half = DIFF_HEAD_DIM

    @pl.when(pl.program_id(1) == 0)
    def _():
        pcol = pc_ref[...]
        prel = jnp.broadcast_to(pcol - pcol[0:1, :], dig_ref.shape)
        shift = jnp.broadcast_to(shr_ref[...], dig_ref.shape)
        dig_ref[...] = (lax.shift_right_logical(prel, shift) & 255).astype(F32).astype(BF16)
        prow = pr_ref[0]
        prel_t = jnp.broadcast_to(prow - prow[:, 0:1], digt_ref.shape)
        shift_t = jnp.broadcast_to(shc_ref[...], digt_ref.shape)
        digt_ref[...] = (lax.shift_right_logical(prel_t, shift_t) & 255).astype(F32).astype(BF16)

    row = lax.broadcasted_iota(jnp.int32, (half, 1), 0)
    alibi_q = jnp.where(row < ALIBI_LANES, digt_ref[...],
                        jnp.broadcast_to(qc_ref[0], digt_ref.shape).astype(BF16))
    qm_ref[0, :half, :] = qt_ref[:half, :]
    qm_ref[0, half:, :] = alibi_q
    qm_ref[1, :half, :] = alibi_q
    qm_ref[1, half:, :] = qt_ref[half:, :]
    lane = lax.broadcasted_iota(jnp.int32, (1, LANE), 1)
    dig = dig_ref[...]
    k = k_ref[...]
    for x, base in enumerate(ALIBI_BASE):
        own = (lane < half) if x == 0 else (lane >= half)
        k_digit = (lane >= base + ALIBI_LANES) & (lane < base + 2 * ALIBI_LANES)
        km_ref[x] = jnp.where(own, k, jnp.where(k_digit, dig, kt_ref[0, x:x + 1, :]))

    def stage_a(i, t):
        for x in range(2):
            s_ref[x] = _dot(km_ref[x, _blk_slice(t, blk), :], qm_ref[x, :, _blk_slice(i, blk)])

    hb = blk // 2

    def stage_a_diag(i):
        top, bot = _half_slices(i, blk)
        for x in range(2):
            s_ref[x, :hb, :] = _dot(km_ref[x, top, :], qm_ref[x, :, _blk_slice(i, blk)])
            s_ref[x, hb:, hb:] = _dot(km_ref[x, bot, :], qm_ref[x, :, bot])

    def stage_b_diag(i):
        for x in range(2):
            m, p_top, p_bot = _diag_softmax(s_ref[x, :hb, :], s_ref[x, hb:, hb:], mb_ref[...])
            m_ref[x, i] = m
            p_ref[x, :hb, :] = p_top.astype(BF16)
            p_ref[x, hb:, hb:] = p_bot.astype(BF16)

    def stage_c_diag(i):
        top, bot = _half_slices(i, blk)
        for x in range(2):
            acc_ref[x, i] = _diag_pv(vt_ref[:, top], vt_ref[:, bot], p_ref[x, :hb, :], p_ref[x, hb:, hb:])

    def stage_b(i, t):
        for x in range(2):
            s = s_ref[x]
            m_prev = m_ref[x, i]
            m_new = jnp.maximum(m_prev, jnp.max(s, axis=0, keepdims=True))
            alpha = jnp.exp2(m_prev - m_new)
            p = jnp.exp2(s - m_new)
            m_ref[x, i] = m_new
            al_ref[x] = alpha
            p_ref[x] = p.astype(BF16)

    def stage_c(i, t):
        vt = vt_ref[:, _blk_slice(t, blk)]
        for x in range(2):
            acc_ref[x, i] = al_ref[x] * acc_ref[x, i] + _dot(vt, p_ref[x])

    def finalize():
        lam = (jnp.exp(jnp.sum(lq1_ref[...] * lk1_ref[...], axis=-1, keepdims=True))
               - jnp.exp(jnp.sum(lq2_ref[...] * lk2_ref[...], axis=-1, keepdims=True))
               + lam_init)
        for i in range(nblk):
            a1, a2 = acc_ref[0, i], acc_ref[1, i]
            o_t = (a1[:MLA_V] * (1.0 / a1[MLA_V:MLA_V + 1])
                   - lam * (a2[:MLA_V] * (1.0 / a2[MLA_V:MLA_V + 1])))
            o = o_t.T
            o_ref[_blk_slice(i, blk), :] = (_rms(o, gsub_ref[...]) * (1.0 - lam_init)).astype(BF16)

    return (stage_a, stage_a_diag, stage_b_diag, stage_c_diag, stage_b, stage_c), finalize


def _diff_scratch(seq, blk, nblk):
    return [
        pltpu.VMEM((seq, LANE), BF16),
        pltpu.VMEM((DIFF_HEAD_DIM, seq), BF16),
        pltpu.VMEM((2, LANE, seq), BF16),
        pltpu.VMEM((2, seq, LANE), BF16),
        pltpu.VMEM((2, blk, blk), F32),
        pltpu.VMEM((2, blk, blk), BF16),
        pltpu.VMEM((2, 1, blk), F32),
        pltpu.VMEM((2, nblk, 1, blk), F32),
        pltpu.VMEM((2, nblk, VT_ROWS, blk), F32),
    ]


N_MLA_IN, N_DIFF_IN = 3, 14
N_MLA_SCRATCH, N_DIFF_SCRATCH = 5, 9


def _attention_body(tab_ref, *refs, blk, nblk, lam_init):
    mla_in, refs = refs[:N_MLA_IN], refs[N_MLA_IN:]
    diff_in, refs = refs[:N_DIFF_IN], refs[N_DIFF_IN:]
    (o_mla_ref, o_diff_ref, mb_ref), refs = refs[:3], refs[3:]
    mla_scr, diff_scr = refs[:N_MLA_SCRATCH], refs[N_MLA_SCRATCH:]
    assert len(diff_scr) == N_DIFF_SCRATCH

    @pl.when((pl.program_id(0) == 0) & (pl.program_id(1) == 0))
    def _():
        _init_mask_bias(mb_ref)

    mla, mla_fin = _mla_stages(*mla_in, o_mla_ref, mb_ref, *mla_scr, blk, nblk)
    diff, diff_fin = _diff_stages(*diff_in, o_diff_ref, mb_ref, *diff_scr, blk, nblk, lam_init)

    _run_pipeline([mla, diff], tab_ref, nblk)
    mla_fin()
    diff_fin()


def _attention(qt, k, vt, big_n, dqt, dvt, pos_icol, pos_irow, slopes_l2e, lq1, lk1, lq2, lk2, g_sub,
               batch, seq, blk, lam_init):
    assert MLA_HEADS == DIFF_HEADS
    nblk = seq // blk
    assert nblk >= 3
    width = DIFF_HEADS * 2 * DIFF_HEAD_DIM
    cb = width // LANE
    k_off = 1 * cb
    k_tab, q_col = _alibi_lane_tables(slopes_l2e)
    digit_k = np.arange(LANE) % DIFF_HEAD_DIM % 3
    shift_row = jnp.asarray((digit_k * 8).reshape(1, LANE), jnp.int32)
    shift_col = jnp.asarray((digit_k[:DIFF_HEAD_DIM] * 8).reshape(DIFF_HEAD_DIM, 1), jnp.int32)
    small = lambda a: pl.BlockSpec(a.shape, lambda b, h: (0,) * a.ndim)
    mla_specs = [
        pl.BlockSpec((QK_PAD, seq), lambda b, h: (h, b)),
        pl.BlockSpec((seq, QK_PAD), lambda b, h: (b, h)),
        pl.BlockSpec((VT_ROWS, seq), lambda b, h: (h, b)),
    ]
    diff_specs = [
        pl.BlockSpec((LANE, seq), lambda b, h: (h, b)),
        pl.BlockSpec((seq, LANE), lambda b, h: (b, k_off + h)),
        pl.BlockSpec((VT_ROWS, seq), lambda b, h: (h, b)),
        pl.BlockSpec((seq, 1), lambda b, h: (b, 0)),
        pl.BlockSpec((1, 1, seq), lambda b, h: (b, 0, 0)),
        small(shift_row), small(shift_col),
        pl.BlockSpec((1, 2, LANE), lambda b, h: (h, 0, 0)),
        pl.BlockSpec((1, DIFF_HEAD_DIM, 1), lambda b, h: (h, 0, 0)),
        small(lq1), small(lk1), small(lq2), small(lk2), small(g_sub),
    ]
    assert len(mla_specs) == N_MLA_IN and len(diff_specs) == N_DIFF_IN
    out_spec = pl.BlockSpec((seq, LANE), lambda b, h: (b, h))
    return pl.pallas_call(
        functools.partial(_attention_body, blk=blk, nblk=nblk, lam_init=lam_init),
        grid=(batch, MLA_HEADS),
        in_specs=[pl.BlockSpec(memory_space=pltpu.SMEM)] + mla_specs + diff_specs,
        out_specs=[out_spec, out_spec],
        out_shape=[jax.ShapeDtypeStruct((batch * seq, MLA_HEADS * MLA_V), BF16),
                   jax.ShapeDtypeStruct((batch * seq, width), BF16)],
        scratch_shapes=([pltpu.VMEM((blk // 2, blk // 2), F32)]
                        + _mla_scratch(blk, nblk) + _diff_scratch(seq, blk, nblk)),
        compiler_params=pltpu.CompilerParams(
            dimension_semantics=("arbitrary", "arbitrary"),
            vmem_limit_bytes=VMEM_LIMIT),
        name="attention",
    )(_pair_table(nblk), qt, k, vt, dqt, big_n, dvt, pos_icol, pos_irow, shift_row, shift_col,
      k_tab, q_col, lq1, lk1, lq2, lk2, g_sub)


def _outproj_body(x_ref, om_ref, od_ref, gm_ref, gd_ref, w_ref, gpost_ref, o_ref):
    gm = gm_ref[...].astype(F32)
    gd = gd_ref[...].astype(F32)
    mm = (om_ref[...].astype(F32) * (gm * jax.nn.sigmoid(gm))).astype(BF16)
    md = (od_ref[...].astype(F32) * (gd * jax.nn.sigmoid(gd))).astype(BF16)
    half = mm.shape[1]
    y = _dot(mm, w_ref[:half, :]) + _dot(md, w_ref[half:, :])
    o_ref[...] = x_ref[...] + _rms(y, gpost_ref[...])


def _outproj(x2, o_mla, o_diff, big, w_out, g_post, tm):
    m, d = x2.shape
    half = o_mla.shape[1]
    gate_diff_blk = (big.shape[1] - half) // half
    return pl.pallas_call(
        _outproj_body,
        grid=(m // tm,),
        in_specs=[
            pl.BlockSpec((tm, d), lambda i: (i, 0)),
            pl.BlockSpec((tm, half), lambda i: (i, 0)),
            pl.BlockSpec((tm, half), lambda i: (i, 0)),
            pl.BlockSpec((tm, half), lambda i: (i, 0)),
            pl.BlockSpec((tm, half), lambda i: (i, gate_diff_blk)),
            pl.BlockSpec(w_out.shape, lambda i: (0, 0)),
            pl.BlockSpec((1, d), lambda i: (0, 0)),
        ],
        out_specs=pl.BlockSpec((tm, d), lambda i: (i, 0)),
        out_shape=jax.ShapeDtypeStruct((m, d), F32),
        compiler_params=pltpu.CompilerParams(
            dimension_semantics=("arbitrary",),
            vmem_limit_bytes=VMEM_LIMIT),
        name="outproj",
    )(x2, o_mla, o_diff, big, big, w_out, g_post)


def kernel(x, positions, g_pre, w_in, g_q_a, w_q_b, g_kv_a, w_kv_b, lambda_q1, lambda_k1,
           lambda_q2, lambda_k2, g_diff_sub, w_out, g_post):
    batch, seq, d = x.shape
    depth = g_pre.shape[0]
    q_rank = w_q_b.shape[1]
    kv_rank = w_kv_b.shape[1]
    half_rope = MLA_ROPE // 2
    lat_end = q_rank + kv_rank
    pe_end = lat_end + MLA_ROPE

    pos_icol = positions.astype(jnp.int32).reshape(batch * seq, 1)
    pos_irow = positions.astype(jnp.int32).reshape(batch, 1, seq)
    freqs = 1.0 / (ROPE_THETA ** (jnp.arange(0, MLA_ROPE, 2, dtype=F32) / MLA_ROPE))
    freq = jnp.tile(freqs, 4).reshape(1, LANE)
    slopes = 2.0 ** (-8.0 * (jnp.arange(DIFF_HEADS, dtype=F32) + 1.0) / DIFF_HEADS)
    slopes_l2e = slopes * LOG2E

    mla_scale = MLA_QK ** -0.5 * LOG2E
    diff_scale = DIFF_HEAD_DIM ** -0.5 * LOG2E
    width = DIFF_HEADS * 2 * DIFF_HEAD_DIM

    x2 = x.reshape(batch * seq, d)
    for l in range(depth):
        lam_init = 0.8 - 0.6 * math.exp(-0.3 * l)
        w_lat, w_n, w_t = _regroup_w_in(w_in, l, lat_end, half_rope, width, rows=256)
        wq = w_q_b[l]
        t1 = wq[:, :, MLA_NOPE:MLA_NOPE + half_rope]
        t2 = wq[:, :, MLA_NOPE + half_rope:]
        w_q = jnp.concatenate([wq[:, :, :MLA_NOPE], t1, t2, t2, t1], axis=-1)
        w_qt = w_q.reshape(q_rank, MLA_HEADS * QK_PAD).T.astype(BF16)
        wkv = w_kv_b[l]
        w_k = wkv[:, :, :MLA_NOPE].reshape(kv_rank, -1).astype(BF16)
        w_vt = wkv[:, :, MLA_NOPE:].reshape(kv_rank, -1).T.astype(BF16)
        gp = g_pre[l].reshape(1, d)

        big_n, dqt, dvt = _inproj(x2, gp, w_n, w_t, diff_scale, tm=512, tn=512)
        qt, k, vt = _latent(x2, pos_icol, freq, gp, w_lat, g_q_a[l].reshape(1, -1), w_qt,
                            g_kv_a[l].reshape(1, -1), w_k, w_vt, tm=512, q_scale=mla_scale)
        o_mla, o_diff = _attention(
            qt, k, vt, big_n, dqt, dvt, pos_icol, pos_irow, slopes_l2e,
            lambda_q1[l].reshape(1, -1), lambda_k1[l].reshape(1, -1),
            lambda_q2[l].reshape(1, -1), lambda_k2[l].reshape(1, -1),
            g_diff_sub[l].reshape(1, -1), batch, seq, blk=512, lam_init=lam_init)
        x2 = _outproj(x2, o_mla, o_diff, big_n, w_out[l].astype(BF16), g_post[l].reshape(1, d), tm=512)
    return x2.reshape(batch, seq, d)
```

```python
import functools
import math

import numpy as np
import jax
import jax.numpy as jnp
from jax import lax
from jax.experimental import pallas as pl
from jax.experimental.pallas import tpu as pltpu

F32 = jnp.float32
BF16 = jnp.bfloat16

EPS = 1e-6
LOG2E = 1.4426950408889634
ROPE_THETA = 10000.0

MLA_HEADS = 8
MLA_NOPE = 128
MLA_ROPE = 64
MLA_V = 128
VT_ROWS = MLA_V + 16
MLA_QK = MLA_NOPE + MLA_ROPE
DIFF_HEADS = 8
DIFF_HEAD_DIM = 64
LANE = 128
QK_PAD = 256
OFF_STEPS = 4
ALIBI_LANES = 9
ALIBI_BASE = (DIFF_HEAD_DIM, 0)

VMEM_LIMIT = 56 * 1024 * 1024


def _rms(xf, g):
    ms = jnp.mean(xf * xf, axis=-1, keepdims=True)
    return xf * lax.rsqrt(ms + EPS) * g


def _dot(a, b):
    return jnp.dot(a, b, preferred_element_type=F32)


def _dot_nt(a, b):
    return lax.dot_general(a, b, (((1,), (1,)), ((), ())), preferred_element_type=F32)


def _store_vt_ext(vt_ref, head, vt_head):
    r = head * VT_ROWS
    vt_ref[r:r + MLA_V, :] = vt_head
    vt_ref[r + MLA_V:r + VT_ROWS, :] = jnp.ones((VT_ROWS - MLA_V, vt_head.shape[1]), vt_head.dtype)


def _regroup_body(wt_ref, lat_ref, n_ref, t_ref, *, lat_end, half_rope, width):
    pe_end = lat_end + 2 * half_rope
    cols = lambda a, b: wt_ref[0, a:b, :]
    u1 = cols(lat_end, lat_end + half_rope)
    u2 = cols(lat_end + half_rope, pe_end)
    lat_ref[:, :lat_end] = cols(0, lat_end).T.astype(BF16)
    lat_ref[:, lat_end:] = jnp.concatenate([u1, u1, u2, u2, u2, u2, u1, u1], axis=0).T.astype(BF16)
    group = lambda n: cols(pe_end + n * width, pe_end + (n + 1) * width).T.astype(BF16)
    for dst, src in enumerate((0, 2, 4)):
        n_ref[:, dst * width:(dst + 1) * width] = group(src)
    for dst, src in enumerate((1, 3)):
        t_ref[:, dst * width:(dst + 1) * width] = group(src)


def _regroup_w_in(w_in, layer, lat_end, half_rope, width, rows):
    _, d, n_in = w_in.shape
    n_lat = lat_end + 8 * half_rope
    w_in_t = jnp.swapaxes(w_in, 1, 2)
    return pl.pallas_call(
        functools.partial(_regroup_body, lat_end=lat_end, half_rope=half_rope, width=width),
        grid=(d // rows,),
        in_specs=[pl.BlockSpec((1, n_in, rows), lambda i: (layer, 0, i))],
        out_specs=[
            pl.BlockSpec((rows, n_lat), lambda i: (i, 0)),
            pl.BlockSpec((rows, 3 * width), lambda i: (i, 0)),
            pl.BlockSpec((rows, 2 * width), lambda i: (i, 0)),
        ],
        out_shape=[
            jax.ShapeDtypeStruct((d, n_lat), BF16),
            jax.ShapeDtypeStruct((d, 3 * width), BF16),
            jax.ShapeDtypeStruct((d, 2 * width), BF16),
        ],
        compiler_params=pltpu.CompilerParams(
            dimension_semantics=("arbitrary",),
            vmem_limit_bytes=VMEM_LIMIT),
        name="regroup_w_in",
    )(w_in_t)


def _inproj_body(x_ref, g_ref, wn_ref, wt_ref, on_ref, oq_ref, ov_ref, *, tn, q_scale):
    h = _rms(x_ref[...], g_ref[...]).astype(BF16)
    for c in range(wn_ref.shape[1] // tn):
        cols = slice(c * tn, (c + 1) * tn)
        on_ref[:, cols] = _dot(h, wn_ref[:, cols]).astype(BF16)
    n_q = oq_ref.shape[0] // tn
    heads_per_chunk = tn // MLA_V
    for c in range(wt_ref.shape[1] // tn):
        cols = slice(c * tn, (c + 1) * tn)
        acc = _dot(h, wt_ref[:, cols])
        if c < n_q:
            oq_ref[cols, :] = (acc * q_scale).T.astype(BF16)
        else:
            vt = acc.T.astype(BF16)
            for j in range(heads_per_chunk):
                _store_vt_ext(ov_ref, (c - n_q) * heads_per_chunk + j, vt[j * MLA_V:(j + 1) * MLA_V, :])


def _inproj(x2, g_pre, w_n, w_t, q_scale, tm, tn):
    m, d = x2.shape
    width = w_t.shape[1] // 2
    nv = width // MLA_V * VT_ROWS
    resident = lambda a: pl.BlockSpec(a.shape, lambda i: (0,) * a.ndim, pipeline_mode=pl.Buffered(1))
    return pl.pallas_call(
        functools.partial(_inproj_body, tn=tn, q_scale=q_scale),
        grid=(m // tm,),
        in_specs=[
            pl.BlockSpec((tm, d), lambda i: (i, 0)),
            pl.BlockSpec((1, d), lambda i: (0, 0)),
            resident(w_n),
            resident(w_t),
        ],
        out_specs=[
            pl.BlockSpec((tm, w_n.shape[1]), lambda i: (i, 0)),
            pl.BlockSpec((width, tm), lambda i: (0, i)),
            pl.BlockSpec((nv, tm), lambda i: (0, i)),
        ],
        out_shape=[
            jax.ShapeDtypeStruct((m, w_n.shape[1]), BF16),
            jax.ShapeDtypeStruct((width, m), BF16),
            jax.ShapeDtypeStruct((nv, m), BF16),
        ],
        compiler_params=pltpu.CompilerParams(
            dimension_semantics=("arbitrary",),
            vmem_limit_bytes=VMEM_LIMIT),
        name="inproj",
    )(x2, g_pre, w_n, w_t)


def _latent_body(x_ref, pos_ref, freq_ref, gpre_ref, wlat_ref, gq_ref, wqt_ref,
                 gkv_ref, wk_ref, wvt_ref, qt_ref, k_ref, vt_ref, *, q_scale):
    h = _rms(x_ref[...], gpre_ref[...]).astype(BF16)
    lat = _dot(h, wlat_ref[...])
    ang = pos_ref[...].astype(F32) * freq_ref[...]
    cos = jnp.cos(ang)
    sin = jnp.sin(ang)
    seg = lax.broadcasted_iota(jnp.int32, (1, LANE), 1) // (MLA_ROPE // 2)
    fq = jnp.where(seg == 1, -sin, jnp.where(seg == 3, sin, cos)) * q_scale
    sk = jnp.where(seg < 2, -sin, sin)
    k_ext = (lat[:, 768:896] * cos + lat[:, 896:1024] * sk).astype(BF16)

    c_q = _rms(lat[:, :512], gq_ref[...]).astype(BF16)
    qft = _dot_nt(wqt_ref[...], c_q)
    fqt = fq.T
    c_kv = _rms(lat[:, 512:768], gkv_ref[...]).astype(BF16)
    kf = _dot(c_kv, wk_ref[...])
    vt = _dot_nt(wvt_ref[...], c_kv).astype(BF16)
    for hd in range(MLA_HEADS):
        _store_vt_ext(vt_ref, hd, vt[hd * MLA_V:(hd + 1) * MLA_V, :])
        o = hd * QK_PAD
        qt_ref[o:o + LANE, :] = (qft[o:o + LANE, :] * q_scale).astype(BF16)
        qt_ref[o + LANE:o + QK_PAD, :] = (qft[o + LANE:o + QK_PAD, :] * fqt).astype(BF16)
        k_ref[:, o:o + LANE] = kf[:, hd * LANE:(hd + 1) * LANE].astype(BF16)
        k_ref[:, o + LANE:o + QK_PAD] = k_ext


def _latent(x2, pos_col, freq, g_pre, w_lat, g_q, w_qt, g_kv, w_k, w_vt, tm, q_scale):
    m, d = x2.shape
    nq = MLA_HEADS * QK_PAD
    nv = MLA_HEADS * VT_ROWS
    full = lambda a: pl.BlockSpec(a.shape, lambda i: (0,) * a.ndim)
    return pl.pallas_call(
        functools.partial(_latent_body, q_scale=q_scale),
        grid=(m // tm,),
        in_specs=[
            pl.BlockSpec((tm, d), lambda i: (i, 0)),
            pl.BlockSpec((tm, 1), lambda i: (i, 0)),
            full(freq), full(g_pre), full(w_lat), full(g_q), full(w_qt), full(g_kv), full(w_k),
            full(w_vt),
        ],
        out_specs=[
            pl.BlockSpec((nq, tm), lambda i: (0, i)),
            pl.BlockSpec((tm, nq), lambda i: (i, 0)),
            pl.BlockSpec((nv, tm), lambda i: (0, i)),
        ],
        out_shape=[
            jax.ShapeDtypeStruct((nq, m), BF16),
            jax.ShapeDtypeStruct((m, nq), BF16),
            jax.ShapeDtypeStruct((nv, m), BF16),
        ],
        compiler_params=pltpu.CompilerParams(
            dimension_semantics=("arbitrary",),
            vmem_limit_bytes=VMEM_LIMIT),
        name="latent",
    )(x2, pos_col, freq, g_pre, w_lat, g_q, w_qt, g_kv, w_k, w_vt)


def _blk_slice(idx, blk):
    if isinstance(idx, int):
        return slice(idx * blk, (idx + 1) * blk)
    return pl.ds(pl.multiple_of(idx * blk, blk), blk)


def _offdiag_pairs(nblk):
    return [(i, t) for i in range(1, nblk) for t in range(i)]


def _pair_table(nblk):
    return jnp.asarray(np.array(_offdiag_pairs(nblk), dtype=np.int32).T)


def _half_slices(idx, blk):
    half = blk // 2
    if isinstance(idx, int):
        return slice(idx * blk, idx * blk + half), slice(idx * blk + half, (idx + 1) * blk)
    start = pl.multiple_of(idx * blk, blk)
    return pl.ds(start, half), pl.ds(pl.multiple_of(start + half, half), half)


def _init_mask_bias(mb_ref):
    kv = lax.broadcasted_iota(jnp.int32, mb_ref.shape, 0)
    qi = lax.broadcasted_iota(jnp.int32, mb_ref.shape, 1)
    mb_ref[...] = jnp.where(kv <= qi, 0.0, -jnp.inf).astype(F32)


def _diag_softmax(s_top, s_bot, mb):
    h = s_bot.shape[0]
    a0 = s_top[:, :h] + mb
    a1 = s_top[:, h:]
    b1 = s_bot + mb
    m0 = jnp.max(a0, axis=0, keepdims=True)
    m1 = jnp.maximum(jnp.max(a1, axis=0, keepdims=True), jnp.max(b1, axis=0, keepdims=True))
    p0 = jnp.exp2(a0 - m0)
    pa1 = jnp.exp2(a1 - m1)
    pb1 = jnp.exp2(b1 - m1)
    return jnp.concatenate([m0, m1], axis=1), jnp.concatenate([p0, pa1], axis=1), pb1


def _diag_pv(vt_top, vt_bot, p_top, p_bot):
    h = p_bot.shape[0]
    a = _dot(vt_top, p_top)
    b = _dot(vt_bot, p_bot)
    return jnp.concatenate([a[:, :h], a[:, h:] + b], axis=1)


def _run_pipeline(heads, tab_ref, nblk):
    A, A_DIAG, B_DIAG, C_DIAG, B, C = range(6)
    pairs = _offdiag_pairs(nblk)
    npairs = len(pairs)
    assert (nblk - 2) % 2 == 0

    def step(c=None, b=None, a=None):
        for call in (c, b, a):
            if call is not None:
                for stages in heads:
                    stages[call[0]](*call[1])

    step(a=(A_DIAG, (0,)))
    step(b=(B_DIAG, (0,)), a=(A_DIAG, (1,)))

    def diag_loop(u, carry):
        for d in (1, 2):
            g = d + 2 * u
            step(c=(C_DIAG, (g - 1,)), b=(B_DIAG, (g,)), a=(A_DIAG, (g + 1,)))
        return carry

    lax.fori_loop(0, (nblk - 2) // 2, diag_loop, 0)

    step(c=(C_DIAG, (nblk - 2,)), b=(B_DIAG, (nblk - 1,)), a=(A, pairs[0]))
    step(c=(C_DIAG, (nblk - 1,)), b=(B, pairs[0]), a=(A, pairs[1]))

    def off_loop(u, carry):
        pair = lambda f: (tab_ref[0, f], tab_ref[1, f])
        for d in range(1, OFF_STEPS + 1):
            f = d + OFF_STEPS * u
            step(c=(C, pair(f - 1)), b=(B, pair(f)), a=(A, pair(f + 1)))
        return carry

    looped = (npairs - 2) // OFF_STEPS * OFF_STEPS
    lax.fori_loop(0, looped // OFF_STEPS, off_loop, 0)
    for f in range(1 + looped, npairs - 1):
        step(c=(C, pairs[f - 1]), b=(B, pairs[f]), a=(A, pairs[f + 1]))

    step(c=(C, pairs[npairs - 2]), b=(B, pairs[npairs - 1]))
    step(c=(C, pairs[npairs - 1]))


def _mla_stages(qt_ref, k_ref, vt_ref, o_ref, mb_ref, s_ref, p_ref, al_ref, m_ref, acc_ref, blk, nblk):
    half = blk // 2

    def stage_a(i, t):
        s_ref[...] = _dot(k_ref[_blk_slice(t, blk), :], qt_ref[:, _blk_slice(i, blk)])

    def stage_a_diag(i):
        top, bot = _half_slices(i, blk)
        s_ref[:half, :] = _dot(k_ref[top, :], qt_ref[:, _blk_slice(i, blk)])
        s_ref[half:, half:] = _dot(k_ref[bot, :], qt_ref[:, bot])

    def stage_b_diag(i):
        m, p_top, p_bot = _diag_softmax(s_ref[:half, :], s_ref[half:, half:], mb_ref[...])
        m_ref[i] = m
        p_ref[:half, :] = p_top.astype(BF16)
        p_ref[half:, half:] = p_bot.astype(BF16)

    def stage_c_diag(i):
        top, bot = _half_slices(i, blk)
        acc_ref[i] = _diag_pv(vt_ref[:, top], vt_ref[:, bot], p_ref[:half, :], p_ref[half:, half:])

    def stage_b(i, t):
        s = s_ref[...]
        m_prev = m_ref[i]
        m_new = jnp.maximum(m_prev, jnp.max(s, axis=0, keepdims=True))
        alpha = jnp.exp2(m_prev - m_new)
        p = jnp.exp2(s - m_new)
        m_ref[i] = m_new
        al_ref[...] = alpha
        p_ref[...] = p.astype(BF16)

    def stage_c(i, t):
        acc_ref[i] = al_ref[...] * acc_ref[i] + _dot(vt_ref[:, _blk_slice(t, blk)], p_ref[...])

    def finalize():
        for i in range(nblk):
            acc = acc_ref[i]
            o_t = acc[:MLA_V] * (1.0 / acc[MLA_V:MLA_V + 1])
            o_ref[_blk_slice(i, blk), :] = o_t.T.astype(BF16)

    return (stage_a, stage_a_diag, stage_b_diag, stage_c_diag, stage_b, stage_c), finalize


def _mla_scratch(blk, nblk):
    return [
        pltpu.VMEM((blk, blk), F32),
        pltpu.VMEM((blk, blk), BF16),
        pltpu.VMEM((1, blk), F32),
        pltpu.VMEM((nblk, 1, blk), F32),
        pltpu.VMEM((nblk, VT_ROWS, blk), F32),
    ]


def _alibi_lane_tables(slopes_l2e):
    c1 = slopes_l2e.astype(BF16).astype(F32)
    c2 = (slopes_l2e - c1).astype(BF16).astype(F32)
    c3 = (slopes_l2e - c1 - c2).astype(BF16).astype(F32)
    cw = (jnp.stack([c1, c2, c3], axis=1)[:, :, None]
          * jnp.asarray([1.0, 256.0, 65536.0], F32)[None, None, :]).reshape(-1, ALIBI_LANES)
    heads = slopes_l2e.shape[0]
    ktab = jnp.zeros((heads, 2, LANE), F32)
    for x, base in enumerate(ALIBI_BASE):
        ktab = ktab.at[:, x, base:base + ALIBI_LANES].set(-cw)
    qcol = jnp.zeros((heads, DIFF_HEAD_DIM), F32).at[:, ALIBI_LANES:2 * ALIBI_LANES].set(cw)
    return ktab.astype(BF16), qcol.reshape(heads, DIFF_HEAD_DIM, 1)


def _diff_stages(qt_ref, k_ref, vt_ref, pc_ref, pr_ref, shr_ref, shc_ref, kt_ref, qc_ref,
                 lq1_ref, lk1_ref, lq2_ref, lk2_ref, gsub_ref, o_ref, mb_ref, dig_ref, digt_ref,
                 qm_ref, km_ref, s_ref, p_ref, al_ref, m_ref, acc_ref, blk, nblk, lam_init):
    half = DIFF_HEAD_DIM

    @pl.when(pl.program_id(1) == 0)
    def _():
        pcol = pc_ref[...]
        prel = jnp.broadcast_to(pcol - pcol[0:1, :], dig_ref.shape)
        shift = jnp.broadcast_to(shr_ref[...], dig_ref.shape)
        dig_ref[...] = (lax.shift_right_logical(prel, shift) & 255).astype(F32).astype(BF16)
        prow = pr_ref[0]
        prel_t = jnp.broadcast_to(prow - prow[:, 0:1], digt_ref.shape)
        shift_t = jnp.broadcast_to(shc_ref[...], digt_ref.shape)
        digt_ref[...] = (lax.shift_right_logical(prel_t, shift_t) & 255).astype(F32).astype(BF16)

    row = lax.broadcasted_iota(jnp.int32, (half, 1), 0)
    alibi_q = jnp.where(row < ALIBI_LANES, digt_ref[...],
                        jnp.broadcast_to(qc_ref[0], digt_ref.shape).astype(BF16))
    qm_ref[0, :half, :] = qt_ref[:half, :]
    qm_ref[0, half:, :] = alibi_q
    qm_ref[1, :half, :] = alibi_q
    qm_ref[1, half:, :] = qt_ref[half:, :]
    lane = lax.broadcasted_iota(jnp.int32, (1, LANE), 1)
    dig = dig_ref[...]
    k = k_ref[...]
    for x, base in enumerate(ALIBI_BASE):
        own = (lane < half) if x == 0 else (lane >= half)
        k_digit = (lane >= base + ALIBI_LANES) & (lane < base + 2 * ALIBI_LANES)
        km_ref[x] = jnp.where(own, k, jnp.where(k_digit, dig, kt_ref[0, x:x + 1, :]))

    def stage_a(i, t):
        for x in range(2):
            s_ref[x] = _dot(km_ref[x, _blk_slice(t, blk), :], qm_ref[x, :, _blk_slice(i, blk)])

    hb = blk // 2

    def stage_a_diag(i):
        top, bot = _half_slices(i, blk)
        for x in range(2):
            s_ref[x, :hb, :] = _dot(km_ref[x, top, :], qm_ref[x, :, _blk_slice(i, blk)])
            s_ref[x, hb:, hb:] = _dot(km_ref[x, bot, :], qm_ref[x, :, bot])

    def stage_b_diag(i):
        for x in range(2):
            m, p_top, p_bot = _diag_softmax(s_ref[x, :hb, :], s_ref[x, hb:, hb:], mb_ref[...])
            m_ref[x, i] = m
            p_ref[x, :hb, :] = p_top.astype(BF16)
            p_ref[x, hb:, hb:] = p_bot.astype(BF16)

    def stage_c_diag(i):
        top, bot = _half_slices(i, blk)
        for x in range(2):
            acc_ref[x, i] = _diag_pv(vt_ref[:, top], vt_ref[:, bot], p_ref[x, :hb, :], p_ref[x, hb:, hb:])

    def stage_b(i, t):
        for x in range(2):
            s = s_ref[x]
            m_prev = m_ref[x, i]
            m_new = jnp.maximum(m_prev, jnp.max(s, axis=0, keepdims=True))
            alpha = jnp.exp2(m_prev - m_new)
            p = jnp.exp2(s - m_new)
            m_ref[x, i] = m_new
            al_ref[x] = alpha
            p_ref[x] = p.astype(BF16)

    def stage_c(i, t):
        vt = vt_ref[:, _blk_slice(t, blk)]
        for x in range(2):
            acc_ref[x, i] = al_ref[x] * acc_ref[x, i] + _dot(vt, p_ref[x])

    def finalize():
        lam = (jnp.exp(jnp.sum(lq1_ref[...] * lk1_ref[...], axis=-1, keepdims=True))
               - jnp.exp(jnp.sum(lq2_ref[...] * lk2_ref[...], axis=-1, keepdims=True))
               + lam_init)
        for i in range(nblk):
            a1, a2 = acc_ref[0, i], acc_ref[1, i]
            o_t = (a1[:MLA_V] * (1.0 / a1[MLA_V:MLA_V + 1])
                   - lam * (a2[:MLA_V] * (1.0 / a2[MLA_V:MLA_V + 1])))
            o = o_t.T
            o_ref[_blk_slice(i, blk), :] = (_rms(o, gsub_ref[...]) * (1.0 - lam_init)).astype(BF16)

    return (stage_a, stage_a_diag, stage_b_diag, stage_c_diag, stage_b, stage_c), finalize


def _diff_scratch(seq, blk, nblk):
    return [
        pltpu.VMEM((seq, LANE), BF16),
        pltpu.VMEM((DIFF_HEAD_DIM, seq), BF16),
        pltpu.VMEM((2, LANE, seq), BF16),
        pltpu.VMEM((2, seq, LANE), BF16),
        pltpu.VMEM((2, blk, blk), F32),
        pltpu.VMEM((2, blk, blk), BF16),
        pltpu.VMEM((2, 1, blk), F32),
        pltpu.VMEM((2, nblk, 1, blk), F32),
        pltpu.VMEM((2, nblk, VT_ROWS, blk), F32),
    ]


N_MLA_IN, N_DIFF_IN = 3, 14
N_MLA_SCRATCH, N_DIFF_SCRATCH = 5, 9


def _attention_body(tab_ref, *refs, blk, nblk, lam_init):
    mla_in, refs = refs[:N_MLA_IN], refs[N_MLA_IN:]
    diff_in, refs = refs[:N_DIFF_IN], refs[N_DIFF_IN:]
    (o_mla_ref, o_diff_ref, mb_ref), refs = refs[:3], refs[3:]
    mla_scr, diff_scr = refs[:N_MLA_SCRATCH], refs[N_MLA_SCRATCH:]
    assert len(diff_scr) == N_DIFF_SCRATCH

    @pl.when((pl.program_id(0) == 0) & (pl.program_id(1) == 0))
    def _():
        _init_mask_bias(mb_ref)

    mla, mla_fin = _mla_stages(*mla_in, o_mla_ref, mb_ref, *mla_scr, blk, nblk)
    diff, diff_fin = _diff_stages(*diff_in, o_diff_ref, mb_ref, *diff_scr, blk, nblk, lam_init)

    _run_pipeline([mla, diff], tab_ref, nblk)
    mla_fin()
    diff_fin()


def _attention(qt, k, vt, big_n, dqt, dvt, pos_icol, pos_irow, slopes_l2e, lq1, lk1, lq2, lk2, g_sub,
               batch, seq, blk, lam_init):
    assert MLA_HEADS == DIFF_HEADS
    nblk = seq // blk
    assert nblk >= 3
    width = DIFF_HEADS * 2 * DIFF_HEAD_DIM
    cb = width // LANE
    k_off = 1 * cb
    k_tab, q_col = _alibi_lane_tables(slopes_l2e)
    digit_k = np.arange(LANE) % DIFF_HEAD_DIM % 3
    shift_row = jnp.asarray((digit_k * 8).reshape(1, LANE), jnp.int32)
    shift_col = jnp.asarray((digit_k[:DIFF_HEAD_DIM] * 8).reshape(DIFF_HEAD_DIM, 1), jnp.int32)
    small = lambda a: pl.BlockSpec(a.shape, lambda b, h: (0,) * a.ndim)
    mla_specs = [
        pl.BlockSpec((QK_PAD, seq), lambda b, h: (h, b)),
        pl.BlockSpec((seq, QK_PAD), lambda b, h: (b, h)),
        pl.BlockSpec((VT_ROWS, seq), lambda b, h: (h, b)),
    ]
    diff_specs = [
        pl.BlockSpec((LANE, seq), lambda b, h: (h, b)),
        pl.BlockSpec((seq, LANE), lambda b, h: (b, k_off + h)),
        pl.BlockSpec((VT_ROWS, seq), lambda b, h: (h, b)),
        pl.BlockSpec((seq, 1), lambda b, h: (b, 0)),
        pl.BlockSpec((1, 1, seq), lambda b, h: (b, 0, 0)),
        small(shift_row), small(shift_col),
        pl.BlockSpec((1, 2, LANE), lambda b, h: (h, 0, 0)),
        pl.BlockSpec((1, DIFF_HEAD_DIM, 1), lambda b, h: (h, 0, 0)),
        small(lq1), small(lk1), small(lq2), small(lk2), small(g_sub),
    ]
    assert len(mla_specs) == N_MLA_IN and len(diff_specs) == N_DIFF_IN
    out_spec = pl.BlockSpec((seq, LANE), lambda b, h: (b, h))
    return pl.pallas_call(
        functools.partial(_attention_body, blk=blk, nblk=nblk, lam_init=lam_init),
        grid=(batch, MLA_HEADS),
        in_specs=[pl.BlockSpec(memory_space=pltpu.SMEM)] + mla_specs + diff_specs,
        out_specs=[out_spec, out_spec],
        out_shape=[jax.ShapeDtypeStruct((batch * seq, MLA_HEADS * MLA_V), BF16),
                   jax.ShapeDtypeStruct((batch * seq, width), BF16)],
        scratch_shapes=([pltpu.VMEM((blk // 2, blk // 2), F32)]
                        + _mla_scratch(blk, nblk) + _diff_scratch(seq, blk, nblk)),
        compiler_params=pltpu.CompilerParams(
            dimension_semantics=("arbitrary", "arbitrary"),
            vmem_limit_bytes=VMEM_LIMIT),
        name="attention",
    )(_pair_table(nblk), qt, k, vt, dqt, big_n, dvt, pos_icol, pos_irow, shift_row, shift_col,
      k_tab, q_col, lq1, lk1, lq2, lk2, g_sub)


def _outproj_body(x_ref, om_ref, od_ref, gm_ref, gd_ref, w_ref, gpost_ref, o_ref):
    gm = gm_ref[...].astype(F32)
    gd = gd_ref[...].astype(F32)
    mm = (om_ref[...].astype(F32) * (gm * jax.nn.sigmoid(gm))).astype(BF16)
    md = (od_ref[...].astype(F32) * (gd * jax.nn.sigmoid(gd))).astype(BF16)
    half = mm.shape[1]
    y = _dot(mm, w_ref[:half, :]) + _dot(md, w_ref[half:, :])
    o_ref[...] = x_ref[...] + _rms(y, gpost_ref[...])


def _outproj(x2, o_mla, o_diff, big, w_out, g_post, tm):
    m, d = x2.shape
    half = o_mla.shape[1]
    gate_diff_blk = (big.shape[1] - half) // half
    return pl.pallas_call(
        _outproj_body,
        grid=(m // tm,),
        in_specs=[
            pl.BlockSpec((tm, d), lambda i: (i, 0)),
            pl.BlockSpec((tm, half), lambda i: (i, 0)),
            pl.BlockSpec((tm, half), lambda i: (i, 0)),
            pl.BlockSpec((tm, half), lambda i: (i, 0)),
            pl.BlockSpec((tm, half), lambda i: (i, gate_diff_blk)),
            pl.BlockSpec(w_out.shape, lambda i: (0, 0)),
            pl.BlockSpec((1, d), lambda i: (0, 0)),
        ],
        out_specs=pl.BlockSpec((tm, d), lambda i: (i, 0)),
        out_shape=jax.ShapeDtypeStruct((m, d), F32),
        compiler_params=pltpu.CompilerParams(
            dimension_semantics=("arbitrary",),
            vmem_limit_bytes=VMEM_LIMIT),
        name="outproj",
    )(x2, o_mla, o_diff, big, big, w_out, g_post)


def kernel(x, positions, g_pre, w_in, g_q_a, w_q_b, g_kv_a, w_kv_b, lambda_q1, lambda_k1,
           lambda_q2, lambda_k2, g_diff_sub, w_out, g_post):
    batch, seq, d = x.shape
    depth = g_pre.shape[0]
    q_rank = w_q_b.shape[1]
    kv_rank = w_kv_b.shape[1]
    half_rope = MLA_ROPE // 2
    lat_end = q_rank + kv_rank
    pe_end = lat_end + MLA_ROPE

    pos_icol = positions.astype(jnp.int32).reshape(batch * seq, 1)
    pos_irow = positions.astype(jnp.int32).reshape(batch, 1, seq)
    freqs = 1.0 / (ROPE_THETA ** (jnp.arange(0, MLA_ROPE, 2, dtype=F32) / MLA_ROPE))
    freq = jnp.tile(freqs, 4).reshape(1, LANE)
    slopes = 2.0 ** (-8.0 * (jnp.arange(DIFF_HEADS, dtype=F32) + 1.0) / DIFF_HEADS)
    slopes_l2e = slopes * LOG2E

    mla_scale = MLA_QK ** -0.5 * LOG2E
    diff_scale = DIFF_HEAD_DIM ** -0.5 * LOG2E
    width = DIFF_HEADS * 2 * DIFF_HEAD_DIM

    x2 = x.reshape(batch * seq, d)
    for l in range(depth):
        lam_init = 0.8 - 0.6 * math.exp(-0.3 * l)
        w_lat, w_n, w_t = _regroup_w_in(w_in, l, lat_end, half_rope, width, rows=256)
        wq = w_q_b[l]
        t1 = wq[:, :, MLA_NOPE:MLA_NOPE + half_rope]
        t2 = wq[:, :, MLA_NOPE + half_rope:]
        w_q = jnp.concatenate([wq[:, :, :MLA_NOPE], t1, t2, t2, t1], axis=-1)
        w_qt = w_q.reshape(q_rank, MLA_HEADS * QK_PAD).T.astype(BF16)
        wkv = w_kv_b[l]
        w_k = wkv[:, :, :MLA_NOPE].reshape(kv_rank, -1).astype(BF16)
        w_vt = wkv[:, :, MLA_NOPE:].reshape(kv_rank, -1).T.astype(BF16)
        gp = g_pre[l].reshape(1, d)

        big_n, dqt, dvt = _inproj(x2, gp, w_n, w_t, diff_scale, tm=512, tn=512)
        qt, k, vt = _latent(x2, pos_icol, freq, gp, w_lat, g_q_a[l].reshape(1, -1), w_qt,
                            g_kv_a[l].reshape(1, -1), w_k, w_vt, tm=512, q_scale=mla_scale)
        o_mla, o_diff = _attention(
            qt, k, vt, big_n, dqt, dvt, pos_icol, pos_irow, slopes_l2e,
            lambda_q1[l].reshape(1, -1), lambda_k1[l].reshape(1, -1),
            lambda_q2[l].reshape(1, -1), lambda_k2[l].reshape(1, -1),
            g_diff_sub[l].reshape(1, -1), batch, seq, blk=512, lam_init=lam_init)
        x2 = _outproj(x2, o_mla, o_diff, big_n, w_out[l].astype(BF16), g_post[l].reshape(1, d), tm=512)
    return x2.reshape(batch, seq, d)
```

```python
import functools
import math

import numpy as np
import jax
import jax.numpy as jnp
from jax import lax
from jax.experimental import pallas as pl
from jax.experimental.pallas import tpu as pltpu

F32 = jnp.float32
BF16 = jnp.bfloat16

EPS = 1e-6
LOG2E = 1.4426950408889634
ROPE_THETA = 10000.0

MLA_HEADS = 8
MLA_NOPE = 128
MLA_ROPE = 64
MLA_V = 128
VT_ROWS = MLA_V + 16
MLA_QK = MLA_NOPE + MLA_ROPE
DIFF_HEADS = 8
DIFF_HEAD_DIM = 64
LANE = 128
QK_PAD = 256
ALIBI_LANES = 9
ALIBI_BASE = (DIFF_HEAD_DIM, 0)

VMEM_LIMIT = 56 * 1024 * 1024

PROJ_ROWS = 512
PROJ_COLS = 512
ATTN_BLOCK = 512
REGROUP_ROWS = 256


def _rms(xf, g):
    ms = jnp.mean(xf * xf, axis=-1, keepdims=True)
    return xf * lax.rsqrt(ms + EPS) * g


def _dot(a, b):
    return jnp.dot(a, b, preferred_element_type=F32)


def _dot_nt(a, b):
    return lax.dot_general(a, b, (((1,), (1,)), ((), ())), preferred_element_type=F32)


def _store_vt_ext(vt_ref, head, vt_head):
    r = head * VT_ROWS
    vt_ref[r:r + MLA_V, :] = vt_head
    vt_ref[r + MLA_V:r + VT_ROWS, :] = jnp.ones((VT_ROWS - MLA_V, vt_head.shape[1]), vt_head.dtype)


def _regroup_body(wt_ref, lat_ref, n_ref, t_ref, *, lat_end, half_rope, width):
    pe_end = lat_end + 2 * half_rope
    cols = lambda a, b: wt_ref[0, a:b, :]
    u1 = cols(lat_end, lat_end + half_rope)
    u2 = cols(lat_end + half_rope, pe_end)
    lat_ref[:, :lat_end] = cols(0, lat_end).T.astype(BF16)
    lat_ref[:, lat_end:] = jnp.concatenate([u1, u1, u2, u2, u2, u2, u1, u1], axis=0).T.astype(BF16)
    group = lambda n: cols(pe_end + n * width, pe_end + (n + 1) * width).T.astype(BF16)
    for dst, src in enumerate((0, 2, 4)):
        n_ref[:, dst * width:(dst + 1) * width] = group(src)
    for dst, src in enumerate((1, 3)):
        t_ref[:, dst * width:(dst + 1) * width] = group(src)


def _regroup_w_in(w_in, layer, lat_end, half_rope, width, rows):
    _, d, n_in = w_in.shape
    n_lat = lat_end + 8 * half_rope
    w_in_t = jnp.swapaxes(w_in, 1, 2)
    return pl.pallas_call(
        functools.partial(_regroup_body, lat_end=lat_end, half_rope=half_rope, width=width),
        grid=(d // rows,),
        in_specs=[pl.BlockSpec((1, n_in, rows), lambda i: (layer, 0, i))],
        out_specs=[
            pl.BlockSpec((rows, n_lat), lambda i: (i, 0)),
            pl.BlockSpec((rows, 3 * width), lambda i: (i, 0)),
            pl.BlockSpec((rows, 2 * width), lambda i: (i, 0)),
        ],
        out_shape=[
            jax.ShapeDtypeStruct((d, n_lat), BF16),
            jax.ShapeDtypeStruct((d, 3 * width), BF16),
            jax.ShapeDtypeStruct((d, 2 * width), BF16),
        ],
        compiler_params=pltpu.CompilerParams(
            dimension_semantics=("arbitrary",),
            vmem_limit_bytes=VMEM_LIMIT),
        name="regroup_w_in",
    )(w_in_t)


def _inproj_body(x_ref, g_ref, wn_ref, wt_ref, h_ref, on_ref, oq_ref, ov_ref, *, tn, q_scale):
    h = _rms(x_ref[...], g_ref[...]).astype(BF16)
    h_ref[...] = h
    for c in range(wn_ref.shape[1] // tn):
        cols = slice(c * tn, (c + 1) * tn)
        on_ref[:, cols] = _dot(h, wn_ref[:, cols]).astype(BF16)
    n_q = oq_ref.shape[0] // tn
    heads_per_chunk = tn // MLA_V
    for c in range(wt_ref.shape[1] // tn):
        cols = slice(c * tn, (c + 1) * tn)
        acc = _dot(h, wt_ref[:, cols])
        if c < n_q:
            oq_ref[cols, :] = (acc * q_scale).T.astype(BF16)
        else:
            vt = acc.T.astype(BF16)
            for j in range(heads_per_chunk):
                _store_vt_ext(ov_ref, (c - n_q) * heads_per_chunk + j, vt[j * MLA_V:(j + 1) * MLA_V, :])


def _inproj(x2, g_pre, w_n, w_t, q_scale, tm, tn):
    m, d = x2.shape
    width = w_t.shape[1] // 2
    nv = width // MLA_V * VT_ROWS
    resident = lambda a: pl.BlockSpec(a.shape, lambda i: (0,) * a.ndim, pipeline_mode=pl.Buffered(1))
    return pl.pallas_call(
        functools.partial(_inproj_body, tn=tn, q_scale=q_scale),
        grid=(m // tm,),
        in_specs=[
            pl.BlockSpec((tm, d), lambda i: (i, 0)),
            pl.BlockSpec((1, d), lambda i: (0, 0)),
            resident(w_n),
            resident(w_t),
        ],
        out_specs=[
            pl.BlockSpec((tm, d), lambda i: (i, 0)),
            pl.BlockSpec((tm, w_n.shape[1]), lambda i: (i, 0)),
            pl.BlockSpec((width, tm), lambda i: (0, i)),
            pl.BlockSpec((nv, tm), lambda i: (0, i)),
        ],
        out_shape=[
            jax.ShapeDtypeStruct((m, d), BF16),
            jax.ShapeDtypeStruct((m, w_n.shape[1]), BF16),
            jax.ShapeDtypeStruct((width, m), BF16),
            jax.ShapeDtypeStruct((nv, m), BF16),
        ],
        compiler_params=pltpu.CompilerParams(
            dimension_semantics=("arbitrary",),
            vmem_limit_bytes=VMEM_LIMIT),
        name="inproj",
    )(x2, g_pre, w_n, w_t)


def _latent_body(h_ref, pos_ref, freq_ref, wlat_ref, gq_ref, wqt_ref,
                 gkv_ref, wk_ref, wvt_ref, qt_ref, k_ref, vt_ref, *, q_scale):
    lat = _dot(h_ref[...], wlat_ref[...])
    ang = pos_ref[...].astype(F32) * freq_ref[...]
    cos = jnp.cos(ang)
    sin = jnp.sin(ang)
    seg = lax.broadcasted_iota(jnp.int32, (1, LANE), 1) // (MLA_ROPE // 2)
    fq = jnp.where(seg == 1, -sin, jnp.where(seg == 3, sin, cos)) * q_scale
    sk = jnp.where(seg < 2, -sin, sin)
    k_ext = (lat[:, 768:896] * cos + lat[:, 896:1024] * sk).astype(BF16)

    c_q = _rms(lat[:, :512], gq_ref[...]).astype(BF16)
    qft = _dot_nt(wqt_ref[...], c_q)
    fqt = fq.T
    c_kv = _rms(lat[:, 512:768], gkv_ref[...]).astype(BF16)
    kf = _dot(c_kv, wk_ref[...])
    vt = _dot_nt(wvt_ref[...], c_kv).astype(BF16)
    for hd in range(MLA_HEADS):
        _store_vt_ext(vt_ref, hd, vt[hd * MLA_V:(hd + 1) * MLA_V, :])
        o = hd * QK_PAD
        qt_ref[o:o + LANE, :] = (qft[o:o + LANE, :] * q_scale).astype(BF16)
        qt_ref[o + LANE:o + QK_PAD, :] = (qft[o + LANE:o + QK_PAD, :] * fqt).astype(BF16)
        k_ref[:, o:o + LANE] = kf[:, hd * LANE:(hd + 1) * LANE].astype(BF16)
        k_ref[:, o + LANE:o + QK_PAD] = k_ext


def _latent(h, pos_col, freq, w_lat, g_q, w_qt, g_kv, w_k, w_vt, tm, q_scale):
    m, d = h.shape
    nq = MLA_HEADS * QK_PAD
    nv = MLA_HEADS * VT_ROWS
    full = lambda a: pl.BlockSpec(a.shape, lambda i: (0,) * a.ndim)
    return pl.pallas_call(
        functools.partial(_latent_body, q_scale=q_scale),
        grid=(m // tm,),
        in_specs=[
            pl.BlockSpec((tm, d), lambda i: (i, 0)),
            pl.BlockSpec((tm, 1), lambda i: (i, 0)),
            full(freq), full(w_lat), full(g_q), full(w_qt), full(g_kv), full(w_k),
            full(w_vt),
        ],
        out_specs=[
            pl.BlockSpec((nq, tm), lambda i: (0, i)),
            pl.BlockSpec((tm, nq), lambda i: (i, 0)),
            pl.BlockSpec((nv, tm), lambda i: (0, i)),
        ],
        out_shape=[
            jax.ShapeDtypeStruct((nq, m), BF16),
            jax.ShapeDtypeStruct((m, nq), BF16),
            jax.ShapeDtypeStruct((nv, m), BF16),
        ],
        compiler_params=pltpu.CompilerParams(
            dimension_semantics=("arbitrary",),
            vmem_limit_bytes=VMEM_LIMIT),
        name="latent",
    )(h, pos_col, freq, w_lat, g_q, w_qt, g_kv, w_k, w_vt)


def _blk_slice(idx, blk):
    if isinstance(idx, int):
        return slice(idx * blk, (idx + 1) * blk)
    return pl.ds(pl.multiple_of(idx * blk, blk), blk)


def _offdiag_pairs(nblk):
    return [(i, t) for i in range(1, nblk) for t in range(i)]


def _pair_table(nblk):
    return jnp.asarray(np.array(_offdiag_pairs(nblk), dtype=np.int32).T)


def _half_slices(idx, blk):
    half = blk // 2
    if isinstance(idx, int):
        return slice(idx * blk, idx * blk + half), slice(idx * blk + half, (idx + 1) * blk)
    start = pl.multiple_of(idx * blk, blk)
    return pl.ds(start, half), pl.ds(pl.multiple_of(start + half, half), half)


def _init_mask_bias(mb_ref):
    kv = lax.broadcasted_iota(jnp.int32, mb_ref.shape, 0)
    qi = lax.broadcasted_iota(jnp.int32, mb_ref.shape, 1)
    mb_ref[...] = jnp.where(kv <= qi, 0.0, -jnp.inf).astype(F32)


def _diag_softmax(s_top, s_bot, mb):
    h = s_bot.shape[0]
    a0 = s_top[:, :h] + mb
    a1 = s_top[:, h:]
    b1 = s_bot + mb
    m0 = jnp.max(a0, axis=0, keepdims=True)
    m1 = jnp.maximum(jnp.max(a1, axis=0, keepdims=True), jnp.max(b1, axis=0, keepdims=True))
    p0 = jnp.exp2(a0 - m0)
    pa1 = jnp.exp2(a1 - m1)
    pb1 = jnp.exp2(b1 - m1)
    return jnp.concatenate([m0, m1], axis=1), jnp.concatenate([p0, pa1], axis=1), pb1


def _diag_pv(vt_top, vt_bot, p_top, p_bot):
    h = p_bot.shape[0]
    a = _dot(vt_top, p_top)
    b = _dot(vt_bot, p_bot)
    return jnp.concatenate([a[:, :h], a[:, h:] + b], axis=1)


def _run_pipeline(heads, tab_ref, nblk):
    A, A_DIAG, B_DIAG, C_DIAG, B, C = range(6)
    pairs = _offdiag_pairs(nblk)
    npairs = len(pairs)
    assert (nblk - 2) % 2 == 0 and (npairs - 2) % 2 == 0

    def step(c=None, b=None, a=None):
        for call in (c, b, a):
            if call is not None:
                for stages in heads:
                    stages[call[0]](*call[1])

    step(a=(A_DIAG, (0,)))
    step(b=(B_DIAG, (0,)), a=(A_DIAG, (1,)))

    def diag_loop(u, carry):
        for d in (1, 2):
            g = d + 2 * u
            step(c=(C_DIAG, (g - 1,)), b=(B_DIAG, (g,)), a=(A_DIAG, (g + 1,)))
        return carry

    lax.fori_loop(0, (nblk - 2) // 2, diag_loop, 0)

    step(c=(C_DIAG, (nblk - 2,)), b=(B_DIAG, (nblk - 1,)), a=(A, pairs[0]))
    step(c=(C_DIAG, (nblk - 1,)), b=(B, pairs[0]), a=(A, pairs[1]))

    def off_loop(u, carry):
        pair = lambda f: (tab_ref[0, f], tab_ref[1, f])
        for d in (1, 2):
            f = d + 2 * u
            step(c=(C, pair(f - 1)), b=(B, pair(f)), a=(A, pair(f + 1)))
        return carry

    lax.fori_loop(0, (npairs - 2) // 2, off_loop, 0)

    step(c=(C, pairs[npairs - 2]), b=(B, pairs[npairs - 1]))
    step(c=(C, pairs[npairs - 1]))


def _mla_stages(qt_ref, k_ref, vt_ref, o_ref, mb_ref, s_ref, p_ref, al_ref, m_ref, acc_ref, blk, nblk):
    half = blk // 2

    def stage_a(i, t):
        s_ref[...] = _dot(k_ref[_blk_slice(t, blk), :], qt_ref[:, _blk_slice(i, blk)])

    def stage_a_diag(i):
        top, bot = _half_slices(i, blk)
        s_ref[:half, :] = _dot(k_ref[top, :], qt_ref[:, _blk_slice(i, blk)])
        s_ref[half:, half:] = _dot(k_ref[bot, :], qt_ref[:, bot])

    def stage_b_diag(i):
        m, p_top, p_bot = _diag_softmax(s_ref[:half, :], s_ref[half:, half:], mb_ref[...])
        m_ref[i] = m
        p_ref[:half, :] = p_top.astype(BF16)
        p_ref[half:, half:] = p_bot.astype(BF16)

    def stage_c_diag(i):
        top, bot = _half_slices(i, blk)
        acc_ref[i] = _diag_pv(vt_ref[:, top], vt_ref[:, bot], p_ref[:half, :], p_ref[half:, half:])

    def stage_b(i, t):
        s = s_ref[...]
        m_prev = m_ref[i]
        m_new = jnp.maximum(m_prev, jnp.max(s, axis=0, keepdims=True))
        alpha = jnp.exp2(m_prev - m_new)
        p = jnp.exp2(s - m_new)
        m_ref[i] = m_new
        al_ref[...] = alpha
        p_ref[...] = p.astype(BF16)

    def stage_c(i, t):
        acc_ref[i] = al_ref[...] * acc_ref[i] + _dot(vt_ref[:, _blk_slice(t, blk)], p_ref[...])

    def finalize():
        for i in range(nblk):
            acc = acc_ref[i]
            o_t = acc[:MLA_V] * (1.0 / acc[MLA_V:MLA_V + 1])
            o_ref[_blk_slice(i, blk), :] = o_t.T.astype(BF16)

    return (stage_a, stage_a_diag, stage_b_diag, stage_c_diag, stage_b, stage_c), finalize


def _mla_scratch(blk, nblk):
    return [
        pltpu.VMEM((blk, blk), F32),
        pltpu.VMEM((blk, blk), BF16),
        pltpu.VMEM((1, blk), F32),
        pltpu.VMEM((nblk, 1, blk), F32),
        pltpu.VMEM((nblk, VT_ROWS, blk), F32),
    ]


def _alibi_lane_tables(slopes_l2e):
    c1 = slopes_l2e.astype(BF16).astype(F32)
    c2 = (slopes_l2e - c1).astype(BF16).astype(F32)
    c3 = (slopes_l2e - c1 - c2).astype(BF16).astype(F32)
    cw = (jnp.stack([c1, c2, c3], axis=1)[:, :, None]
          * jnp.asarray([1.0, 256.0, 65536.0], F32)[None, None, :]).reshape(-1, ALIBI_LANES)
    heads = slopes_l2e.shape[0]
    ktab = jnp.zeros((heads, 2, LANE), F32)
    for x, base in enumerate(ALIBI_BASE):
        ktab = ktab.at[:, x, base:base + ALIBI_LANES].set(-cw)
    qcol = jnp.zeros((heads, DIFF_HEAD_DIM), F32).at[:, ALIBI_LANES:2 * ALIBI_LANES].set(cw)
    return ktab.astype(BF16), qcol.reshape(heads, DIFF_HEAD_DIM, 1)


def _diff_stages(qt_ref, k_ref, vt_ref, pc_ref, pr_ref, shr_ref, shc_ref, kt_ref, qc_ref,
                 lq1_ref, lk1_ref, lq2_ref, lk2_ref, gsub_ref, o_ref, mb_ref, dig_ref, digt_ref,
                 qm_ref, km_ref, s_ref, p_ref, al_ref, m_ref, acc_ref, blk, nblk, lam_init):
    half = DIFF_HEAD_DIM

    @pl.when(pl.program_id(1) == 0)
    def _():
        pcol = pc_ref[...]
        prel = jnp.broadcast_to(pcol - pcol[0:1, :], dig_ref.shape)
        shift = jnp.broadcast_to(shr_ref[...], dig_ref.shape)
        dig_ref[...] = (lax.shift_right_logical(prel, shift) & 255).astype(F32).astype(BF16)
        prow = pr_ref[0]
        prel_t = jnp.broadcast_to(prow - prow[:, 0:1], digt_ref.shape)
        shift_t = jnp.broadcast_to(shc_ref[...], digt_ref.shape)
        digt_ref[...] = (lax.shift_right_logical(prel_t, shift_t) & 255).astype(F32).astype(BF16)

    row = lax.broadcasted_iota(jnp.int32, (half, 1), 0)
    alibi_q = jnp.where(row < ALIBI_LANES, digt_ref[...],
                        jnp.broadcast_to(qc_ref[0], digt_ref.shape).astype(BF16))
    qm_ref[0, :half, :] = qt_ref[:half, :]
    qm_ref[0, half:, :] = alibi_q
    qm_ref[1, :half, :] = alibi_q
    qm_ref[1, half:, :] = qt_ref[half:, :]
    lane = lax.broadcasted_iota(jnp.int32, (1, LANE), 1)
    dig = dig_ref[...]
    k = k_ref[...]
    for x, base in enumerate(ALIBI_BASE):
        own = (lane < half) if x == 0 else (lane >= half)
        k_digit = (lane >= base + ALIBI_LANES) & (lane < base + 2 * ALIBI_LANES)
        km_ref[x] = jnp.where(own, k, jnp.where(k_digit, dig, kt_ref[0, x:x + 1, :]))

    def stage_a(i, t):
        for x in range(2):
            s_ref[x] = _dot(km_ref[x, _blk_slice(t, blk), :], qm_ref[x, :, _blk_slice(i, blk)])

    hb = blk // 2

    def stage_a_diag(i):
        top, bot = _half_slices(i, blk)
        for x in range(2):
            s_ref[x, :hb, :] = _dot(km_ref[x, top, :], qm_ref[x, :, _blk_slice(i, blk)])
            s_ref[x, hb:, hb:] = _dot(km_ref[x, bot, :], qm_ref[x, :, bot])

    def stage_b_diag(i):
        for x in range(2):
            m, p_top, p_bot = _diag_softmax(s_ref[x, :hb, :], s_ref[x, hb:, hb:], mb_ref[...])
            m_ref[x, i] = m
            p_ref[x, :hb, :] = p_top.astype(BF16)
            p_ref[x, hb:, hb:] = p_bot.astype(BF16)

    def stage_c_diag(i):
        top, bot = _half_slices(i, blk)
        for x in range(2):
            acc_ref[x, i] = _diag_pv(vt_ref[:, top], vt_ref[:, bot], p_ref[x, :hb, :], p_ref[x, hb:, hb:])

    def stage_b(i, t):
        for x in range(2):
            s = s_ref[x]
            m_prev = m_ref[x, i]
            m_new = jnp.maximum(m_prev, jnp.max(s, axis=0, keepdims=True))
            alpha = jnp.exp2(m_prev - m_new)
            p = jnp.exp2(s - m_new)
            m_ref[x, i] = m_new
            al_ref[x] = alpha
            p_ref[x] = p.astype(BF16)

    def stage_c(i, t):
        vt = vt_ref[:, _blk_slice(t, blk)]
        for x in range(2):
            acc_ref[x, i] = al_ref[x] * acc_ref[x, i] + _dot(vt, p_ref[x])

    def finalize():
        lam = (jnp.exp(jnp.sum(lq1_ref[...] * lk1_ref[...], axis=-1, keepdims=True))
               - jnp.exp(jnp.sum(lq2_ref[...] * lk2_ref[...], axis=-1, keepdims=True))
               + lam_init)
        for i in range(nblk):
            a1, a2 = acc_ref[0, i], acc_ref[1, i]
            o_t = (a1[:MLA_V] * (1.0 / a1[MLA_V:MLA_V + 1])
                   - lam * (a2[:MLA_V] * (1.0 / a2[MLA_V:MLA_V + 1])))
            o = o_t.T
            o_ref[_blk_slice(i, blk), :] = (_rms(o, gsub_ref[...]) * (1.0 - lam_init)).astype(BF16)

    return (stage_a, stage_a_diag, stage_b_diag, stage_c_diag, stage_b, stage_c), finalize


def _diff_scratch(seq, blk, nblk):
    return [
        pltpu.VMEM((seq, LANE), BF16),
        pltpu.VMEM((DIFF_HEAD_DIM, seq), BF16),
        pltpu.VMEM((2, LANE, seq), BF16),
        pltpu.VMEM((2, seq, LANE), BF16),
        pltpu.VMEM((2, blk, blk), F32),
        pltpu.VMEM((2, blk, blk), BF16),
        pltpu.VMEM((2, 1, blk), F32),
        pltpu.VMEM((2, nblk, 1, blk), F32),
        pltpu.VMEM((2, nblk, VT_ROWS, blk), F32),
    ]


N_MLA_IN, N_DIFF_IN = 3, 14
N_MLA_SCRATCH, N_DIFF_SCRATCH = 5, 9


def _attention_body(tab_ref, *refs, blk, nblk, lam_init):
    mla_in, refs = refs[:N_MLA_IN], refs[N_MLA_IN:]
    diff_in, refs = refs[:N_DIFF_IN], refs[N_DIFF_IN:]
    (o_mla_ref, o_diff_ref, mb_ref), refs = refs[:3], refs[3:]
    mla_scr, diff_scr = refs[:N_MLA_SCRATCH], refs[N_MLA_SCRATCH:]
    assert len(diff_scr) == N_DIFF_SCRATCH

    @pl.when((pl.program_id(0) == 0) & (pl.program_id(1) == 0))
    def _():
        _init_mask_bias(mb_ref)

    mla, mla_fin = _mla_stages(*mla_in, o_mla_ref, mb_ref, *mla_scr, blk, nblk)
    diff, diff_fin = _diff_stages(*diff_in, o_diff_ref, mb_ref, *diff_scr, blk, nblk, lam_init)

    _run_pipeline([mla, diff], tab_ref, nblk)
    mla_fin()
    diff_fin()


def _attention(qt, k, vt, big_n, dqt, dvt, pos_icol, pos_irow, slopes_l2e, lq1, lk1, lq2, lk2, g_sub,
               batch, seq, blk, lam_init):
    assert MLA_HEADS == DIFF_HEADS
    nblk = seq // blk
    assert nblk >= 3
    width = DIFF_HEADS * 2 * DIFF_HEAD_DIM
    cb = width // LANE
    k_off = 1 * cb
    k_tab, q_col = _alibi_lane_tables(slopes_l2e)
    digit_k = np.arange(LANE) % DIFF_HEAD_DIM % 3
    shift_row = jnp.asarray((digit_k * 8).reshape(1, LANE), jnp.int32)
    shift_col = jnp.asarray((digit_k[:DIFF_HEAD_DIM] * 8).reshape(DIFF_HEAD_DIM, 1), jnp.int32)
    small = lambda a: pl.BlockSpec(a.shape, lambda b, h: (0,) * a.ndim)
    mla_specs = [
        pl.BlockSpec((QK_PAD, seq), lambda b, h: (h, b)),
        pl.BlockSpec((seq, QK_PAD), lambda b, h: (b, h)),
        pl.BlockSpec((VT_ROWS, seq), lambda b, h: (h, b)),
    ]
    diff_specs = [
        pl.BlockSpec((LANE, seq), lambda b, h: (h, b)),
        pl.BlockSpec((seq, LANE), lambda b, h: (b, k_off + h)),
        pl.BlockSpec((VT_ROWS, seq), lambda b, h: (h, b)),
        pl.BlockSpec((seq, 1), lambda b, h: (b, 0)),
        pl.BlockSpec((1, 1, seq), lambda b, h: (b, 0, 0)),
        small(shift_row), small(shift_col),
        pl.BlockSpec((1, 2, LANE), lambda b, h: (h, 0, 0)),
        pl.BlockSpec((1, DIFF_HEAD_DIM, 1), lambda b, h: (h, 0, 0)),
        small(lq1), small(lk1), small(lq2), small(lk2), small(g_sub),
    ]
    assert len(mla_specs) == N_MLA_IN and len(diff_specs) == N_DIFF_IN
    out_spec = pl.BlockSpec((seq, LANE), lambda b, h: (b, h))
    return pl.pallas_call(
        functools.partial(_attention_body, blk=blk, nblk=nblk, lam_init=lam_init),
        grid=(batch, MLA_HEADS),
        in_specs=[pl.BlockSpec(memory_space=pltpu.SMEM)] + mla_specs + diff_specs,
        out_specs=[out_spec, out_spec],
        out_shape=[jax.ShapeDtypeStruct((batch * seq, MLA_HEADS * MLA_V), BF16),
                   jax.ShapeDtypeStruct((batch * seq, width), BF16)],
        scratch_shapes=([pltpu.VMEM((blk // 2, blk // 2), F32)]
                        + _mla_scratch(blk, nblk) + _diff_scratch(seq, blk, nblk)),
        compiler_params=pltpu.CompilerParams(
            dimension_semantics=("arbitrary", "arbitrary"),
            vmem_limit_bytes=VMEM_LIMIT),
        name="attention",
    )(_pair_table(nblk), qt, k, vt, dqt, big_n, dvt, pos_icol, pos_irow, shift_row, shift_col,
      k_tab, q_col, lq1, lk1, lq2, lk2, g_sub)


def _outproj_body(x_ref, om_ref, od_ref, gm_ref, gd_ref, w_ref, gpost_ref, o_ref):
    gm = gm_ref[...].astype(F32)
    gd = gd_ref[...].astype(F32)
    mm = (om_ref[...].astype(F32) * (gm * jax.nn.sigmoid(gm))).astype(BF16)
    md = (od_ref[...].astype(F32) * (gd * jax.nn.sigmoid(gd))).astype(BF16)
    half = mm.shape[1]
    y = _dot(mm, w_ref[:half, :]) + _dot(md, w_ref[half:, :])
    o_ref[...] = x_ref[...] + _rms(y, gpost_ref[...])


def _outproj(x2, o_mla, o_diff, big, w_out, g_post, tm):
    m, d = x2.shape
    half = o_mla.shape[1]
    gate_diff_blk = (big.shape[1] - half) // half
    return pl.pallas_call(
        _outproj_body,
        grid=(m // tm,),
        in_specs=[
            pl.BlockSpec((tm, d), lambda i: (i, 0)),
            pl.BlockSpec((tm, half), lambda i: (i, 0)),
            pl.BlockSpec((tm, half), lambda i: (i, 0)),
            pl.BlockSpec((tm, half), lambda i: (i, 0)),
            pl.BlockSpec((tm, half), lambda i: (i, gate_diff_blk)),
            pl.BlockSpec(w_out.shape, lambda i: (0, 0)),
            pl.BlockSpec((1, d), lambda i: (0, 0)),
        ],
        out_specs=pl.BlockSpec((tm, d), lambda i: (i, 0)),
        out_shape=jax.ShapeDtypeStruct((m, d), F32),
        compiler_params=pltpu.CompilerParams(
            dimension_semantics=("arbitrary",),
            vmem_limit_bytes=VMEM_LIMIT),
        name="outproj",
    )(x2, o_mla, o_diff, big, big, w_out, g_post)


def kernel(x, positions, g_pre, w_in, g_q_a, w_q_b, g_kv_a, w_kv_b, lambda_q1, lambda_k1,
           lambda_q2, lambda_k2, g_diff_sub, w_out, g_post):
    batch, seq, d = x.shape
    depth = g_pre.shape[0]
    q_rank = w_q_b.shape[1]
    kv_rank = w_kv_b.shape[1]
    half_rope = MLA_ROPE // 2
    lat_end = q_rank + kv_rank
    pe_end = lat_end + MLA_ROPE

    pos_icol = positions.astype(jnp.int32).reshape(batch * seq, 1)
    pos_irow = positions.astype(jnp.int32).reshape(batch, 1, seq)
    freqs = 1.0 / (ROPE_THETA ** (jnp.arange(0, MLA_ROPE, 2, dtype=F32) / MLA_ROPE))
    freq = jnp.tile(freqs, 4).reshape(1, LANE)
    slopes = 2.0 ** (-8.0 * (jnp.arange(DIFF_HEADS, dtype=F32) + 1.0) / DIFF_HEADS)
    slopes_l2e = slopes * LOG2E

    mla_scale = MLA_QK ** -0.5 * LOG2E
    diff_scale = DIFF_HEAD_DIM ** -0.5 * LOG2E
    width = DIFF_HEADS * 2 * DIFF_HEAD_DIM

    x2 = x.reshape(batch * seq, d)
    for l in range(depth):
        lam_init = 0.8 - 0.6 * math.exp(-0.3 * l)
        w_lat, w_n, w_t = _regroup_w_in(w_in, l, lat_end, half_rope, width, rows=REGROUP_ROWS)
        wq = w_q_b[l]
        t1 = wq[:, :, MLA_NOPE:MLA_NOPE + half_rope]
        t2 = wq[:, :, MLA_NOPE + half_rope:]
        w_q = jnp.concatenate([wq[:, :, :MLA_NOPE], t1, t2, t2, t1], axis=-1)
        w_qt = w_q.reshape(q_rank, MLA_HEADS * QK_PAD).T.astype(BF16)
        wkv = w_kv_b[l]
        w_k = wkv[:, :, :MLA_NOPE].reshape(kv_rank, -1).astype(BF16)
        w_vt = wkv[:, :, MLA_NOPE:].reshape(kv_rank, -1).T.astype(BF16)
        gp = g_pre[l].reshape(1, d)

        h, big_n, dqt, dvt = _inproj(x2, gp, w_n, w_t, diff_scale, tm=PROJ_ROWS, tn=PROJ_COLS)
        qt, k, vt = _latent(h, pos_icol, freq, w_lat, g_q_a[l].reshape(1, -1), w_qt,
                            g_kv_a[l].reshape(1, -1), w_k, w_vt, tm=PROJ_ROWS, q_scale=mla_scale)
        o_mla, o_diff = _attention(
            qt, k, vt, big_n, dqt, dvt, pos_icol, pos_irow, slopes_l2e,
            lambda_q1[l].reshape(1, -1), lambda_k1[l].reshape(1, -1),
            lambda_q2[l].reshape(1, -1), lambda_k2[l].reshape(1, -1),
            g_diff_sub[l].reshape(1, -1), batch, seq, blk=ATTN_BLOCK, lam_init=lam_init)
        x2 = _outproj(x2, o_mla, o_diff, big_n, w_out[l].astype(BF16), g_post[l].reshape(1, d), tm=PROJ_ROWS)
    return x2.reshape(batch, seq, d)
```

```python
import functools
import math

import numpy as np
import jax
import jax.numpy as jnp
from jax import lax
from jax.experimental import pallas as pl
from jax.experimental.pallas import tpu as pltpu

F32 = jnp.float32
BF16 = jnp.bfloat16

EPS = 1e-6
LOG2E = 1.4426950408889634
ROPE_THETA = 10000.0

MLA_HEADS = 8
MLA_NOPE = 128
MLA_ROPE = 64
MLA_V = 128
VT_ROWS = MLA_V + 16
MLA_QK = MLA_NOPE + MLA_ROPE
DIFF_HEADS = 8
DIFF_HEAD_DIM = 64
LANE = 128
QK_PAD = 256
ALIBI_LANES = 9
ALIBI_BASE = (DIFF_HEAD_DIM, 0)

VMEM_LIMIT = 56 * 1024 * 1024

IN_ROWS = 256
PROJ_ROWS = 512
PROJ_COLS = 512
ATTN_BLOCK = 512
REGROUP_ROWS = 256


def _rms(xf, g):
    ms = jnp.mean(xf * xf, axis=-1, keepdims=True)
    return xf * lax.rsqrt(ms + EPS) * g


def _dot(a, b):
    return jnp.dot(a, b, preferred_element_type=F32)


def _dot_nt(a, b):
    return lax.dot_general(a, b, (((1,), (1,)), ((), ())), preferred_element_type=F32)


def _store_vt_ext(vt_ref, head, vt_head):
    r = head * VT_ROWS
    vt_ref[r:r + MLA_V, :] = vt_head
    vt_ref[r + MLA_V:r + VT_ROWS, :] = jnp.ones((VT_ROWS - MLA_V, vt_head.shape[1]), vt_head.dtype)


def _regroup_body(wt_ref, lat_ref, n_ref, t_ref, *, lat_end, half_rope, width):
    pe_end = lat_end + 2 * half_rope
    cols = lambda a, b: wt_ref[0, a:b, :]
    u1 = cols(lat_end, lat_end + half_rope)
    u2 = cols(lat_end + half_rope, pe_end)
    lat_ref[:, :lat_end] = cols(0, lat_end).T.astype(BF16)
    lat_ref[:, lat_end:] = jnp.concatenate([u1, u1, u2, u2, u2, u2, u1, u1], axis=0).T.astype(BF16)
    group = lambda n: cols(pe_end + n * width, pe_end + (n + 1) * width).T.astype(BF16)
    for dst, src in enumerate((0, 2, 4)):
        n_ref[:, dst * width:(dst + 1) * width] = group(src)
    for dst, src in enumerate((1, 3)):
        t_ref[:, dst * width:(dst + 1) * width] = group(src)


def _regroup_w_in(w_in, layer, lat_end, half_rope, width, rows):
    _, d, n_in = w_in.shape
    n_lat = lat_end + 8 * half_rope
    w_in_t = jnp.swapaxes(w_in, 1, 2)
    return pl.pallas_call(
        functools.partial(_regroup_body, lat_end=lat_end, half_rope=half_rope, width=width),
        grid=(d // rows,),
        in_specs=[pl.BlockSpec((1, n_in, rows), lambda i: (layer, 0, i))],
        out_specs=[
            pl.BlockSpec((rows, n_lat), lambda i: (i, 0)),
            pl.BlockSpec((rows, 3 * width), lambda i: (i, 0)),
            pl.BlockSpec((rows, 2 * width), lambda i: (i, 0)),
        ],
        out_shape=[
            jax.ShapeDtypeStruct((d, n_lat), BF16),
            jax.ShapeDtypeStruct((d, 3 * width), BF16),
            jax.ShapeDtypeStruct((d, 2 * width), BF16),
        ],
        compiler_params=pltpu.CompilerParams(
            dimension_semantics=("arbitrary",),
            vmem_limit_bytes=VMEM_LIMIT),
        name="regroup_w_in",
    )(w_in_t)


def _wide_projection(h, wn_ref, wt_ref, on_ref, oq_ref, ov_ref, tn, q_scale):
    for c in range(wn_ref.shape[1] // tn):
        cols = slice(c * tn, (c + 1) * tn)
        on_ref[:, cols] = _dot(h, wn_ref[:, cols]).astype(BF16)
    n_q = oq_ref.shape[0] // tn
    heads_per_chunk = tn // MLA_V
    for c in range(wt_ref.shape[1] // tn):
        cols = slice(c * tn, (c + 1) * tn)
        acc = _dot(h, wt_ref[:, cols])
        if c < n_q:
            oq_ref[cols, :] = (acc * q_scale).T.astype(BF16)
        else:
            vt = acc.T.astype(BF16)
            for j in range(heads_per_chunk):
                _store_vt_ext(ov_ref, (c - n_q) * heads_per_chunk + j, vt[j * MLA_V:(j + 1) * MLA_V, :])


def _latent_path(h, pos_ref, freq_ref, wlat_ref, gq_ref, wqt_ref,
                 gkv_ref, wk_ref, wvt_ref, qt_ref, k_ref, vt_ref, q_scale):
    lat = _dot(h, wlat_ref[...])
    ang = pos_ref[...].astype(F32) * freq_ref[...]
    cos = jnp.cos(ang)
    sin = jnp.sin(ang)
    seg = lax.broadcasted_iota(jnp.int32, (1, LANE), 1) // (MLA_ROPE // 2)
    fq = jnp.where(seg == 1, -sin, jnp.where(seg == 3, sin, cos)) * q_scale
    sk = jnp.where(seg < 2, -sin, sin)
    k_ext = (lat[:, 768:896] * cos + lat[:, 896:1024] * sk).astype(BF16)

    c_q = _rms(lat[:, :512], gq_ref[...]).astype(BF16)
    qft = _dot_nt(wqt_ref[...], c_q)
    fqt = fq.T
    c_kv = _rms(lat[:, 512:768], gkv_ref[...]).astype(BF16)
    kf = _dot(c_kv, wk_ref[...])
    vt = _dot_nt(wvt_ref[...], c_kv).astype(BF16)
    for hd in range(MLA_HEADS):
        _store_vt_ext(vt_ref, hd, vt[hd * MLA_V:(hd + 1) * MLA_V, :])
        o = hd * QK_PAD
        qt_ref[o:o + LANE, :] = (qft[o:o + LANE, :] * q_scale).astype(BF16)
        qt_ref[o + LANE:o + QK_PAD, :] = (qft[o + LANE:o + QK_PAD, :] * fqt).astype(BF16)
        k_ref[:, o:o + LANE] = kf[:, hd * LANE:(hd + 1) * LANE].astype(BF16)
        k_ref[:, o + LANE:o + QK_PAD] = k_ext


def _proj_body(x_ref, g_ref, pos_ref, freq_ref, wn_ref, wt_ref, wlat_ref, gq_ref, wqt_ref, gkv_ref,
               wk_ref, wvt_ref, on_ref, oq_ref, ov_ref, qt_ref, k_ref, vt_ref,
               *, tn, diff_scale, mla_scale):
    h = _rms(x_ref[...], g_ref[...]).astype(BF16)
    _latent_path(h, pos_ref, freq_ref, wlat_ref, gq_ref, wqt_ref, gkv_ref, wk_ref, wvt_ref,
                 qt_ref, k_ref, vt_ref, mla_scale)
    _wide_projection(h, wn_ref, wt_ref, on_ref, oq_ref, ov_ref, tn, diff_scale)


def _proj(x2, g_pre, pos_col, freq, w_n, w_t, w_lat, g_q, w_qt, g_kv, w_k, w_vt,
          diff_scale, mla_scale, tm, tn):
    m, d = x2.shape
    width = w_t.shape[1] // 2
    nq = MLA_HEADS * QK_PAD
    nv = MLA_HEADS * VT_ROWS
    assert width // MLA_V == MLA_HEADS
    resident = lambda a: pl.BlockSpec(a.shape, lambda i: (0,) * a.ndim, pipeline_mode=pl.Buffered(1))
    rows = lambda n: pl.BlockSpec((tm, n), lambda i: (i, 0))
    cols = lambda n: pl.BlockSpec((n, tm), lambda i: (0, i))
    return pl.pallas_call(
        functools.partial(_proj_body, tn=tn, diff_scale=diff_scale, mla_scale=mla_scale),
        grid=(m // tm,),
        in_specs=[rows(d), resident(g_pre), rows(1), resident(freq)]
                 + [resident(a) for a in (w_n, w_t, w_lat, g_q, w_qt, g_kv, w_k, w_vt)],
        out_specs=[rows(w_n.shape[1]), cols(width), cols(nv), cols(nq), rows(nq), cols(nv)],
        out_shape=[
            jax.ShapeDtypeStruct((m, w_n.shape[1]), BF16),
            jax.ShapeDtypeStruct((width, m), BF16),
            jax.ShapeDtypeStruct((nv, m), BF16),
            jax.ShapeDtypeStruct((nq, m), BF16),
            jax.ShapeDtypeStruct((m, nq), BF16),
            jax.ShapeDtypeStruct((nv, m), BF16),
        ],
        compiler_params=pltpu.CompilerParams(
            dimension_semantics=("arbitrary",),
            vmem_limit_bytes=VMEM_LIMIT),
        name="proj",
    )(x2, g_pre, pos_col, freq, w_n, w_t, w_lat, g_q, w_qt, g_kv, w_k, w_vt)


def _blk_slice(idx, blk):
    if isinstance(idx, int):
        return slice(idx * blk, (idx + 1) * blk)
    return pl.ds(pl.multiple_of(idx * blk, blk), blk)


def _offdiag_pairs(nblk):
    return [(i, t) for i in range(1, nblk) for t in range(i)]


def _pair_table(nblk):
    return jnp.asarray(np.array(_offdiag_pairs(nblk), dtype=np.int32).T)


def _half_slices(idx, blk):
    half = blk // 2
    if isinstance(idx, int):
        return slice(idx * blk, idx * blk + half), slice(idx * blk + half, (idx + 1) * blk)
    start = pl.multiple_of(idx * blk, blk)
    return pl.ds(start, half), pl.ds(pl.multiple_of(start + half, half), half)


def _init_mask_bias(mb_ref):
    kv = lax.broadcasted_iota(jnp.int32, mb_ref.shape, 0)
    qi = lax.broadcasted_iota(jnp.int32, mb_ref.shape, 1)
    mb_ref[...] = jnp.where(kv <= qi, 0.0, -jnp.inf).astype(F32)


def _diag_softmax(s_top, s_bot, mb):
    h = s_bot.shape[0]
    a0 = s_top[:, :h] + mb
    a1 = s_top[:, h:]
    b1 = s_bot + mb
    m0 = jnp.max(a0, axis=0, keepdims=True)
    m1 = jnp.maximum(jnp.max(a1, axis=0, keepdims=True), jnp.max(b1, axis=0, keepdims=True))
    p0 = jnp.exp2(a0 - m0)
    pa1 = jnp.exp2(a1 - m1)
    pb1 = jnp.exp2(b1 - m1)
    return jnp.concatenate([m0, m1], axis=1), jnp.concatenate([p0, pa1], axis=1), pb1


def _diag_pv(vt_top, vt_bot, p_top, p_bot):
    h = p_bot.shape[0]
    a = _dot(vt_top, p_top)
    b = _dot(vt_bot, p_bot)
    return jnp.concatenate([a[:, :h], a[:, h:] + b], axis=1)


def _run_pipeline(heads, tab_ref, nblk):
    A, A_DIAG, B_DIAG, C_DIAG, B, C = range(6)
    pairs = _offdiag_pairs(nblk)
    npairs = len(pairs)
    assert (nblk - 2) % 2 == 0 and (npairs - 2) % 2 == 0

    def step(c=None, b=None, a=None):
        for call in (c, b, a):
            if call is not None:
                for stages in heads:
                    stages[call[0]](*call[1])

    step(a=(A_DIAG, (0,)))
    step(b=(B_DIAG, (0,)), a=(A_DIAG, (1,)))

    def diag_loop(u, carry):
        for d in (1, 2):
            g = d + 2 * u
            step(c=(C_DIAG, (g - 1,)), b=(B_DIAG, (g,)), a=(A_DIAG, (g + 1,)))
        return carry

    lax.fori_loop(0, (nblk - 2) // 2, diag_loop, 0)

    step(c=(C_DIAG, (nblk - 2,)), b=(B_DIAG, (nblk - 1,)), a=(A, pairs[0]))
    step(c=(C_DIAG, (nblk - 1,)), b=(B, pairs[0]), a=(A, pairs[1]))

    def off_loop(u, carry):
        pair = lambda f: (tab_ref[0, f], tab_ref[1, f])
        for d in (1, 2):
            f = d + 2 * u
            step(c=(C, pair(f - 1)), b=(B, pair(f)), a=(A, pair(f + 1)))
        return carry

    lax.fori_loop(0, (npairs - 2) // 2, off_loop, 0)

    step(c=(C, pairs[npairs - 2]), b=(B, pairs[npairs - 1]))
    step(c=(C, pairs[npairs - 1]))


def _mla_stages(qt_ref, k_ref, vt_ref, o_ref, mb_ref, s_ref, p_ref, al_ref, m_ref, acc_ref, blk, nblk):
    half = blk // 2

    def stage_a(i, t):
        s_ref[...] = _dot(k_ref[_blk_slice(t, blk), :], qt_ref[:, _blk_slice(i, blk)])

    def stage_a_diag(i):
        top, bot = _half_slices(i, blk)
        s_ref[:half, :] = _dot(k_ref[top, :], qt_ref[:, _blk_slice(i, blk)])
        s_ref[half:, half:] = _dot(k_ref[bot, :], qt_ref[:, bot])

    def stage_b_diag(i):
        m, p_top, p_bot = _diag_softmax(s_ref[:half, :], s_ref[half:, half:], mb_ref[...])
        m_ref[i] = m
        p_ref[:half, :] = p_top.astype(BF16)
        p_ref[half:, half:] = p_bot.astype(BF16)

    def stage_c_diag(i):
        top, bot = _half_slices(i, blk)
        acc_ref[i] = _diag_pv(vt_ref[:, top], vt_ref[:, bot], p_ref[:half, :], p_ref[half:, half:])

    def stage_b(i, t):
        s = s_ref[...]
        m_prev = m_ref[i]
        m_new = jnp.maximum(m_prev, jnp.max(s, axis=0, keepdims=True))
        alpha = jnp.exp2(m_prev - m_new)
        p = jnp.exp2(s - m_new)
        m_ref[i] = m_new
        al_ref[...] = alpha
        p_ref[...] = p.astype(BF16)

    def stage_c(i, t):
        acc_ref[i] = al_ref[...] * acc_ref[i] + _dot(vt_ref[:, _blk_slice(t, blk)], p_ref[...])

    def finalize():
        for i in range(nblk):
            acc = acc_ref[i]
            o_t = acc[:MLA_V] * (1.0 / acc[MLA_V:MLA_V + 1])
            o_ref[_blk_slice(i, blk), :] = o_t.T.astype(BF16)

    return (stage_a, stage_a_diag, stage_b_diag, stage_c_diag, stage_b, stage_c), finalize


def _mla_scratch(blk, nblk):
    return [
        pltpu.VMEM((blk, blk), F32),
        pltpu.VMEM((blk, blk), BF16),
        pltpu.VMEM((1, blk), F32),
        pltpu.VMEM((nblk, 1, blk), F32),
        pltpu.VMEM((nblk, VT_ROWS, blk), F32),
    ]


def _alibi_lane_tables(slopes_l2e):
    c1 = slopes_l2e.astype(BF16).astype(F32)
    c2 = (slopes_l2e - c1).astype(BF16).astype(F32)
    c3 = (slopes_l2e - c1 - c2).astype(BF16).astype(F32)
    cw = (jnp.stack([c1, c2, c3], axis=1)[:, :, None]
          * jnp.asarray([1.0, 256.0, 65536.0], F32)[None, None, :]).reshape(-1, ALIBI_LANES)
    heads = slopes_l2e.shape[0]
    ktab = jnp.zeros((heads, 2, LANE), F32)
    for x, base in enumerate(ALIBI_BASE):
        ktab = ktab.at[:, x, base:base + ALIBI_LANES].set(-cw)
    qcol = jnp.zeros((heads, DIFF_HEAD_DIM), F32).at[:, ALIBI_LANES:2 * ALIBI_LANES].set(cw)
    return ktab.astype(BF16), qcol.reshape(heads, DIFF_HEAD_DIM, 1)


def _diff_stages(qt_ref, k_ref, vt_ref, pc_ref, pr_ref, shr_ref, shc_ref, kt_ref, qc_ref,
                 lq1_ref, lk1_ref, lq2_ref, lk2_ref, gsub_ref, o_ref, mb_ref, dig_ref, digt_ref,
                 qm_ref, km_ref, s_ref, p_ref, al_ref, m_ref, acc_ref, blk, nblk, lam_init):
    half = DIFF_HEAD_DIM

    @pl.when(pl.program_id(1) == 0)
    def _():
        pcol = pc_ref[...]
        prel = jnp.broadcast_to(pcol - pcol[0:1, :], dig_ref.shape)
        shift = jnp.broadcast_to(shr_ref[...], dig_ref.shape)
        dig_ref[...] = (lax.shift_right_logical(prel, shift) & 255).astype(F32).astype(BF16)
        prow = pr_ref[0]
        prel_t = jnp.broadcast_to(prow - prow[:, 0:1], digt_ref.shape)
        shift_t = jnp.broadcast_to(shc_ref[...], digt_ref.shape)
        digt_ref[...] = (lax.shift_right_logical(prel_t, shift_t) & 255).astype(F32).astype(BF16)

    row = lax.broadcasted_iota(jnp.int32, (half, 1), 0)
    alibi_q = jnp.where(row < ALIBI_LANES, digt_ref[...],
                        jnp.broadcast_to(qc_ref[0], digt_ref.shape).astype(BF16))
    qm_ref[0, :half, :] = qt_ref[:half, :]
    qm_ref[0, half:, :] = alibi_q
    qm_ref[1, :half, :] = alibi_q
    qm_ref[1, half:, :] = qt_ref[half:, :]
    lane = lax.broadcasted_iota(jnp.int32, (1, LANE), 1)
    dig = dig_ref[...]
    k = k_ref[...]
    for x, base in enumerate(ALIBI_BASE):
        own = (lane < half) if x == 0 else (lane >= half)
        k_digit = (lane >= base + ALIBI_LANES) & (lane < base + 2 * ALIBI_LANES)
        km_ref[x] = jnp.where(own, k, jnp.where(k_digit, dig, kt_ref[0, x:x + 1, :]))

    def stage_a(i, t):
        for x in range(2):
            s_ref[x] = _dot(km_ref[x, _blk_slice(t, blk), :], qm_ref[x, :, _blk_slice(i, blk)])

    hb = blk // 2

    def stage_a_diag(i):
        top, bot = _half_slices(i, blk)
        for x in range(2):
            s_ref[x, :hb, :] = _dot(km_ref[x, top, :], qm_ref[x, :, _blk_slice(i, blk)])
            s_ref[x, hb:, hb:] = _dot(km_ref[x, bot, :], qm_ref[x, :, bot])

    def stage_b_diag(i):
        for x in range(2):
            m, p_top, p_bot = _diag_softmax(s_ref[x, :hb, :], s_ref[x, hb:, hb:], mb_ref[...])
            m_ref[x, i] = m
            p_ref[x, :hb, :] = p_top.astype(BF16)
            p_ref[x, hb:, hb:] = p_bot.astype(BF16)

    def stage_c_diag(i):
        top, bot = _half_slices(i, blk)
        for x in range(2):
            acc_ref[x, i] = _diag_pv(vt_ref[:, top], vt_ref[:, bot], p_ref[x, :hb, :], p_ref[x, hb:, hb:])

    def stage_b(i, t):
        for x in range(2):
            s = s_ref[x]
            m_prev = m_ref[x, i]
            m_new = jnp.maximum(m_prev, jnp.max(s, axis=0, keepdims=True))
            alpha = jnp.exp2(m_prev - m_new)
            p = jnp.exp2(s - m_new)
            m_ref[x, i] = m_new
            al_ref[x] = alpha
            p_ref[x] = p.astype(BF16)

    def stage_c(i, t):
        vt = vt_ref[:, _blk_slice(t, blk)]
        for x in range(2):
            acc_ref[x, i] = al_ref[x] * acc_ref[x, i] + _dot(vt, p_ref[x])

    def finalize():
        lam = (jnp.exp(jnp.sum(lq1_ref[...] * lk1_ref[...], axis=-1, keepdims=True))
               - jnp.exp(jnp.sum(lq2_ref[...] * lk2_ref[...], axis=-1, keepdims=True))
               + lam_init)
        for i in range(nblk):
            a1, a2 = acc_ref[0, i], acc_ref[1, i]
            o_t = (a1[:MLA_V] * (1.0 / a1[MLA_V:MLA_V + 1])
                   - lam * (a2[:MLA_V] * (1.0 / a2[MLA_V:MLA_V + 1])))
            o = o_t.T
            o_ref[_blk_slice(i, blk), :] = (_rms(o, gsub_ref[...]) * (1.0 - lam_init)).astype(BF16)

    return (stage_a, stage_a_diag, stage_b_diag, stage_c_diag, stage_b, stage_c), finalize


def _diff_scratch(seq, blk, nblk):
    return [
        pltpu.VMEM((seq, LANE), BF16),
        pltpu.VMEM((DIFF_HEAD_DIM, seq), BF16),
        pltpu.VMEM((2, LANE, seq), BF16),
        pltpu.VMEM((2, seq, LANE), BF16),
        pltpu.VMEM((2, blk, blk), F32),
        pltpu.VMEM((2, blk, blk), BF16),
        pltpu.VMEM((2, 1, blk), F32),
        pltpu.VMEM((2, nblk, 1, blk), F32),
        pltpu.VMEM((2, nblk, VT_ROWS, blk), F32),
    ]


N_MLA_IN, N_DIFF_IN = 3, 14
N_MLA_SCRATCH, N_DIFF_SCRATCH = 5, 9


def _attention_body(tab_ref, *refs, blk, nblk, lam_init):
    mla_in, refs = refs[:N_MLA_IN], refs[N_MLA_IN:]
    diff_in, refs = refs[:N_DIFF_IN], refs[N_DIFF_IN:]
    (o_mla_ref, o_diff_ref, mb_ref), refs = refs[:3], refs[3:]
    mla_scr, diff_scr = refs[:N_MLA_SCRATCH], refs[N_MLA_SCRATCH:]
    assert len(diff_scr) == N_DIFF_SCRATCH

    @pl.when((pl.program_id(0) == 0) & (pl.program_id(1) == 0))
    def _():
        _init_mask_bias(mb_ref)

    mla, mla_fin = _mla_stages(*mla_in, o_mla_ref, mb_ref, *mla_scr, blk, nblk)
    diff, diff_fin = _diff_stages(*diff_in, o_diff_ref, mb_ref, *diff_scr, blk, nblk, lam_init)

    _run_pipeline([mla, diff], tab_ref, nblk)
    mla_fin()
    diff_fin()


def _attention(qt, k, vt, big_n, dqt, dvt, pos_icol, pos_irow, slopes_l2e, lq1, lk1, lq2, lk2, g_sub,
               batch, seq, blk, lam_init):
    assert MLA_HEADS == DIFF_HEADS
    nblk = seq // blk
    assert nblk >= 3
    width = DIFF_HEADS * 2 * DIFF_HEAD_DIM
    cb = width // LANE
    k_off = 1 * cb
    k_tab, q_col = _alibi_lane_tables(slopes_l2e)
    digit_k = np.arange(LANE) % DIFF_HEAD_DIM % 3
    shift_row = jnp.asarray((digit_k * 8).reshape(1, LANE), jnp.int32)
    shift_col = jnp.asarray((digit_k[:DIFF_HEAD_DIM] * 8).reshape(DIFF_HEAD_DIM, 1), jnp.int32)
    small = lambda a: pl.BlockSpec(a.shape, lambda b, h: (0,) * a.ndim)
    mla_specs = [
        pl.BlockSpec((QK_PAD, seq), lambda b, h: (h, b)),
        pl.BlockSpec((seq, QK_PAD), lambda b, h: (b, h)),
        pl.BlockSpec((VT_ROWS, seq), lambda b, h: (h, b)),
    ]
    diff_specs = [
        pl.BlockSpec((LANE, seq), lambda b, h: (h, b)),
        pl.BlockSpec((seq, LANE), lambda b, h: (b, k_off + h)),
        pl.BlockSpec((VT_ROWS, seq), lambda b, h: (h, b)),
        pl.BlockSpec((seq, 1), lambda b, h: (b, 0)),
        pl.BlockSpec((1, 1, seq), lambda b, h: (b, 0, 0)),
        small(shift_row), small(shift_col),
        pl.BlockSpec((1, 2, LANE), lambda b, h: (h, 0, 0)),
        pl.BlockSpec((1, DIFF_HEAD_DIM, 1), lambda b, h: (h, 0, 0)),
        small(lq1), small(lk1), small(lq2), small(lk2), small(g_sub),
    ]
    assert len(mla_specs) == N_MLA_IN and len(diff_specs) == N_DIFF_IN
    out_spec = pl.BlockSpec((seq, LANE), lambda b, h: (b, h))
    return pl.pallas_call(
        functools.partial(_attention_body, blk=blk, nblk=nblk, lam_init=lam_init),
        grid=(batch, MLA_HEADS),
        in_specs=[pl.BlockSpec(memory_space=pltpu.SMEM)] + mla_specs + diff_specs,
        out_specs=[out_spec, out_spec],
        out_shape=[jax.ShapeDtypeStruct((batch * seq, MLA_HEADS * MLA_V), BF16),
                   jax.ShapeDtypeStruct((batch * seq, width), BF16)],
        scratch_shapes=([pltpu.VMEM((blk // 2, blk // 2), F32)]
                        + _mla_scratch(blk, nblk) + _diff_scratch(seq, blk, nblk)),
        compiler_params=pltpu.CompilerParams(
            dimension_semantics=("arbitrary", "arbitrary"),
            vmem_limit_bytes=VMEM_LIMIT),
        name="attention",
    )(_pair_table(nblk), qt, k, vt, dqt, big_n, dvt, pos_icol, pos_irow, shift_row, shift_col,
      k_tab, q_col, lq1, lk1, lq2, lk2, g_sub)


def _outproj_body(x_ref, om_ref, od_ref, gm_ref, gd_ref, w_ref, gpost_ref, o_ref):
    gm = gm_ref[...].astype(F32)
    gd = gd_ref[...].astype(F32)
    mm = (om_ref[...].astype(F32) * (gm * jax.nn.sigmoid(gm))).astype(BF16)
    md = (od_ref[...].astype(F32) * (gd * jax.nn.sigmoid(gd))).astype(BF16)
    half = mm.shape[1]
    y = _dot(mm, w_ref[:half, :]) + _dot(md, w_ref[half:, :])
    o_ref[...] = x_ref[...] + _rms(y, gpost_ref[...])


def _outproj(x2, o_mla, o_diff, big, w_out, g_post, tm):
    m, d = x2.shape
    half = o_mla.shape[1]
    gate_diff_blk = (big.shape[1] - half) // half
    return pl.pallas_call(
        _outproj_body,
        grid=(m // tm,),
        in_specs=[
            pl.BlockSpec((tm, d), lambda i: (i, 0)),
            pl.BlockSpec((tm, half), lambda i: (i, 0)),
            pl.BlockSpec((tm, half), lambda i: (i, 0)),
            pl.BlockSpec((tm, half), lambda i: (i, 0)),
            pl.BlockSpec((tm, half), lambda i: (i, gate_diff_blk)),
            pl.BlockSpec(w_out.shape, lambda i: (0, 0)),
            pl.BlockSpec((1, d), lambda i: (0, 0)),
        ],
        out_specs=pl.BlockSpec((tm, d), lambda i: (i, 0)),
        out_shape=jax.ShapeDtypeStruct((m, d), F32),
        compiler_params=pltpu.CompilerParams(
            dimension_semantics=("arbitrary",),
            vmem_limit_bytes=VMEM_LIMIT),
        name="outproj",
    )(x2, o_mla, o_diff, big, big, w_out, g_post)


def kernel(x, positions, g_pre, w_in, g_q_a, w_q_b, g_kv_a, w_kv_b, lambda_q1, lambda_k1,
           lambda_q2, lambda_k2, g_diff_sub, w_out, g_post):
    batch, seq, d = x.shape
    depth = g_pre.shape[0]
    q_rank = w_q_b.shape[1]
    kv_rank = w_kv_b.shape[1]
    half_rope = MLA_ROPE // 2
    lat_end = q_rank + kv_rank
    pe_end = lat_end + MLA_ROPE

    pos_icol = positions.astype(jnp.int32).reshape(batch * seq, 1)
    pos_irow = positions.astype(jnp.int32).reshape(batch, 1, seq)
    freqs = 1.0 / (ROPE_THETA ** (jnp.arange(0, MLA_ROPE, 2, dtype=F32) / MLA_ROPE))
    freq = jnp.tile(freqs, 4).reshape(1, LANE)
    slopes = 2.0 ** (-8.0 * (jnp.arange(DIFF_HEADS, dtype=F32) + 1.0) / DIFF_HEADS)
    slopes_l2e = slopes * LOG2E

    mla_scale = MLA_QK ** -0.5 * LOG2E
    diff_scale = DIFF_HEAD_DIM ** -0.5 * LOG2E
    width = DIFF_HEADS * 2 * DIFF_HEAD_DIM

    x2 = x.reshape(batch * seq, d)
    for l in range(depth):
        lam_init = 0.8 - 0.6 * math.exp(-0.3 * l)
        w_lat, w_n, w_t = _regroup_w_in(w_in, l, lat_end, half_rope, width, rows=REGROUP_ROWS)
        wq = w_q_b[l]
        t1 = wq[:, :, MLA_NOPE:MLA_NOPE + half_rope]
        t2 = wq[:, :, MLA_NOPE + half_rope:]
        w_q = jnp.concatenate([wq[:, :, :MLA_NOPE], t1, t2, t2, t1], axis=-1)
        w_qt = w_q.reshape(q_rank, MLA_HEADS * QK_PAD).T.astype(BF16)
        wkv = w_kv_b[l]
        w_k = wkv[:, :, :MLA_NOPE].reshape(kv_rank, -1).astype(BF16)
        w_vt = wkv[:, :, MLA_NOPE:].reshape(kv_rank, -1).T.astype(BF16)
        gp = g_pre[l].reshape(1, d)

        big_n, dqt, dvt, qt, k, vt = _proj(
            x2, gp, pos_icol, freq, w_n, w_t, w_lat, g_q_a[l].reshape(1, -1), w_qt,
            g_kv_a[l].reshape(1, -1), w_k, w_vt, diff_scale, mla_scale, tm=IN_ROWS, tn=PROJ_COLS)
        o_mla, o_diff = _attention(
            qt, k, vt, big_n, dqt, dvt, pos_icol, pos_irow, slopes_l2e,
            lambda_q1[l].reshape(1, -1), lambda_k1[l].reshape(1, -1),
            lambda_q2[l].reshape(1, -1), lambda_k2[l].reshape(1, -1),
            g_diff_sub[l].reshape(1, -1), batch, seq, blk=ATTN_BLOCK, lam_init=lam_init)
        x2 = _outproj(x2, o_mla, o_diff, big_n, w_out[l].astype(BF16), g_post[l].reshape(1, d), tm=PROJ_ROWS)
    return x2.reshape(batch, seq, d)
```

```python
import functools
import math

import numpy as np
import jax
import jax.numpy as jnp
from jax import lax
from jax.experimental import pallas as pl
from jax.experimental.pallas import tpu as pltpu

F32 = jnp.float32
BF16 = jnp.bfloat16

EPS = 1e-6
LOG2E = 1.4426950408889634
ROPE_THETA = 10000.0

MLA_HEADS = 8
MLA_NOPE = 128
MLA_ROPE = 64
MLA_V = 128
VT_ROWS = MLA_V + 16
MLA_QK = MLA_NOPE + MLA_ROPE
DIFF_HEADS = 8
DIFF_HEAD_DIM = 64
LANE = 128
QK_PAD = 256
ALIBI_LANES = 9
ALIBI_BASE = (DIFF_HEAD_DIM, 0)

VMEM_LIMIT = 56 * 1024 * 1024

IN_ROWS = 256
PROJ_ROWS = 512
PROJ_COLS = 512
ATTN_BLOCK = 512
REGROUP_ROWS = 256


def _rms(xf, g):
    ms = jnp.mean(xf * xf, axis=-1, keepdims=True)
    return xf * lax.rsqrt(ms + EPS) * g


def _dot(a, b):
    return jnp.dot(a, b, preferred_element_type=F32)


def _dot_nt(a, b):
    return lax.dot_general(a, b, (((1,), (1,)), ((), ())), preferred_element_type=F32)


def _store_vt_ext(vt_ref, head, vt_head):
    r = head * VT_ROWS
    vt_ref[r:r + MLA_V, :] = vt_head
    vt_ref[r + MLA_V:r + VT_ROWS, :] = jnp.ones((VT_ROWS - MLA_V, vt_head.shape[1]), vt_head.dtype)


def _regroup_body(wt_ref, lat_ref, n_ref, t_ref, *, lat_end, half_rope, width):
    pe_end = lat_end + 2 * half_rope
    cols = lambda a, b: wt_ref[0, a:b, :]
    u1 = cols(lat_end, lat_end + half_rope)
    u2 = cols(lat_end + half_rope, pe_end)
    lat_ref[:, :lat_end] = cols(0, lat_end).T.astype(BF16)
    lat_ref[:, lat_end:] = jnp.concatenate([u1, u1, u2, u2, u2, u2, u1, u1], axis=0).T.astype(BF16)
    group = lambda n: cols(pe_end + n * width, pe_end + (n + 1) * width).T.astype(BF16)
    for dst, src in enumerate((0, 2, 4)):
        n_ref[:, dst * width:(dst + 1) * width] = group(src)
    for dst, src in enumerate((1, 3)):
        t_ref[:, dst * width:(dst + 1) * width] = group(src)


def _regroup_w_in(w_in, layer, lat_end, half_rope, width, rows):
    _, d, n_in = w_in.shape
    n_lat = lat_end + 8 * half_rope
    w_in_t = jnp.swapaxes(w_in, 1, 2)
    return pl.pallas_call(
        functools.partial(_regroup_body, lat_end=lat_end, half_rope=half_rope, width=width),
        grid=(d // rows,),
        in_specs=[pl.BlockSpec((1, n_in, rows), lambda i: (layer, 0, i))],
        out_specs=[
            pl.BlockSpec((rows, n_lat), lambda i: (i, 0)),
            pl.BlockSpec((rows, 3 * width), lambda i: (i, 0)),
            pl.BlockSpec((rows, 2 * width), lambda i: (i, 0)),
        ],
        out_shape=[
            jax.ShapeDtypeStruct((d, n_lat), BF16),
            jax.ShapeDtypeStruct((d, 3 * width), BF16),
            jax.ShapeDtypeStruct((d, 2 * width), BF16),
        ],
        compiler_params=pltpu.CompilerParams(
            dimension_semantics=("arbitrary",),
            vmem_limit_bytes=VMEM_LIMIT),
        name="regroup_w_in",
    )(w_in_t)


def _wide_projection(h, wn_ref, wt_ref, on_ref, oq_ref, ov_ref, tn, q_scale):
    for c in range(wn_ref.shape[1] // tn):
        cols = slice(c * tn, (c + 1) * tn)
        on_ref[:, cols] = _dot(h, wn_ref[:, cols]).astype(BF16)
    n_q = oq_ref.shape[0] // tn
    heads_per_chunk = tn // MLA_V
    for c in range(wt_ref.shape[1] // tn):
        cols = slice(c * tn, (c + 1) * tn)
        acc = _dot(h, wt_ref[:, cols])
        if c < n_q:
            oq_ref[cols, :] = (acc * q_scale).T.astype(BF16)
        else:
            vt = acc.T.astype(BF16)
            for j in range(heads_per_chunk):
                _store_vt_ext(ov_ref, (c - n_q) * heads_per_chunk + j, vt[j * MLA_V:(j + 1) * MLA_V, :])


def _latent_path(h, pos_ref, freq_ref, wlat_ref, gq_ref, wqt_ref,
                 gkv_ref, wk_ref, wvt_ref, qt_ref, k_ref, vt_ref, q_scale):
    lat = _dot(h, wlat_ref[...])
    ang = pos_ref[...].astype(F32) * freq_ref[...]
    cos = jnp.cos(ang)
    sin = jnp.sin(ang)
    seg = lax.broadcasted_iota(jnp.int32, (1, LANE), 1) // (MLA_ROPE // 2)
    fq = jnp.where(seg == 1, -sin, jnp.where(seg == 3, sin, cos)) * q_scale
    sk = jnp.where(seg < 2, -sin, sin)
    k_ext = (lat[:, 768:896] * cos + lat[:, 896:1024] * sk).astype(BF16)

    c_q = _rms(lat[:, :512], gq_ref[...]).astype(BF16)
    qft = _dot_nt(wqt_ref[...], c_q)
    fqt = fq.T
    c_kv = _rms(lat[:, 512:768], gkv_ref[...]).astype(BF16)
    kf = _dot(c_kv, wk_ref[...])
    vt = _dot_nt(wvt_ref[...], c_kv).astype(BF16)
    for hd in range(MLA_HEADS):
        _store_vt_ext(vt_ref, hd, vt[hd * MLA_V:(hd + 1) * MLA_V, :])
        o = hd * QK_PAD
        qt_ref[o:o + LANE, :] = (qft[o:o + LANE, :] * q_scale).astype(BF16)
        qt_ref[o + LANE:o + QK_PAD, :] = (qft[o + LANE:o + QK_PAD, :] * fqt).astype(BF16)
        k_ref[:, o:o + LANE] = kf[:, hd * LANE:(hd + 1) * LANE].astype(BF16)
        k_ref[:, o + LANE:o + QK_PAD] = k_ext


def _proj_body(x_ref, g_ref, pos_ref, freq_ref, wn_ref, wt_ref, wlat_ref, gq_ref, wqt_ref, gkv_ref,
               wk_ref, wvt_ref, on_ref, oq_ref, ov_ref, qt_ref, k_ref, vt_ref,
               *, tn, diff_scale, mla_scale):
    h = _rms(x_ref[...], g_ref[...]).astype(BF16)
    _latent_path(h, pos_ref, freq_ref, wlat_ref, gq_ref, wqt_ref, gkv_ref, wk_ref, wvt_ref,
                 qt_ref, k_ref, vt_ref, mla_scale)
    _wide_projection(h, wn_ref, wt_ref, on_ref, oq_ref, ov_ref, tn, diff_scale)


def _proj(x2, g_pre, pos_col, freq, w_n, w_t, w_lat, g_q, w_qt, g_kv, w_k, w_vt,
          diff_scale, mla_scale, tm, tn):
    m, d = x2.shape
    width = w_t.shape[1] // 2
    nq = MLA_HEADS * QK_PAD
    nv = MLA_HEADS * VT_ROWS
    assert width // MLA_V == MLA_HEADS
    resident = lambda a: pl.BlockSpec(a.shape, lambda i: (0,) * a.ndim, pipeline_mode=pl.Buffered(1))
    rows = lambda n: pl.BlockSpec((tm, n), lambda i: (i, 0))
    cols = lambda n: pl.BlockSpec((n, tm), lambda i: (0, i))
    return pl.pallas_call(
        functools.partial(_proj_body, tn=tn, diff_scale=diff_scale, mla_scale=mla_scale),
        grid=(m // tm,),
        in_specs=[rows(d), resident(g_pre), rows(1), resident(freq)]
                 + [resident(a) for a in (w_n, w_t, w_lat, g_q, w_qt, g_kv, w_k, w_vt)],
        out_specs=[rows(w_n.shape[1]), cols(width), cols(nv), cols(nq), rows(nq), cols(nv)],
        out_shape=[
            jax.ShapeDtypeStruct((m, w_n.shape[1]), BF16),
            jax.ShapeDtypeStruct((width, m), BF16),
            jax.ShapeDtypeStruct((nv, m), BF16),
            jax.ShapeDtypeStruct((nq, m), BF16),
            jax.ShapeDtypeStruct((m, nq), BF16),
            jax.ShapeDtypeStruct((nv, m), BF16),
        ],
        compiler_params=pltpu.CompilerParams(
            dimension_semantics=("arbitrary",),
            vmem_limit_bytes=VMEM_LIMIT),
        name="proj",
    )(x2, g_pre, pos_col, freq, w_n, w_t, w_lat, g_q, w_qt, g_kv, w_k, w_vt)


def _blk_slice(idx, blk):
    if isinstance(idx, int):
        return slice(idx * blk, (idx + 1) * blk)
    return pl.ds(pl.multiple_of(idx * blk, blk), blk)


def _offdiag_pairs(nblk):
    return [(i, t) for i in range(1, nblk) for t in range(i)]


def _pair_table(nblk):
    return jnp.asarray(np.array(_offdiag_pairs(nblk), dtype=np.int32).T)


def _half_slices(idx, blk):
    half = blk // 2
    if isinstance(idx, int):
        return slice(idx * blk, idx * blk + half), slice(idx * blk + half, (idx + 1) * blk)
    start = pl.multiple_of(idx * blk, blk)
    return pl.ds(start, half), pl.ds(pl.multiple_of(start + half, half), half)


def _init_mask_bias(mb_ref):
    kv = lax.broadcasted_iota(jnp.int32, mb_ref.shape, 0)
    qi = lax.broadcasted_iota(jnp.int32, mb_ref.shape, 1)
    mb_ref[...] = jnp.where(kv <= qi, 0.0, -jnp.inf).astype(F32)


def _diag_softmax(s_top, s_bot, mb):
    h = s_bot.shape[0]
    a0 = s_top[:, :h] + mb
    a1 = s_top[:, h:]
    b1 = s_bot + mb
    m0 = jnp.max(a0, axis=0, keepdims=True)
    m1 = jnp.maximum(jnp.max(a1, axis=0, keepdims=True), jnp.max(b1, axis=0, keepdims=True))
    p0 = jnp.exp2(a0 - m0)
    pa1 = jnp.exp2(a1 - m1)
    pb1 = jnp.exp2(b1 - m1)
    return jnp.concatenate([m0, m1], axis=1), jnp.concatenate([p0, pa1], axis=1), pb1


def _diag_pv(vt_top, vt_bot, p_top, p_bot):
    h = p_bot.shape[0]
    a = _dot(vt_top, p_top)
    b = _dot(vt_bot, p_bot)
    return jnp.concatenate([a[:, :h], a[:, h:] + b], axis=1)


def _run_pipeline(heads, tab_ref, nblk):
    A, A_DIAG, B_DIAG, C_DIAG, B, C = range(6)
    pairs = _offdiag_pairs(nblk)
    npairs = len(pairs)
    assert (nblk - 2) % 2 == 0 and (npairs - 2) % 2 == 0

    def step(c=None, b=None, a=None, interleave=False):
        if not interleave:
            for call in (c, b, a):
                if call is not None:
                    for stages in heads:
                        stages[call[0]](*call[1])
            return
        if c is not None:
            for stages in heads:
                stages[c[0]](*c[1])
        for stages in heads[::-1]:
            for call in (b, a):
                if call is not None:
                    stages[call[0]](*call[1])

    step(a=(A_DIAG, (0,)))
    step(b=(B_DIAG, (0,)), a=(A_DIAG, (1,)))

    def diag_loop(u, carry):
        for d in (1, 2):
            g = d + 2 * u
            step(c=(C_DIAG, (g - 1,)), b=(B_DIAG, (g,)), a=(A_DIAG, (g + 1,)))
        return carry

    lax.fori_loop(0, (nblk - 2) // 2, diag_loop, 0)

    step(c=(C_DIAG, (nblk - 2,)), b=(B_DIAG, (nblk - 1,)), a=(A, pairs[0]))
    step(c=(C_DIAG, (nblk - 1,)), b=(B, pairs[0]), a=(A, pairs[1]))

    def off_loop(u, carry):
        pair = lambda f: (tab_ref[0, f], tab_ref[1, f])
        for d in (1, 2):
            f = d + 2 * u
            step(c=(C, pair(f - 1)), b=(B, pair(f)), a=(A, pair(f + 1)), interleave=True)
        return carry

    lax.fori_loop(0, (npairs - 2) // 2, off_loop, 0)

    step(c=(C, pairs[npairs - 2]), b=(B, pairs[npairs - 1]))
    step(c=(C, pairs[npairs - 1]))


def _mla_stages(qt_ref, k_ref, vt_ref, o_ref, mb_ref, s_ref, p_ref, al_ref, m_ref, acc_ref, pv_ref,
                blk, nblk):
    half = blk // 2

    def stage_a(i, t):
        s_ref[...] = _dot(k_ref[_blk_slice(t, blk), :], qt_ref[:, _blk_slice(i, blk)])

    def stage_a_diag(i):
        top, bot = _half_slices(i, blk)
        s_ref[:half, :] = _dot(k_ref[top, :], qt_ref[:, _blk_slice(i, blk)])
        s_ref[half:, half:] = _dot(k_ref[bot, :], qt_ref[:, bot])

    def stage_b_diag(i):
        m, p_top, p_bot = _diag_softmax(s_ref[:half, :], s_ref[half:, half:], mb_ref[...])
        m_ref[i] = m
        p_ref[:half, :] = p_top.astype(BF16)
        p_ref[half:, half:] = p_bot.astype(BF16)

    def stage_c_diag(i):
        top, bot = _half_slices(i, blk)
        acc_ref[i] = _diag_pv(vt_ref[:, top], vt_ref[:, bot], p_ref[:half, :], p_ref[half:, half:])

    def stage_b(i, t):
        s = s_ref[...]
        m_prev = m_ref[i]
        m_new = jnp.maximum(m_prev, jnp.max(s, axis=0, keepdims=True))
        alpha = jnp.exp2(m_prev - m_new)
        p = jnp.exp2(s - m_new)
        m_ref[i] = m_new
        al_ref[...] = alpha
        p_ref[...] = p.astype(BF16)

    def stage_c(i, t):
        pv_ref[...] = _dot(vt_ref[:, _blk_slice(t, blk)], p_ref[...])
        acc_ref[i] = al_ref[...] * acc_ref[i] + pv_ref[...]

    def finalize():
        for i in range(nblk):
            acc = acc_ref[i]
            o_t = acc[:MLA_V] * (1.0 / acc[MLA_V:MLA_V + 1])
            o_ref[_blk_slice(i, blk), :] = o_t.T.astype(BF16)

    return (stage_a, stage_a_diag, stage_b_diag, stage_c_diag, stage_b, stage_c), finalize


def _mla_scratch(blk, nblk):
    return [
        pltpu.VMEM((blk, blk), F32),
        pltpu.VMEM((blk, blk), BF16),
        pltpu.VMEM((1, blk), F32),
        pltpu.VMEM((nblk, 1, blk), F32),
        pltpu.VMEM((nblk, VT_ROWS, blk), F32),
        pltpu.VMEM((VT_ROWS, blk), F32),
    ]


def _alibi_lane_tables(slopes_l2e):
    c1 = slopes_l2e.astype(BF16).astype(F32)
    c2 = (slopes_l2e - c1).astype(BF16).astype(F32)
    c3 = (slopes_l2e - c1 - c2).astype(BF16).astype(F32)
    cw = (jnp.stack([c1, c2, c3], axis=1)[:, :, None]
          * jnp.asarray([1.0, 256.0, 65536.0], F32)[None, None, :]).reshape(-1, ALIBI_LANES)
    heads = slopes_l2e.shape[0]
    ktab = jnp.zeros((heads, 2, LANE), F32)
    for x, base in enumerate(ALIBI_BASE):
        ktab = ktab.at[:, x, base:base + ALIBI_LANES].set(-cw)
    qcol = jnp.zeros((heads, DIFF_HEAD_DIM), F32).at[:, ALIBI_LANES:2 * ALIBI_LANES].set(cw)
    return ktab.astype(BF16), qcol.reshape(heads, DIFF_HEAD_DIM, 1)


def _diff_stages(qt_ref, k_ref, vt_ref, pc_ref, pr_ref, shr_ref, shc_ref, kt_ref, qc_ref,
                 lq1_ref, lk1_ref, lq2_ref, lk2_ref, gsub_ref, o_ref, mb_ref, dig_ref, digt_ref,
                 qm_ref, km_ref, s_ref, p_ref, al_ref, m_ref, acc_ref, pv_ref, blk, nblk, lam_init):
    half = DIFF_HEAD_DIM

    @pl.when(pl.program_id(1) == 0)
    def _():
        pcol = pc_ref[...]
        prel = jnp.broadcast_to(pcol - pcol[0:1, :], dig_ref.shape)
        shift = jnp.broadcast_to(shr_ref[...], dig_ref.shape)
        dig_ref[...] = (lax.shift_right_logical(prel, shift) & 255).astype(F32).astype(BF16)
        prow = pr_ref[0]
        prel_t = jnp.broadcast_to(prow - prow[:, 0:1], digt_ref.shape)
        shift_t = jnp.broadcast_to(shc_ref[...], digt_ref.shape)
        digt_ref[...] = (lax.shift_right_logical(prel_t, shift_t) & 255).astype(F32).astype(BF16)

    row = lax.broadcasted_iota(jnp.int32, (half, 1), 0)
    alibi_q = jnp.where(row < ALIBI_LANES, digt_ref[...],
                        jnp.broadcast_to(qc_ref[0], digt_ref.shape).astype(BF16))
    qm_ref[0, :half, :] = qt_ref[:half, :]
    qm_ref[0, half:, :] = alibi_q
    qm_ref[1, :half, :] = alibi_q
    qm_ref[1, half:, :] = qt_ref[half:, :]
    lane = lax.broadcasted_iota(jnp.int32, (1, LANE), 1)
    dig = dig_ref[...]
    k = k_ref[...]
    for x, base in enumerate(ALIBI_BASE):
        own = (lane < half) if x == 0 else (lane >= half)
        k_digit = (lane >= base + ALIBI_LANES) & (lane < base + 2 * ALIBI_LANES)
        km_ref[x] = jnp.where(own, k, jnp.where(k_digit, dig, kt_ref[0, x:x + 1, :]))

    def stage_a(i, t):
        for x in range(2):
            s_ref[x] = _dot(km_ref[x, _blk_slice(t, blk), :], qm_ref[x, :, _blk_slice(i, blk)])

    hb = blk // 2

    def stage_a_diag(i):
        top, bot = _half_slices(i, blk)
        for x in range(2):
            s_ref[x, :hb, :] = _dot(km_ref[x, top, :], qm_ref[x, :, _blk_slice(i, blk)])
            s_ref[x, hb:, hb:] = _dot(km_ref[x, bot, :], qm_ref[x, :, bot])

    def stage_b_diag(i):
        for x in range(2):
            m, p_top, p_bot = _diag_softmax(s_ref[x, :hb, :], s_ref[x, hb:, hb:], mb_ref[...])
            m_ref[x, i] = m
            p_ref[x, :hb, :] = p_top.astype(BF16)
            p_ref[x, hb:, hb:] = p_bot.astype(BF16)

    def stage_c_diag(i):
        top, bot = _half_slices(i, blk)
        for x in range(2):
            acc_ref[x, i] = _diag_pv(vt_ref[:, top], vt_ref[:, bot], p_ref[x, :hb, :], p_ref[x, hb:, hb:])

    def stage_b(i, t):
        for x in range(2):
            s = s_ref[x]
            m_prev = m_ref[x, i]
            m_new = jnp.maximum(m_prev, jnp.max(s, axis=0, keepdims=True))
            alpha = jnp.exp2(m_prev - m_new)
            p = jnp.exp2(s - m_new)
            m_ref[x, i] = m_new
            al_ref[x] = alpha
            p_ref[x] = p.astype(BF16)

    def stage_c(i, t):
        vt = vt_ref[:, _blk_slice(t, blk)]
        for x in range(2):
            pv_ref[x] = _dot(vt, p_ref[x])
            acc_ref[x, i] = al_ref[x] * acc_ref[x, i] + pv_ref[x]

    def finalize():
        lam = (jnp.exp(jnp.sum(lq1_ref[...] * lk1_ref[...], axis=-1, keepdims=True))
               - jnp.exp(jnp.sum(lq2_ref[...] * lk2_ref[...], axis=-1, keepdims=True))
               + lam_init)
        for i in range(nblk):
            a1, a2 = acc_ref[0, i], acc_ref[1, i]
            o_t = (a1[:MLA_V] * (1.0 / a1[MLA_V:MLA_V + 1])
                   - lam * (a2[:MLA_V] * (1.0 / a2[MLA_V:MLA_V + 1])))
            o = o_t.T
            o_ref[_blk_slice(i, blk), :] = (_rms(o, gsub_ref[...]) * (1.0 - lam_init)).astype(BF16)

    return (stage_a, stage_a_diag, stage_b_diag, stage_c_diag, stage_b, stage_c), finalize


def _diff_scratch(seq, blk, nblk):
    return [
        pltpu.VMEM((seq, LANE), BF16),
        pltpu.VMEM((DIFF_HEAD_DIM, seq), BF16),
        pltpu.VMEM((2, LANE, seq), BF16),
        pltpu.VMEM((2, seq, LANE), BF16),
        pltpu.VMEM((2, blk, blk), F32),
        pltpu.VMEM((2, blk, blk), BF16),
        pltpu.VMEM((2, 1, blk), F32),
        pltpu.VMEM((2, nblk, 1, blk), F32),
        pltpu.VMEM((2, nblk, VT_ROWS, blk), F32),
        pltpu.VMEM((2, VT_ROWS, blk), F32),
    ]


N_MLA_IN, N_DIFF_IN = 3, 14
N_MLA_SCRATCH, N_DIFF_SCRATCH = 6, 10


def _attention_body(tab_ref, *refs, blk, nblk, lam_init):
    mla_in, refs = refs[:N_MLA_IN], refs[N_MLA_IN:]
    diff_in, refs = refs[:N_DIFF_IN], refs[N_DIFF_IN:]
    (o_mla_ref, o_diff_ref, mb_ref), refs = refs[:3], refs[3:]
    mla_scr, diff_scr = refs[:N_MLA_SCRATCH], refs[N_MLA_SCRATCH:]
    assert len(diff_scr) == N_DIFF_SCRATCH

    @pl.when((pl.program_id(0) == 0) & (pl.program_id(1) == 0))
    def _():
        _init_mask_bias(mb_ref)

    mla, mla_fin = _mla_stages(*mla_in, o_mla_ref, mb_ref, *mla_scr, blk, nblk)
    diff, diff_fin = _diff_stages(*diff_in, o_diff_ref, mb_ref, *diff_scr, blk, nblk, lam_init)

    _run_pipeline([mla, diff], tab_ref, nblk)
    mla_fin()
    diff_fin()


def _attention(qt, k, vt, big_n, dqt, dvt, pos_icol, pos_irow, slopes_l2e, lq1, lk1, lq2, lk2, g_sub,
               batch, seq, blk, lam_init):
    assert MLA_HEADS == DIFF_HEADS
    nblk = seq // blk
    assert nblk >= 3
    width = DIFF_HEADS * 2 * DIFF_HEAD_DIM
    cb = width // LANE
    k_off = 1 * cb
    k_tab, q_col = _alibi_lane_tables(slopes_l2e)
    digit_k = np.arange(LANE) % DIFF_HEAD_DIM % 3
    shift_row = jnp.asarray((digit_k * 8).reshape(1, LANE), jnp.int32)
    shift_col = jnp.asarray((digit_k[:DIFF_HEAD_DIM] * 8).reshape(DIFF_HEAD_DIM, 1), jnp.int32)
    small = lambda a: pl.BlockSpec(a.shape, lambda b, h: (0,) * a.ndim)
    mla_specs = [
        pl.BlockSpec((QK_PAD, seq), lambda b, h: (h, b)),
        pl.BlockSpec((seq, QK_PAD), lambda b, h: (b, h)),
        pl.BlockSpec((VT_ROWS, seq), lambda b, h: (h, b)),
    ]
    diff_specs = [
        pl.BlockSpec((LANE, seq), lambda b, h: (h, b)),
        pl.BlockSpec((seq, LANE), lambda b, h: (b, k_off + h)),
        pl.BlockSpec((VT_ROWS, seq), lambda b, h: (h, b)),
        pl.BlockSpec((seq, 1), lambda b, h: (b, 0)),
        pl.BlockSpec((1, 1, seq), lambda b, h: (b, 0, 0)),
        small(shift_row), small(shift_col),
        pl.BlockSpec((1, 2, LANE), lambda b, h: (h, 0, 0)),
        pl.BlockSpec((1, DIFF_HEAD_DIM, 1), lambda b, h: (h, 0, 0)),
        small(lq1), small(lk1), small(lq2), small(lk2), small(g_sub),
    ]
    assert len(mla_specs) == N_MLA_IN and len(diff_specs) == N_DIFF_IN
    out_spec = pl.BlockSpec((seq, LANE), lambda b, h: (b, h))
    return pl.pallas_call(
        functools.partial(_attention_body, blk=blk, nblk=nblk, lam_init=lam_init),
        grid=(batch, MLA_HEADS),
        in_specs=[pl.BlockSpec(memory_space=pltpu.SMEM)] + mla_specs + diff_specs,
        out_specs=[out_spec, out_spec],
        out_shape=[jax.ShapeDtypeStruct((batch * seq, MLA_HEADS * MLA_V), BF16),
                   jax.ShapeDtypeStruct((batch * seq, width), BF16)],
        scratch_shapes=([pltpu.VMEM((blk // 2, blk // 2), F32)]
                        + _mla_scratch(blk, nblk) + _diff_scratch(seq, blk, nblk)),
        compiler_params=pltpu.CompilerParams(
            dimension_semantics=("arbitrary", "arbitrary"),
            vmem_limit_bytes=VMEM_LIMIT),
        name="attention",
    )(_pair_table(nblk), qt, k, vt, dqt, big_n, dvt, pos_icol, pos_irow, shift_row, shift_col,
      k_tab, q_col, lq1, lk1, lq2, lk2, g_sub)


def _outproj_body(x_ref, om_ref, od_ref, gm_ref, gd_ref, w_ref, gpost_ref, o_ref):
    gm = gm_ref[...].astype(F32)
    gd = gd_ref[...].astype(F32)
    mm = (om_ref[...].astype(F32) * (gm * jax.nn.sigmoid(gm))).astype(BF16)
    md = (od_ref[...].astype(F32) * (gd * jax.nn.sigmoid(gd))).astype(BF16)
    half = mm.shape[1]
    y = _dot(mm, w_ref[:half, :]) + _dot(md, w_ref[half:, :])
    o_ref[...] = x_ref[...] + _rms(y, gpost_ref[...])


def _outproj(x2, o_mla, o_diff, big, w_out, g_post, tm):
    m, d = x2.shape
    half = o_mla.shape[1]
    gate_diff_blk = (big.shape[1] - half) // half
    return pl.pallas_call(
        _outproj_body,
        grid=(m // tm,),
        in_specs=[
            pl.BlockSpec((tm, d), lambda i: (i, 0)),
            pl.BlockSpec((tm, half), lambda i: (i, 0)),
            pl.BlockSpec((tm, half), lambda i: (i, 0)),
            pl.BlockSpec((tm, half), lambda i: (i, 0)),
            pl.BlockSpec((tm, half), lambda i: (i, gate_diff_blk)),
            pl.BlockSpec(w_out.shape, lambda i: (0, 0)),
            pl.BlockSpec((1, d), lambda i: (0, 0)),
        ],
        out_specs=pl.BlockSpec((tm, d), lambda i: (i, 0)),
        out_shape=jax.ShapeDtypeStruct((m, d), F32),
        compiler_params=pltpu.CompilerParams(
            dimension_semantics=("arbitrary",),
            vmem_limit_bytes=VMEM_LIMIT),
        name="outproj",
    )(x2, o_mla, o_diff, big, big, w_out, g_post)


def kernel(x, positions, g_pre, w_in, g_q_a, w_q_b, g_kv_a, w_kv_b, lambda_q1, lambda_k1,
           lambda_q2, lambda_k2, g_diff_sub, w_out, g_post):
    batch, seq, d = x.shape
    depth = g_pre.shape[0]
    q_rank = w_q_b.shape[1]
    kv_rank = w_kv_b.shape[1]
    half_rope = MLA_ROPE // 2
    lat_end = q_rank + kv_rank
    pe_end = lat_end + MLA_ROPE

    pos_icol = positions.astype(jnp.int32).reshape(batch * seq, 1)
    pos_irow = positions.astype(jnp.int32).reshape(batch, 1, seq)
    freqs = 1.0 / (ROPE_THETA ** (jnp.arange(0, MLA_ROPE, 2, dtype=F32) / MLA_ROPE))
    freq = jnp.tile(freqs, 4).reshape(1, LANE)
    slopes = 2.0 ** (-8.0 * (jnp.arange(DIFF_HEADS, dtype=F32) + 1.0) / DIFF_HEADS)
    slopes_l2e = slopes * LOG2E

    mla_scale = MLA_QK ** -0.5 * LOG2E
    diff_scale = DIFF_HEAD_DIM ** -0.5 * LOG2E
    width = DIFF_HEADS * 2 * DIFF_HEAD_DIM

    x2 = x.reshape(batch * seq, d)
    for l in range(depth):
        lam_init = 0.8 - 0.6 * math.exp(-0.3 * l)
        w_lat, w_n, w_t = _regroup_w_in(w_in, l, lat_end, half_rope, width, rows=REGROUP_ROWS)
        wq = w_q_b[l]
        t1 = wq[:, :, MLA_NOPE:MLA_NOPE + half_rope]
        t2 = wq[:, :, MLA_NOPE + half_rope:]
        w_q = jnp.concatenate([wq[:, :, :MLA_NOPE], t1, t2, t2, t1], axis=-1)
        w_qt = w_q.reshape(q_rank, MLA_HEADS * QK_PAD).T.astype(BF16)
        wkv = w_kv_b[l]
        w_k = wkv[:, :, :MLA_NOPE].reshape(kv_rank, -1).astype(BF16)
        w_vt = wkv[:, :, MLA_NOPE:].reshape(kv_rank, -1).T.astype(BF16)
        gp = g_pre[l].reshape(1, d)

        big_n, dqt, dvt, qt, k, vt = _proj(
            x2, gp, pos_icol, freq, w_n, w_t, w_lat, g_q_a[l].reshape(1, -1), w_qt,
            g_kv_a[l].reshape(1, -1), w_k, w_vt, diff_scale, mla_scale, tm=IN_ROWS, tn=PROJ_COLS)
        o_mla, o_diff = _attention(
            qt, k, vt, big_n, dqt, dvt, pos_icol, pos_irow, slopes_l2e,
            lambda_q1[l].reshape(1, -1), lambda_k1[l].reshape(1, -1),
            lambda_q2[l].reshape(1, -1), lambda_k2[l].reshape(1, -1),
            g_diff_sub[l].reshape(1, -1), batch, seq, blk=ATTN_BLOCK, lam_init=lam_init)
        x2 = _outproj(x2, o_mla, o_diff, big_n, w_out[l].astype(BF16), g_post[l].reshape(1, d), tm=PROJ_ROWS)
    return x2.reshape(batch, seq, d)
```

```python
import functools
import math

import numpy as np
import jax
import jax.numpy as jnp
from jax import lax
from jax.experimental import pallas as pl
from jax.experimental.pallas import tpu as pltpu

F32 = jnp.float32
BF16 = jnp.bfloat16

EPS = 1e-6
LOG2E = 1.4426950408889634
ROPE_THETA = 10000.0

MLA_HEADS = 8
MLA_NOPE = 128
MLA_ROPE = 64
MLA_V = 128
VT_ROWS = MLA_V + 16
MLA_QK = MLA_NOPE + MLA_ROPE
DIFF_HEADS = 8
DIFF_HEAD_DIM = 64
LANE = 128
QK_PAD = 256
ALIBI_LANES = 9
ALIBI_BASE = (DIFF_HEAD_DIM, 0)

VMEM_LIMIT = 56 * 1024 * 1024

IN_ROWS = 256
PROJ_ROWS = 512
PROJ_COLS = 512
ATTN_BLOCK = 512
REGROUP_ROWS = 256


def _rms(xf, g):
    ms = jnp.mean(xf * xf, axis=-1, keepdims=True)
    return xf * lax.rsqrt(ms + EPS) * g


def _dot(a, b):
    return jnp.dot(a, b, preferred_element_type=F32)


def _dot_nt(a, b):
    return lax.dot_general(a, b, (((1,), (1,)), ((), ())), preferred_element_type=F32)


def _exp2_bf16(d):
    return jnp.exp2(d.astype(BF16))


def _store_vt_ext(vt_ref, head, vt_head):
    r = head * VT_ROWS
    vt_ref[r:r + MLA_V, :] = vt_head
    vt_ref[r + MLA_V:r + VT_ROWS, :] = jnp.ones((VT_ROWS - MLA_V, vt_head.shape[1]), vt_head.dtype)


def _regroup_body(wt_ref, lat_ref, n_ref, t_ref, *, lat_end, half_rope, width):
    pe_end = lat_end + 2 * half_rope
    cols = lambda a, b: wt_ref[0, a:b, :]
    u1 = cols(lat_end, lat_end + half_rope)
    u2 = cols(lat_end + half_rope, pe_end)
    lat_ref[:, :lat_end] = cols(0, lat_end).T.astype(BF16)
    lat_ref[:, lat_end:] = jnp.concatenate([u1, u1, u2, u2, u2, u2, u1, u1], axis=0).T.astype(BF16)
    group = lambda n: cols(pe_end + n * width, pe_end + (n + 1) * width).T.astype(BF16)
    for dst, src in enumerate((0, 2, 4)):
        n_ref[:, dst * width:(dst + 1) * width] = group(src)
    for dst, src in enumerate((1, 3)):
        t_ref[:, dst * width:(dst + 1) * width] = group(src)


def _regroup_w_in(w_in, layer, lat_end, half_rope, width, rows):
    _, d, n_in = w_in.shape
    n_lat = lat_end + 8 * half_rope
    w_in_t = jnp.swapaxes(w_in, 1, 2)
    return pl.pallas_call(
        functools.partial(_regroup_body, lat_end=lat_end, half_rope=half_rope, width=width),
        grid=(d // rows,),
        in_specs=[pl.BlockSpec((1, n_in, rows), lambda i: (layer, 0, i))],
        out_specs=[
            pl.BlockSpec((rows, n_lat), lambda i: (i, 0)),
            pl.BlockSpec((rows, 3 * width), lambda i: (i, 0)),
            pl.BlockSpec((rows, 2 * width), lambda i: (i, 0)),
        ],
        out_shape=[
            jax.ShapeDtypeStruct((d, n_lat), BF16),
            jax.ShapeDtypeStruct((d, 3 * width), BF16),
            jax.ShapeDtypeStruct((d, 2 * width), BF16),
        ],
        compiler_params=pltpu.CompilerParams(
            dimension_semantics=("arbitrary",),
            vmem_limit_bytes=VMEM_LIMIT),
        name="regroup_w_in",
    )(w_in_t)


def _wide_projection(h, wn_ref, wt_ref, on_ref, oq_ref, ov_ref, tn, q_scale):
    for c in range(wn_ref.shape[1] // tn):
        cols = slice(c * tn, (c + 1) * tn)
        on_ref[:, cols] = _dot(h, wn_ref[:, cols]).astype(BF16)
    n_q = oq_ref.shape[0] // tn
    heads_per_chunk = tn // MLA_V
    for c in range(wt_ref.shape[1] // tn):
        cols = slice(c * tn, (c + 1) * tn)
        acc = _dot(h, wt_ref[:, cols])
        if c < n_q:
            oq_ref[cols, :] = (acc * q_scale).T.astype(BF16)
        else:
            vt = acc.T.astype(BF16)
            for j in range(heads_per_chunk):
                _store_vt_ext(ov_ref, (c - n_q) * heads_per_chunk + j, vt[j * MLA_V:(j + 1) * MLA_V, :])


def _latent_path(h, pos_ref, freq_ref, wlat_ref, gq_ref, wqt_ref,
                 gkv_ref, wk_ref, wvt_ref, qt_ref, k_ref, vt_ref, q_scale):
    lat = _dot(h, wlat_ref[...])
    ang = pos_ref[...].astype(F32) * freq_ref[...]
    cos = jnp.cos(ang)
    sin = jnp.sin(ang)
    seg = lax.broadcasted_iota(jnp.int32, (1, LANE), 1) // (MLA_ROPE // 2)
    fq = jnp.where(seg == 1, -sin, jnp.where(seg == 3, sin, cos)) * q_scale
    sk = jnp.where(seg < 2, -sin, sin)
    k_ext = (lat[:, 768:896] * cos + lat[:, 896:1024] * sk).astype(BF16)

    c_q = _rms(lat[:, :512], gq_ref[...]).astype(BF16)
    qft = _dot_nt(wqt_ref[...], c_q)
    fqt = fq.T
    c_kv = _rms(lat[:, 512:768], gkv_ref[...]).astype(BF16)
    kf = _dot(c_kv, wk_ref[...])
    vt = _dot_nt(wvt_ref[...], c_kv).astype(BF16)
    for hd in range(MLA_HEADS):
        _store_vt_ext(vt_ref, hd, vt[hd * MLA_V:(hd + 1) * MLA_V, :])
        o = hd * QK_PAD
        qt_ref[o:o + LANE, :] = (qft[o:o + LANE, :] * q_scale).astype(BF16)
        qt_ref[o + LANE:o + QK_PAD, :] = (qft[o + LANE:o + QK_PAD, :] * fqt).astype(BF16)
        k_ref[:, o:o + LANE] = kf[:, hd * LANE:(hd + 1) * LANE].astype(BF16)
        k_ref[:, o + LANE:o + QK_PAD] = k_ext


def _proj_body(x_ref, g_ref, pos_ref, freq_ref, wn_ref, wt_ref, wlat_ref, gq_ref, wqt_ref, gkv_ref,
               wk_ref, wvt_ref, on_ref, oq_ref, ov_ref, qt_ref, k_ref, vt_ref,
               *, tn, diff_scale, mla_scale):
    h = _rms(x_ref[...], g_ref[...]).astype(BF16)
    _latent_path(h, pos_ref, freq_ref, wlat_ref, gq_ref, wqt_ref, gkv_ref, wk_ref, wvt_ref,
                 qt_ref, k_ref, vt_ref, mla_scale)
    _wide_projection(h, wn_ref, wt_ref, on_ref, oq_ref, ov_ref, tn, diff_scale)


def _proj(x2, g_pre, pos_col, freq, w_n, w_t, w_lat, g_q, w_qt, g_kv, w_k, w_vt,
          diff_scale, mla_scale, tm, tn):
    m, d = x2.shape
    width = w_t.shape[1] // 2
    nq = MLA_HEADS * QK_PAD
    nv = MLA_HEADS * VT_ROWS
    assert width // MLA_V == MLA_HEADS
    resident = lambda a: pl.BlockSpec(a.shape, lambda i: (0,) * a.ndim, pipeline_mode=pl.Buffered(1))
    rows = lambda n: pl.BlockSpec((tm, n), lambda i: (i, 0))
    cols = lambda n: pl.BlockSpec((n, tm), lambda i: (0, i))
    return pl.pallas_call(
        functools.partial(_proj_body, tn=tn, diff_scale=diff_scale, mla_scale=mla_scale),
        grid=(m // tm,),
        in_specs=[rows(d), resident(g_pre), rows(1), resident(freq)]
                 + [resident(a) for a in (w_n, w_t, w_lat, g_q, w_qt, g_kv, w_k, w_vt)],
        out_specs=[rows(w_n.shape[1]), cols(width), cols(nv), cols(nq), rows(nq), cols(nv)],
        out_shape=[
            jax.ShapeDtypeStruct((m, w_n.shape[1]), BF16),
            jax.ShapeDtypeStruct((width, m), BF16),
            jax.ShapeDtypeStruct((nv, m), BF16),
            jax.ShapeDtypeStruct((nq, m), BF16),
            jax.ShapeDtypeStruct((m, nq), BF16),
            jax.ShapeDtypeStruct((nv, m), BF16),
        ],
        compiler_params=pltpu.CompilerParams(
            dimension_semantics=("arbitrary",),
            vmem_limit_bytes=VMEM_LIMIT),
        name="proj",
    )(x2, g_pre, pos_col, freq, w_n, w_t, w_lat, g_q, w_qt, g_kv, w_k, w_vt)


def _blk_slice(idx, blk):
    if isinstance(idx, int):
        return slice(idx * blk, (idx + 1) * blk)
    return pl.ds(pl.multiple_of(idx * blk, blk), blk)


def _offdiag_pairs(nblk):
    return [(i, t) for i in range(1, nblk) for t in range(i)]


def _pair_table(nblk):
    return jnp.asarray(np.array(_offdiag_pairs(nblk), dtype=np.int32).T)


def _half_slices(idx, blk):
    half = blk // 2
    if isinstance(idx, int):
        return slice(idx * blk, idx * blk + half), slice(idx * blk + half, (idx + 1) * blk)
    start = pl.multiple_of(idx * blk, blk)
    return pl.ds(start, half), pl.ds(pl.multiple_of(start + half, half), half)


def _init_mask_bias(mb_ref):
    kv = lax.broadcasted_iota(jnp.int32, mb_ref.shape, 0)
    qi = lax.broadcasted_iota(jnp.int32, mb_ref.shape, 1)
    mb_ref[...] = jnp.where(kv <= qi, 0.0, -jnp.inf).astype(F32)


def _diag_softmax(s_top, s_bot, mb):
    h = s_bot.shape[0]
    a0 = s_top[:, :h] + mb
    a1 = s_top[:, h:]
    b1 = s_bot + mb
    m0 = jnp.max(a0, axis=0, keepdims=True)
    m1 = jnp.maximum(jnp.max(a1, axis=0, keepdims=True), jnp.max(b1, axis=0, keepdims=True))
    p0 = _exp2_bf16(a0 - m0)
    pa1 = _exp2_bf16(a1 - m1)
    pb1 = _exp2_bf16(b1 - m1)
    return jnp.concatenate([m0, m1], axis=1), jnp.concatenate([p0, pa1], axis=1), pb1


def _diag_pv(vt_top, vt_bot, p_top, p_bot):
    h = p_bot.shape[0]
    a = _dot(vt_top, p_top)
    b = _dot(vt_bot, p_bot)
    return jnp.concatenate([a[:, :h], a[:, h:] + b], axis=1)


def _run_pipeline(heads, tab_ref, nblk):
    A, A_DIAG, B_DIAG, C_DIAG, B, C = range(6)
    pairs = _offdiag_pairs(nblk)
    npairs = len(pairs)
    assert (nblk - 2) % 2 == 0 and (npairs - 2) % 2 == 0

    def step(c=None, b=None, a=None):
        for call in (c, b, a):
            if call is not None:
                for stages in heads:
                    stages[call[0]](*call[1])

    step(a=(A_DIAG, (0,)))
    step(b=(B_DIAG, (0,)), a=(A_DIAG, (1,)))

    def diag_loop(u, carry):
        for d in (1, 2):
            g = d + 2 * u
            step(c=(C_DIAG, (g - 1,)), b=(B_DIAG, (g,)), a=(A_DIAG, (g + 1,)))
        return carry

    lax.fori_loop(0, (nblk - 2) // 2, diag_loop, 0)

    step(c=(C_DIAG, (nblk - 2,)), b=(B_DIAG, (nblk - 1,)), a=(A, pairs[0]))
    step(c=(C_DIAG, (nblk - 1,)), b=(B, pairs[0]), a=(A, pairs[1]))

    def off_loop(u, carry):
        pair = lambda f: (tab_ref[0, f], tab_ref[1, f])
        for d in (1, 2):
            f = d + 2 * u
            step(c=(C, pair(f - 1)), b=(B, pair(f)), a=(A, pair(f + 1)))
        return carry

    lax.fori_loop(0, (npairs - 2) // 2, off_loop, 0)

    step(c=(C, pairs[npairs - 2]), b=(B, pairs[npairs - 1]))
    step(c=(C, pairs[npairs - 1]))


def _mla_stages(qt_ref, k_ref, vt_ref, o_ref, mb_ref, s_ref, p_ref, al_ref, m_ref, acc_ref, blk, nblk):
    half = blk // 2

    def stage_a(i, t):
        s_ref[...] = _dot(k_ref[_blk_slice(t, blk), :], qt_ref[:, _blk_slice(i, blk)])

    def stage_a_diag(i):
        top, bot = _half_slices(i, blk)
        s_ref[:half, :] = _dot(k_ref[top, :], qt_ref[:, _blk_slice(i, blk)])
        s_ref[half:, half:] = _dot(k_ref[bot, :], qt_ref[:, bot])

    def stage_b_diag(i):
        m, p_top, p_bot = _diag_softmax(s_ref[:half, :], s_ref[half:, half:], mb_ref[...])
        m_ref[i] = m
        p_ref[:half, :] = p_top.astype(BF16)
        p_ref[half:, half:] = p_bot.astype(BF16)

    def stage_c_diag(i):
        top, bot = _half_slices(i, blk)
        acc_ref[i] = _diag_pv(vt_ref[:, top], vt_ref[:, bot], p_ref[:half, :], p_ref[half:, half:])

    def stage_b(i, t):
        s = s_ref[...]
        m_prev = m_ref[i]
        m_new = jnp.maximum(m_prev, jnp.max(s, axis=0, keepdims=True))
        alpha = jnp.exp2(m_prev - m_new)
        p = _exp2_bf16(s - m_new)
        m_ref[i] = m_new
        al_ref[...] = alpha
        p_ref[...] = p.astype(BF16)

    def stage_c(i, t):
        acc_ref[i] = al_ref[...] * acc_ref[i] + _dot(vt_ref[:, _blk_slice(t, blk)], p_ref[...])

    def finalize():
        for i in range(nblk):
            acc = acc_ref[i]
            o_t = acc[:MLA_V] * (1.0 / acc[MLA_V:MLA_V + 1])
            o_ref[_blk_slice(i, blk), :] = o_t.T.astype(BF16)

    return (stage_a, stage_a_diag, stage_b_diag, stage_c_diag, stage_b, stage_c), finalize


def _mla_scratch(blk, nblk):
    return [
        pltpu.VMEM((blk, blk), F32),
        pltpu.VMEM((blk, blk), BF16),
        pltpu.VMEM((1, blk), F32),
        pltpu.VMEM((nblk, 1, blk), F32),
        pltpu.VMEM((nblk, VT_ROWS, blk), F32),
    ]


def _alibi_lane_tables(slopes_l2e):
    c1 = slopes_l2e.astype(BF16).astype(F32)
    c2 = (slopes_l2e - c1).astype(BF16).astype(F32)
    c3 = (slopes_l2e - c1 - c2).astype(BF16).astype(F32)
    cw = (jnp.stack([c1, c2, c3], axis=1)[:, :, None]
          * jnp.asarray([1.0, 256.0, 65536.0], F32)[None, None, :]).reshape(-1, ALIBI_LANES)
    heads = slopes_l2e.shape[0]
    ktab = jnp.zeros((heads, 2, LANE), F32)
    for x, base in enumerate(ALIBI_BASE):
        ktab = ktab.at[:, x, base:base + ALIBI_LANES].set(-cw)
    qcol = jnp.zeros((heads, DIFF_HEAD_DIM), F32).at[:, ALIBI_LANES:2 * ALIBI_LANES].set(cw)
    return ktab.astype(BF16), qcol.reshape(heads, DIFF_HEAD_DIM, 1)


def _diff_stages(qt_ref, k_ref, vt_ref, pc_ref, pr_ref, shr_ref, shc_ref, kt_ref, qc_ref,
                 lq1_ref, lk1_ref, lq2_ref, lk2_ref, gsub_ref, o_ref, mb_ref, dig_ref, digt_ref,
                 qm_ref, km_ref, s_ref, p_ref, al_ref, m_ref, acc_ref, blk, nblk, lam_init):
    half = DIFF_HEAD_DIM

    @pl.when(pl.program_id(1) == 0)
    def _():
        pcol = pc_ref[...]
        prel = jnp.broadcast_to(pcol - pcol[0:1, :], dig_ref.shape)
        shift = jnp.broadcast_to(shr_ref[...], dig_ref.shape)
        dig_ref[...] = (lax.shift_right_logical(prel, shift) & 255).astype(F32).astype(BF16)
        prow = pr_ref[0]
        prel_t = jnp.broadcast_to(prow - prow[:, 0:1], digt_ref.shape)
        shift_t = jnp.broadcast_to(shc_ref[...], digt_ref.shape)
        digt_ref[...] = (lax.shift_right_logical(prel_t, shift_t) & 255).astype(F32).astype(BF16)

    row = lax.broadcasted_iota(jnp.int32, (half, 1), 0)
    alibi_q = jnp.where(row < ALIBI_LANES, digt_ref[...],
                        jnp.broadcast_to(qc_ref[0], digt_ref.shape).astype(BF16))
    qm_ref[0, :half, :] = qt_ref[:half, :]
    qm_ref[0, half:, :] = alibi_q
    qm_ref[1, :half, :] = alibi_q
    qm_ref[1, half:, :] = qt_ref[half:, :]
    lane = lax.broadcasted_iota(jnp.int32, (1, LANE), 1)
    dig = dig_ref[...]
    k = k_ref[...]
    for x, base in enumerate(ALIBI_BASE):
        own = (lane < half) if x == 0 else (lane >= half)
        k_digit = (lane >= base + ALIBI_LANES) & (lane < base + 2 * ALIBI_LANES)
        km_ref[x] = jnp.where(own, k, jnp.where(k_digit, dig, kt_ref[0, x:x + 1, :]))

    def stage_a(i, t):
        for x in range(2):
            s_ref[x] = _dot(km_ref[x, _blk_slice(t, blk), :], qm_ref[x, :, _blk_slice(i, blk)])

    hb = blk // 2

    def stage_a_diag(i):
        top, bot = _half_slices(i, blk)
        for x in range(2):
            s_ref[x, :hb, :] = _dot(km_ref[x, top, :], qm_ref[x, :, _blk_slice(i, blk)])
            s_ref[x, hb:, hb:] = _dot(km_ref[x, bot, :], qm_ref[x, :, bot])

    def stage_b_diag(i):
        for x in range(2):
            m, p_top, p_bot = _diag_softmax(s_ref[x, :hb, :], s_ref[x, hb:, hb:], mb_ref[...])
            m_ref[x, i] = m
            p_ref[x, :hb, :] = p_top.astype(BF16)
            p_ref[x, hb:, hb:] = p_bot.astype(BF16)

    def stage_c_diag(i):
        top, bot = _half_slices(i, blk)
        for x in range(2):
            acc_ref[x, i] = _diag_pv(vt_ref[:, top], vt_ref[:, bot], p_ref[x, :hb, :], p_ref[x, hb:, hb:])

    def stage_b(i, t):
        for x in range(2):
            s = s_ref[x]
            m_prev = m_ref[x, i]
            m_new = jnp.maximum(m_prev, jnp.max(s, axis=0, keepdims=True))
            alpha = jnp.exp2(m_prev - m_new)
            p = _exp2_bf16(s - m_new)
            m_ref[x, i] = m_new
            al_ref[x] = alpha
            p_ref[x] = p.astype(BF16)

    def stage_c(i, t):
        vt = vt_ref[:, _blk_slice(t, blk)]
        for x in range(2):
            acc_ref[x, i] = al_ref[x] * acc_ref[x, i] + _dot(vt, p_ref[x])

    def finalize():
        lam = (jnp.exp(jnp.sum(lq1_ref[...] * lk1_ref[...], axis=-1, keepdims=True))
               - jnp.exp(jnp.sum(lq2_ref[...] * lk2_ref[...], axis=-1, keepdims=True))
               + lam_init)
        for i in range(nblk):
            a1, a2 = acc_ref[0, i], acc_ref[1, i]
            o_t = (a1[:MLA_V] * (1.0 / a1[MLA_V:MLA_V + 1])
                   - lam * (a2[:MLA_V] * (1.0 / a2[MLA_V:MLA_V + 1])))
            o = o_t.T
            o_ref[_blk_slice(i, blk), :] = (_rms(o, gsub_ref[...]) * (1.0 - lam_init)).astype(BF16)

    return (stage_a, stage_a_diag, stage_b_diag, stage_c_diag, stage_b, stage_c), finalize


def _diff_scratch(seq, blk, nblk):
    return [
        pltpu.VMEM((seq, LANE), BF16),
        pltpu.VMEM((DIFF_HEAD_DIM, seq), BF16),
        pltpu.VMEM((2, LANE, seq), BF16),
        pltpu.VMEM((2, seq, LANE), BF16),
        pltpu.VMEM((2, blk, blk), F32),
        pltpu.VMEM((2, blk, blk), BF16),
        pltpu.VMEM((2, 1, blk), F32),
        pltpu.VMEM((2, nblk, 1, blk), F32),
        pltpu.VMEM((2, nblk, VT_ROWS, blk), F32),
    ]


N_MLA_IN, N_DIFF_IN = 3, 14
N_MLA_SCRATCH, N_DIFF_SCRATCH = 5, 9


def _attention_body(tab_ref, *refs, blk, nblk, lam_init):
    mla_in, refs = refs[:N_MLA_IN], refs[N_MLA_IN:]
    diff_in, refs = refs[:N_DIFF_IN], refs[N_DIFF_IN:]
    (o_mla_ref, o_diff_ref, mb_ref), refs = refs[:3], refs[3:]
    mla_scr, diff_scr = refs[:N_MLA_SCRATCH], refs[N_MLA_SCRATCH:]
    assert len(diff_scr) == N_DIFF_SCRATCH

    @pl.when((pl.program_id(0) == 0) & (pl.program_id(1) == 0))
    def _():
        _init_mask_bias(mb_ref)

    mla, mla_fin = _mla_stages(*mla_in, o_mla_ref, mb_ref, *mla_scr, blk, nblk)
    diff, diff_fin = _diff_stages(*diff_in, o_diff_ref, mb_ref, *diff_scr, blk, nblk, lam_init)

    _run_pipeline([mla, diff], tab_ref, nblk)
    mla_fin()
    diff_fin()


def _attention(qt, k, vt, big_n, dqt, dvt, pos_icol, pos_irow, slopes_l2e, lq1, lk1, lq2, lk2, g_sub,
               batch, seq, blk, lam_init):
    assert MLA_HEADS == DIFF_HEADS
    nblk = seq // blk
    assert nblk >= 3
    width = DIFF_HEADS * 2 * DIFF_HEAD_DIM
    cb = width // LANE
    k_off = 1 * cb
    k_tab, q_col = _alibi_lane_tables(slopes_l2e)
    digit_k = np.arange(LANE) % DIFF_HEAD_DIM % 3
    shift_row = jnp.asarray((digit_k * 8).reshape(1, LANE), jnp.int32)
    shift_col = jnp.asarray((digit_k[:DIFF_HEAD_DIM] * 8).reshape(DIFF_HEAD_DIM, 1), jnp.int32)
    small = lambda a: pl.BlockSpec(a.shape, lambda b, h: (0,) * a.ndim)
    mla_specs = [
        pl.BlockSpec((QK_PAD, seq), lambda b, h: (h, b)),
        pl.BlockSpec((seq, QK_PAD), lambda b, h: (b, h)),
        pl.BlockSpec((VT_ROWS, seq), lambda b, h: (h, b)),
    ]
    diff_specs = [
        pl.BlockSpec((LANE, seq), lambda b, h: (h, b)),
        pl.BlockSpec((seq, LANE), lambda b, h: (b, k_off + h)),
        pl.BlockSpec((VT_ROWS, seq), lambda b, h: (h, b)),
        pl.BlockSpec((seq, 1), lambda b, h: (b, 0)),
        pl.BlockSpec((1, 1, seq), lambda b, h: (b, 0, 0)),
        small(shift_row), small(shift_col),
        pl.BlockSpec((1, 2, LANE), lambda b, h: (h, 0, 0)),
        pl.BlockSpec((1, DIFF_HEAD_DIM, 1), lambda b, h: (h, 0, 0)),
        small(lq1), small(lk1), small(lq2), small(lk2), small(g_sub),
    ]
    assert len(mla_specs) == N_MLA_IN and len(diff_specs) == N_DIFF_IN
    out_spec = pl.BlockSpec((seq, LANE), lambda b, h: (b, h))
    return pl.pallas_call(
        functools.partial(_attention_body, blk=blk, nblk=nblk, lam_init=lam_init),
        grid=(batch, MLA_HEADS),
        in_specs=[pl.BlockSpec(memory_space=pltpu.SMEM)] + mla_specs + diff_specs,
        out_specs=[out_spec, out_spec],
        out_shape=[jax.ShapeDtypeStruct((batch * seq, MLA_HEADS * MLA_V), BF16),
                   jax.ShapeDtypeStruct((batch * seq, width), BF16)],
        scratch_shapes=([pltpu.VMEM((blk // 2, blk // 2), F32)]
                        + _mla_scratch(blk, nblk) + _diff_scratch(seq, blk, nblk)),
        compiler_params=pltpu.CompilerParams(
            dimension_semantics=("arbitrary", "arbitrary"),
            vmem_limit_bytes=VMEM_LIMIT),
        name="attention",
    )(_pair_table(nblk), qt, k, vt, dqt, big_n, dvt, pos_icol, pos_irow, shift_row, shift_col,
      k_tab, q_col, lq1, lk1, lq2, lk2, g_sub)


def _outproj_body(x_ref, om_ref, od_ref, gm_ref, gd_ref, w_ref, gpost_ref, o_ref):
    gm = gm_ref[...].astype(F32)
    gd = gd_ref[...].astype(F32)
    mm = (om_ref[...].astype(F32) * (gm * jax.nn.sigmoid(gm))).astype(BF16)
    md = (od_ref[...].astype(F32) * (gd * jax.nn.sigmoid(gd))).astype(BF16)
    half = mm.shape[1]
    y = _dot(mm, w_ref[:half, :]) + _dot(md, w_ref[half:, :])
    o_ref[...] = x_ref[...] + _rms(y, gpost_ref[...])


def _outproj(x2, o_mla, o_diff, big, w_out, g_post, tm):
    m, d = x2.shape
    half = o_mla.shape[1]
    gate_diff_blk = (big.shape[1] - half) // half
    return pl.pallas_call(
        _outproj_body,
        grid=(m // tm,),
        in_specs=[
            pl.BlockSpec((tm, d), lambda i: (i, 0)),
            pl.BlockSpec((tm, half), lambda i: (i, 0)),
            pl.BlockSpec((tm, half), lambda i: (i, 0)),
            pl.BlockSpec((tm, half), lambda i: (i, 0)),
            pl.BlockSpec((tm, half), lambda i: (i, gate_diff_blk)),
            pl.BlockSpec(w_out.shape, lambda i: (0, 0)),
            pl.BlockSpec((1, d), lambda i: (0, 0)),
        ],
        out_specs=pl.BlockSpec((tm, d), lambda i: (i, 0)),
        out_shape=jax.ShapeDtypeStruct((m, d), F32),
        compiler_params=pltpu.CompilerParams(
            dimension_semantics=("arbitrary",),
            vmem_limit_bytes=VMEM_LIMIT),
        name="outproj",
    )(x2, o_mla, o_diff, big, big, w_out, g_post)


def kernel(x, positions, g_pre, w_in, g_q_a, w_q_b, g_kv_a, w_kv_b, lambda_q1, lambda_k1,
           lambda_q2, lambda_k2, g_diff_sub, w_out, g_post):
    batch, seq, d = x.shape
    depth = g_pre.shape[0]
    q_rank = w_q_b.shape[1]
    kv_rank = w_kv_b.shape[1]
    half_rope = MLA_ROPE // 2
    lat_end = q_rank + kv_rank
    pe_end = lat_end + MLA_ROPE

    pos_icol = positions.astype(jnp.int32).reshape(batch * seq, 1)
    pos_irow = positions.astype(jnp.int32).reshape(batch, 1, seq)
    freqs = 1.0 / (ROPE_THETA ** (jnp.arange(0, MLA_ROPE, 2, dtype=F32) / MLA_ROPE))
    freq = jnp.tile(freqs, 4).reshape(1, LANE)
    slopes = 2.0 ** (-8.0 * (jnp.arange(DIFF_HEADS, dtype=F32) + 1.0) / DIFF_HEADS)
    slopes_l2e = slopes * LOG2E

    mla_scale = MLA_QK ** -0.5 * LOG2E
    diff_scale = DIFF_HEAD_DIM ** -0.5 * LOG2E
    width = DIFF_HEADS * 2 * DIFF_HEAD_DIM

    x2 = x.reshape(batch * seq, d)
    for l in range(depth):
        lam_init = 0.8 - 0.6 * math.exp(-0.3 * l)
        w_lat, w_n, w_t = _regroup_w_in(w_in, l, lat_end, half_rope, width, rows=REGROUP_ROWS)
        wq = w_q_b[l]
        t1 = wq[:, :, MLA_NOPE:MLA_NOPE + half_rope]
        t2 = wq[:, :, MLA_NOPE + half_rope:]
        w_q = jnp.concatenate([wq[:, :, :MLA_NOPE], t1, t2, t2, t1], axis=-1)
        w_qt = w_q.reshape(q_rank, MLA_HEADS * QK_PAD).T.astype(BF16)
        wkv = w_kv_b[l]
        w_k = wkv[:, :, :MLA_NOPE].reshape(kv_rank, -1).astype(BF16)
        w_vt = wkv[:, :, MLA_NOPE:].reshape(kv_rank, -1).T.astype(BF16)
        gp = g_pre[l].reshape(1, d)

        big_n, dqt, dvt, qt, k, vt = _proj(
            x2, gp, pos_icol, freq, w_n, w_t, w_lat, g_q_a[l].reshape(1, -1), w_qt,
            g_kv_a[l].reshape(1, -1), w_k, w_vt, diff_scale, mla_scale, tm=IN_ROWS, tn=PROJ_COLS)
        o_mla, o_diff = _attention(
            qt, k, vt, big_n, dqt, dvt, pos_icol, pos_irow, slopes_l2e,
            lambda_q1[l].reshape(1, -1), lambda_k1[l].reshape(1, -1),
            lambda_q2[l].reshape(1, -1), lambda_k2[l].reshape(1, -1),
            g_diff_sub[l].reshape(1, -1), batch, seq, blk=ATTN_BLOCK, lam_init=lam_init)
        x2 = _outproj(x2, o_mla, o_diff, big_n, w_out[l].astype(BF16), g_post[l].reshape(1, d), tm=PROJ_ROWS)
    return x2.reshape(batch, seq, d)
```

```python
import functools
import math

import numpy as np
import jax
import jax.numpy as jnp
from jax import lax
from jax.experimental import pallas as pl
from jax.experimental.pallas import tpu as pltpu

F32 = jnp.float32
BF16 = jnp.bfloat16

EPS = 1e-6
LOG2E = 1.4426950408889634
ROPE_THETA = 10000.0

MLA_HEADS = 8
MLA_NOPE = 128
MLA_ROPE = 64
MLA_V = 128
VT_ROWS = MLA_V + 16
MLA_QK = MLA_NOPE + MLA_ROPE
DIFF_HEADS = 8
DIFF_HEAD_DIM = 64
LANE = 128
QK_PAD = 256
ALIBI_LANES = 9
ALIBI_BASE = (DIFF_HEAD_DIM, 0)

VMEM_LIMIT = 56 * 1024 * 1024

PROJ_ROWS = 512
PROJ_COLS = 512
ATTN_BLOCK = 512
REGROUP_ROWS = 256


def _rms(xf, g):
    ms = jnp.mean(xf * xf, axis=-1, keepdims=True)
    return xf * lax.rsqrt(ms + EPS) * g


def _dot(a, b):
    return jnp.dot(a, b, preferred_element_type=F32)


def _dot_nt(a, b):
    return lax.dot_general(a, b, (((1,), (1,)), ((), ())), preferred_element_type=F32)


def _store_vt_ext(vt_ref, head, vt_head):
    r = head * VT_ROWS
    vt_ref[r:r + MLA_V, :] = vt_head
    vt_ref[r + MLA_V:r + VT_ROWS, :] = jnp.ones((VT_ROWS - MLA_V, vt_head.shape[1]), vt_head.dtype)


def _regroup_body(wt_ref, lat_ref, n_ref, t_ref, *, lat_end, half_rope, width):
    pe_end = lat_end + 2 * half_rope
    cols = lambda a, b: wt_ref[0, a:b, :]
    u1 = cols(lat_end, lat_end + half_rope)
    u2 = cols(lat_end + half_rope, pe_end)
    lat_ref[:, :lat_end] = cols(0, lat_end).T.astype(BF16)
    lat_ref[:, lat_end:] = jnp.concatenate([u1, u1, u2, u2, u2, u2, u1, u1], axis=0).T.astype(BF16)
    group = lambda n: cols(pe_end + n * width, pe_end + (n + 1) * width).T.astype(BF16)
    for dst, src in enumerate((0, 2, 4)):
        n_ref[:, dst * width:(dst + 1) * width] = group(src)
    for dst, src in enumerate((1, 3)):
        t_ref[:, dst * width:(dst + 1) * width] = group(src)


def _regroup_w_in(w_in, layer, lat_end, half_rope, width, rows):
    _, d, n_in = w_in.shape
    n_lat = lat_end + 8 * half_rope
    w_in_t = jnp.swapaxes(w_in, 1, 2)
    return pl.pallas_call(
        functools.partial(_regroup_body, lat_end=lat_end, half_rope=half_rope, width=width),
        grid=(d // rows,),
        in_specs=[pl.BlockSpec((1, n_in, rows), lambda i: (layer, 0, i))],
        out_specs=[
            pl.BlockSpec((rows, n_lat), lambda i: (i, 0)),
            pl.BlockSpec((rows, 3 * width), lambda i: (i, 0)),
            pl.BlockSpec((rows, 2 * width), lambda i: (i, 0)),
        ],
        out_shape=[
            jax.ShapeDtypeStruct((d, n_lat), BF16),
            jax.ShapeDtypeStruct((d, 3 * width), BF16),
            jax.ShapeDtypeStruct((d, 2 * width), BF16),
        ],
        compiler_params=pltpu.CompilerParams(
            dimension_semantics=("arbitrary",),
            vmem_limit_bytes=VMEM_LIMIT),
        name="regroup_w_in",
    )(w_in_t)


def _inproj_body(x_ref, g_ref, wn_ref, wt_ref, h_ref, on_ref, oq_ref, ov_ref, *, tn, q_scale):
    h = _rms(x_ref[...], g_ref[...]).astype(BF16)
    h_ref[...] = h
    for c in range(wn_ref.shape[1] // tn):
        cols = slice(c * tn, (c + 1) * tn)
        on_ref[:, cols] = _dot(h, wn_ref[:, cols]).astype(BF16)
    n_q = oq_ref.shape[0] // tn
    heads_per_chunk = tn // MLA_V
    for c in range(wt_ref.shape[1] // tn):
        cols = slice(c * tn, (c + 1) * tn)
        acc = _dot(h, wt_ref[:, cols])
        if c < n_q:
            oq_ref[cols, :] = (acc * q_scale).T.astype(BF16)
        else:
            vt = acc.T.astype(BF16)
            for j in range(heads_per_chunk):
                _store_vt_ext(ov_ref, (c - n_q) * heads_per_chunk + j, vt[j * MLA_V:(j + 1) * MLA_V, :])


def _inproj(x2, g_pre, w_n, w_t, q_scale, tm, tn):
    m, d = x2.shape
    width = w_t.shape[1] // 2
    nv = width // MLA_V * VT_ROWS
    resident = lambda a: pl.BlockSpec(a.shape, lambda i: (0,) * a.ndim, pipeline_mode=pl.Buffered(1))
    return pl.pallas_call(
        functools.partial(_inproj_body, tn=tn, q_scale=q_scale),
        grid=(m // tm,),
        in_specs=[
            pl.BlockSpec((tm, d), lambda i: (i, 0)),
            pl.BlockSpec((1, d), lambda i: (0, 0)),
            resident(w_n),
            resident(w_t),
        ],
        out_specs=[
            pl.BlockSpec((tm, d), lambda i: (i, 0)),
            pl.BlockSpec((tm, w_n.shape[1]), lambda i: (i, 0)),
            pl.BlockSpec((width, tm), lambda i: (0, i)),
            pl.BlockSpec((nv, tm), lambda i: (0, i)),
        ],
        out_shape=[
            jax.ShapeDtypeStruct((m, d), BF16),
            jax.ShapeDtypeStruct((m, w_n.shape[1]), BF16),
            jax.ShapeDtypeStruct((width, m), BF16),
            jax.ShapeDtypeStruct((nv, m), BF16),
        ],
        compiler_params=pltpu.CompilerParams(
            dimension_semantics=("arbitrary",),
            vmem_limit_bytes=VMEM_LIMIT),
        name="inproj",
    )(x2, g_pre, w_n, w_t)


def _latent_body(h_ref, pos_ref, freq_ref, wlat_ref, gq_ref, wqt_ref,
                 gkv_ref, wk_ref, wvt_ref, qt_ref, k_ref, vt_ref, *, q_scale):
    lat = _dot(h_ref[...], wlat_ref[...])
    ang = pos_ref[...].astype(F32) * freq_ref[...]
    cos = jnp.cos(ang)
    sin = jnp.sin(ang)
    seg = lax.broadcasted_iota(jnp.int32, (1, LANE), 1) // (MLA_ROPE // 2)
    fq = jnp.where(seg == 1, -sin, jnp.where(seg == 3, sin, cos)) * q_scale
    sk = jnp.where(seg < 2, -sin, sin)
    k_ext = (lat[:, 768:896] * cos + lat[:, 896:1024] * sk).astype(BF16)

    c_q = _rms(lat[:, :512], gq_ref[...]).astype(BF16)
    qft = _dot_nt(wqt_ref[...], c_q)
    fqt = fq.T
    c_kv = _rms(lat[:, 512:768], gkv_ref[...]).astype(BF16)
    kf = _dot(c_kv, wk_ref[...])
    vt = _dot_nt(wvt_ref[...], c_kv).astype(BF16)
    for hd in range(MLA_HEADS):
        _store_vt_ext(vt_ref, hd, vt[hd * MLA_V:(hd + 1) * MLA_V, :])
        o = hd * QK_PAD
        qt_ref[o:o + LANE, :] = (qft[o:o + LANE, :] * q_scale).astype(BF16)
        qt_ref[o + LANE:o + QK_PAD, :] = (qft[o + LANE:o + QK_PAD, :] * fqt).astype(BF16)
        k_ref[:, o:o + LANE] = kf[:, hd * LANE:(hd + 1) * LANE].astype(BF16)
        k_ref[:, o + LANE:o + QK_PAD] = k_ext


def _latent(h, pos_col, freq, w_lat, g_q, w_qt, g_kv, w_k, w_vt, tm, q_scale):
    m, d = h.shape
    nq = MLA_HEADS * QK_PAD
    nv = MLA_HEADS * VT_ROWS
    full = lambda a: pl.BlockSpec(a.shape, lambda i: (0,) * a.ndim)
    return pl.pallas_call(
        functools.partial(_latent_body, q_scale=q_scale),
        grid=(m // tm,),
        in_specs=[
            pl.BlockSpec((tm, d), lambda i: (i, 0)),
            pl.BlockSpec((tm, 1), lambda i: (i, 0)),
            full(freq), full(w_lat), full(g_q), full(w_qt), full(g_kv), full(w_k),
            full(w_vt),
        ],
        out_specs=[
            pl.BlockSpec((nq, tm), lambda i: (0, i)),
            pl.BlockSpec((tm, nq), lambda i: (i, 0)),
            pl.BlockSpec((nv, tm), lambda i: (0, i)),
        ],
        out_shape=[
            jax.ShapeDtypeStruct((nq, m), BF16),
            jax.ShapeDtypeStruct((m, nq), BF16),
            jax.ShapeDtypeStruct((nv, m), BF16),
        ],
        compiler_params=pltpu.CompilerParams(
            dimension_semantics=("arbitrary",),
            vmem_limit_bytes=VMEM_LIMIT),
        name="latent",
    )(h, pos_col, freq, w_lat, g_q, w_qt, g_kv, w_k, w_vt)


def _blk_slice(idx, blk):
    if isinstance(idx, int):
        return slice(idx * blk, (idx + 1) * blk)
    return pl.ds(pl.multiple_of(idx * blk, blk), blk)


def _offdiag_pairs(nblk):
    return [(i, t) for i in range(1, nblk) for t in range(i)]


def _pair_table(nblk):
    return jnp.asarray(np.array(_offdiag_pairs(nblk), dtype=np.int32).T)


def _half_slices(idx, blk):
    half = blk // 2
    if isinstance(idx, int):
        return slice(idx * blk, idx * blk + half), slice(idx * blk + half, (idx + 1) * blk)
    start = pl.multiple_of(idx * blk, blk)
    return pl.ds(start, half), pl.ds(pl.multiple_of(start + half, half), half)


def _init_mask_bias(mb_ref):
    kv = lax.broadcasted_iota(jnp.int32, mb_ref.shape, 0)
    qi = lax.broadcasted_iota(jnp.int32, mb_ref.shape, 1)
    mb_ref[...] = jnp.where(kv <= qi, 0.0, -jnp.inf).astype(F32)


def _diag_softmax(s_top, s_bot, mb):
    h = s_bot.shape[0]
    a0 = s_top[:, :h] + mb
    a1 = s_top[:, h:]
    b1 = s_bot + mb
    m0 = jnp.max(a0, axis=0, keepdims=True)
    m1 = jnp.maximum(jnp.max(a1, axis=0, keepdims=True), jnp.max(b1, axis=0, keepdims=True))
    p0 = jnp.exp2(a0 - m0)
    pa1 = jnp.exp2(a1 - m1)
    pb1 = jnp.exp2(b1 - m1)
    return jnp.concatenate([m0, m1], axis=1), jnp.concatenate([p0, pa1], axis=1), pb1


def _diag_pv(vt_top, vt_bot, p_top, p_bot):
    h = p_bot.shape[0]
    a = _dot(vt_top, p_top)
    b = _dot(vt_bot, p_bot)
    return jnp.concatenate([a[:, :h], a[:, h:] + b], axis=1)


def _run_pipeline(heads, tab_ref, nblk):
    A, A_DIAG, B_DIAG, C_DIAG, B, C = range(6)
    pairs = _offdiag_pairs(nblk)
    npairs = len(pairs)
    assert (nblk - 2) % 2 == 0 and (npairs - 2) % 2 == 0

    def step(c=None, b=None, a=None):
        for call in (c, b, a):
            if call is not None:
                for stages in heads:
                    stages[call[0]](*call[1])

    step(a=(A_DIAG, (0,)))
    step(b=(B_DIAG, (0,)), a=(A_DIAG, (1,)))

    def diag_loop(u, carry):
        for d in (1, 2):
            g = d + 2 * u
            step(c=(C_DIAG, (g - 1,)), b=(B_DIAG, (g,)), a=(A_DIAG, (g + 1,)))
        return carry

    lax.fori_loop(0, (nblk - 2) // 2, diag_loop, 0)

    step(c=(C_DIAG, (nblk - 2,)), b=(B_DIAG, (nblk - 1,)), a=(A, pairs[0]))
    step(c=(C_DIAG, (nblk - 1,)), b=(B, pairs[0]), a=(A, pairs[1]))

    def off_loop(u, carry):
        pair = lambda f: (tab_ref[0, f], tab_ref[1, f])
        for d in (1, 2):
            f = d + 2 * u
            step(c=(C, pair(f - 1)), b=(B, pair(f)), a=(A, pair(f + 1)))
        return carry

    lax.fori_loop(0, (npairs - 2) // 2, off_loop, 0)

    step(c=(C, pairs[npairs - 2]), b=(B, pairs[npairs - 1]))
    step(c=(C, pairs[npairs - 1]))


def _mla_stages(qt_ref, k_ref, vt_ref, o_ref, mb_ref, s_ref, p_ref, al_ref, m_ref, acc_ref, blk, nblk):
    half = blk // 2

    def stage_a(i, t):
        s_ref[...] = _dot(k_ref[_blk_slice(t, blk), :], qt_ref[:, _blk_slice(i, blk)])

    def stage_a_diag(i):
        top, bot = _half_slices(i, blk)
        s_ref[:half, :] = _dot(k_ref[top, :], qt_ref[:, _blk_slice(i, blk)])
        s_ref[half:, half:] = _dot(k_ref[bot, :], qt_ref[:, bot])

    def stage_b_diag(i):
        m, p_top, p_bot = _diag_softmax(s_ref[:half, :], s_ref[half:, half:], mb_ref[...])
        m_ref[i] = m
        p_ref[:half, :] = p_top.astype(BF16)
        p_ref[half:, half:] = p_bot.astype(BF16)

    def stage_c_diag(i):
        top, bot = _half_slices(i, blk)
        acc_ref[i] = _diag_pv(vt_ref[:, top], vt_ref[:, bot], p_ref[:half, :], p_ref[half:, half:])

    def stage_b(i, t):
        s = s_ref[...]
        m_prev = m_ref[i]
        m_new = jnp.maximum(m_prev, jnp.max(s, axis=0, keepdims=True))
        alpha = jnp.exp2(m_prev - m_new)
        p = jnp.exp2(s - m_new)
        m_ref[i] = m_new
        al_ref[...] = alpha
        p_ref[...] = p.astype(BF16)

    def stage_c(i, t):
        acc_ref[i] = al_ref[...] * acc_ref[i] + _dot(vt_ref[:, _blk_slice(t, blk)], p_ref[...])

    def finalize():
        for i in range(nblk):
            acc = acc_ref[i]
            o_t = acc[:MLA_V] * (1.0 / acc[MLA_V:MLA_V + 1])
            o_ref[_blk_slice(i, blk), :] = o_t.T.astype(BF16)

    return (stage_a, stage_a_diag, stage_b_diag, stage_c_diag, stage_b, stage_c), finalize


def _mla_scratch(blk, nblk):
    return [
        pltpu.VMEM((blk, blk), F32),
        pltpu.VMEM((blk, blk), BF16),
        pltpu.VMEM((1, blk), F32),
        pltpu.VMEM((nblk, 1, blk), F32),
        pltpu.VMEM((nblk, VT_ROWS, blk), F32),
    ]


def _alibi_lane_tables(slopes_l2e):
    c1 = slopes_l2e.astype(BF16).astype(F32)
    c2 = (slopes_l2e - c1).astype(BF16).astype(F32)
    c3 = (slopes_l2e - c1 - c2).astype(BF16).astype(F32)
    cw = (jnp.stack([c1, c2, c3], axis=1)[:, :, None]
          * jnp.asarray([1.0, 256.0, 65536.0], F32)[None, None, :]).reshape(-1, ALIBI_LANES)
    heads = slopes_l2e.shape[0]
    ktab = jnp.zeros((heads, 2, LANE), F32)
    for x, base in enumerate(ALIBI_BASE):
        ktab = ktab.at[:, x, base:base + ALIBI_LANES].set(-cw)
    qcol = jnp.zeros((heads, DIFF_HEAD_DIM), F32).at[:, ALIBI_LANES:2 * ALIBI_LANES].set(cw)
    return ktab.astype(BF16), qcol.reshape(heads, DIFF_HEAD_DIM, 1)


def _diff_stages(qt_ref, k_ref, vt_ref, pc_ref, pr_ref, shr_ref, shc_ref, kt_ref, qc_ref,
                 lq1_ref, lk1_ref, lq2_ref, lk2_ref, gsub_ref, o_ref, mb_ref, dig_ref, digt_ref,
                 qm_ref, km_ref, s_ref, p_ref, al_ref, m_ref, acc_ref, blk, nblk, lam_init):
    half = DIFF_HEAD_DIM

    @pl.when(pl.program_id(1) == 0)
    def _():
        pcol = pc_ref[...]
        prel = jnp.broadcast_to(pcol - pcol[0:1, :], dig_ref.shape)
        shift = jnp.broadcast_to(shr_ref[...], dig_ref.shape)
        dig_ref[...] = (lax.shift_right_logical(prel, shift) & 255).astype(F32).astype(BF16)
        prow = pr_ref[0]
        prel_t = jnp.broadcast_to(prow - prow[:, 0:1], digt_ref.shape)
        shift_t = jnp.broadcast_to(shc_ref[...], digt_ref.shape)
        digt_ref[...] = (lax.shift_right_logical(prel_t, shift_t) & 255).astype(F32).astype(BF16)

    row = lax.broadcasted_iota(jnp.int32, (half, 1), 0)
    alibi_q = jnp.where(row < ALIBI_LANES, digt_ref[...],
                        jnp.broadcast_to(qc_ref[0], digt_ref.shape).astype(BF16))
    qm_ref[0, :half, :] = qt_ref[:half, :]
    qm_ref[0, half:, :] = alibi_q
    qm_ref[1, :half, :] = alibi_q
    qm_ref[1, half:, :] = qt_ref[half:, :]
    lane = lax.broadcasted_iota(jnp.int32, (1, LANE), 1)
    dig = dig_ref[...]
    k = k_ref[...]
    for x, base in enumerate(ALIBI_BASE):
        own = (lane < half) if x == 0 else (lane >= half)
        k_digit = (lane >= base + ALIBI_LANES) & (lane < base + 2 * ALIBI_LANES)
        km_ref[x] = jnp.where(own, k, jnp.where(k_digit, dig, kt_ref[0, x:x + 1, :]))

    def stage_a(i, t):
        for x in range(2):
            s_ref[x] = _dot(km_ref[x, _blk_slice(t, blk), :], qm_ref[x, :, _blk_slice(i, blk)])

    hb = blk // 2

    def stage_a_diag(i):
        top, bot = _half_slices(i, blk)
        for x in range(2):
            s_ref[x, :hb, :] = _dot(km_ref[x, top, :], qm_ref[x, :, _blk_slice(i, blk)])
            s_ref[x, hb:, hb:] = _dot(km_ref[x, bot, :], qm_ref[x, :, bot])

    def stage_b_diag(i):
        for x in range(2):
            m, p_top, p_bot = _diag_softmax(s_ref[x, :hb, :], s_ref[x, hb:, hb:], mb_ref[...])
            m_ref[x, i] = m
            p_ref[x, :hb, :] = p_top.astype(BF16)
            p_ref[x, hb:, hb:] = p_bot.astype(BF16)

    def stage_c_diag(i):
        top, bot = _half_slices(i, blk)
        for x in range(2):
            acc_ref[x, i] = _diag_pv(vt_ref[:, top], vt_ref[:, bot], p_ref[x, :hb, :], p_ref[x, hb:, hb:])

    def stage_b(i, t):
        for x in range(2):
            s = s_ref[x]
            m_prev = m_ref[x, i]
            m_new = jnp.maximum(m_prev, jnp.max(s, axis=0, keepdims=True))
            alpha = jnp.exp2(m_prev - m_new)
            p = jnp.exp2(s - m_new)
            m_ref[x, i] = m_new
            al_ref[x] = alpha
            p_ref[x] = p.astype(BF16)

    def stage_c(i, t):
        vt = vt_ref[:, _blk_slice(t, blk)]
        for x in range(2):
            acc_ref[x, i] = al_ref[x] * acc_ref[x, i] + _dot(vt, p_ref[x])

    def finalize():
        lam = (jnp.exp(jnp.sum(lq1_ref[...] * lk1_ref[...], axis=-1, keepdims=True))
               - jnp.exp(jnp.sum(lq2_ref[...] * lk2_ref[...], axis=-1, keepdims=True))
               + lam_init)
        for i in range(nblk):
            a1, a2 = acc_ref[0, i], acc_ref[1, i]
            o_t = (a1[:MLA_V] * (1.0 / a1[MLA_V:MLA_V + 1])
                   - lam * (a2[:MLA_V] * (1.0 / a2[MLA_V:MLA_V + 1])))
            o = o_t.T
            o_ref[_blk_slice(i, blk), :] = (_rms(o, gsub_ref[...]) * (1.0 - lam_init)).astype(BF16)

    return (stage_a, stage_a_diag, stage_b_diag, stage_c_diag, stage_b, stage_c), finalize


def _diff_scratch(seq, blk, nblk):
    return [
        pltpu.VMEM((seq, LANE), BF16),
        pltpu.VMEM((DIFF_HEAD_DIM, seq), BF16),
        pltpu.VMEM((2, LANE, seq), BF16),
        pltpu.VMEM((2, seq, LANE), BF16),
        pltpu.VMEM((2, blk, blk), F32),
        pltpu.VMEM((2, blk, blk), BF16),
        pltpu.VMEM((2, 1, blk), F32),
        pltpu.VMEM((2, nblk, 1, blk), F32),
        pltpu.VMEM((2, nblk, VT_ROWS, blk), F32),
    ]


N_MLA_IN, N_DIFF_IN = 3, 14
N_MLA_SCRATCH, N_DIFF_SCRATCH = 5, 9


def _attention_body(tab_ref, *refs, blk, nblk, lam_init):
    mla_in, refs = refs[:N_MLA_IN], refs[N_MLA_IN:]
    diff_in, refs = refs[:N_DIFF_IN], refs[N_DIFF_IN:]
    (o_mla_ref, o_diff_ref, mb_ref), refs = refs[:3], refs[3:]
    mla_scr, diff_scr = refs[:N_MLA_SCRATCH], refs[N_MLA_SCRATCH:]
    assert len(diff_scr) == N_DIFF_SCRATCH

    @pl.when((pl.program_id(0) == 0) & (pl.program_id(1) == 0))
    def _():
        _init_mask_bias(mb_ref)

    mla, mla_fin = _mla_stages(*mla_in, o_mla_ref, mb_ref, *mla_scr, blk, nblk)
    diff, diff_fin = _diff_stages(*diff_in, o_diff_ref, mb_ref, *diff_scr, blk, nblk, lam_init)

    _run_pipeline([mla, diff], tab_ref, nblk)
    mla_fin()
    diff_fin()


def _attention(qt, k, vt, big_n, dqt, dvt, pos_icol, pos_irow, slopes_l2e, lq1, lk1, lq2, lk2, g_sub,
               batch, seq, blk, lam_init):
    assert MLA_HEADS == DIFF_HEADS
    nblk = seq // blk
    assert nblk >= 3
    width = DIFF_HEADS * 2 * DIFF_HEAD_DIM
    cb = width // LANE
    k_off = 1 * cb
    k_tab, q_col = _alibi_lane_tables(slopes_l2e)
    digit_k = np.arange(LANE) % DIFF_HEAD_DIM % 3
    shift_row = jnp.asarray((digit_k * 8).reshape(1, LANE), jnp.int32)
    shift_col = jnp.asarray((digit_k[:DIFF_HEAD_DIM] * 8).reshape(DIFF_HEAD_DIM, 1), jnp.int32)
    small = lambda a: pl.BlockSpec(a.shape, lambda b, h: (0,) * a.ndim)
    mla_specs = [
        pl.BlockSpec((QK_PAD, seq), lambda b, h: (h, b)),
        pl.BlockSpec((seq, QK_PAD), lambda b, h: (b, h)),
        pl.BlockSpec((VT_ROWS, seq), lambda b, h: (h, b)),
    ]
    diff_specs = [
        pl.BlockSpec((LANE, seq), lambda b, h: (h, b)),
        pl.BlockSpec((seq, LANE), lambda b, h: (b, k_off + h)),
        pl.BlockSpec((VT_ROWS, seq), lambda b, h: (h, b)),
        pl.BlockSpec((seq, 1), lambda b, h: (b, 0)),
        pl.BlockSpec((1, 1, seq), lambda b, h: (b, 0, 0)),
        small(shift_row), small(shift_col),
        pl.BlockSpec((1, 2, LANE), lambda b, h: (h, 0, 0)),
        pl.BlockSpec((1, DIFF_HEAD_DIM, 1), lambda b, h: (h, 0, 0)),
        small(lq1), small(lk1), small(lq2), small(lk2), small(g_sub),
    ]
    assert len(mla_specs) == N_MLA_IN and len(diff_specs) == N_DIFF_IN
    out_spec = pl.BlockSpec((seq, LANE), lambda b, h: (b, h))
    return pl.pallas_call(
        functools.partial(_attention_body, blk=blk, nblk=nblk, lam_init=lam_init),
        grid=(batch, MLA_HEADS),
        in_specs=[pl.BlockSpec(memory_space=pltpu.SMEM)] + mla_specs + diff_specs,
        out_specs=[out_spec, out_spec],
        out_shape=[jax.ShapeDtypeStruct((batch * seq, MLA_HEADS * MLA_V), BF16),
                   jax.ShapeDtypeStruct((batch * seq, width), BF16)],
        scratch_shapes=([pltpu.VMEM((blk // 2, blk // 2), F32)]
                        + _mla_scratch(blk, nblk) + _diff_scratch(seq, blk, nblk)),
        compiler_params=pltpu.CompilerParams(
            dimension_semantics=("arbitrary", "arbitrary"),
            vmem_limit_bytes=VMEM_LIMIT),
        name="attention",
    )(_pair_table(nblk), qt, k, vt, dqt, big_n, dvt, pos_icol, pos_irow, shift_row, shift_col,
      k_tab, q_col, lq1, lk1, lq2, lk2, g_sub)


def _outproj_body(x_ref, om_ref, od_ref, gm_ref, gd_ref, wf_ref, gpost_ref, o_ref, w_ref):
    @pl.when(pl.program_id(0) == 0)
    def _():
        rows = 256
        def cast(c, carry):
            r = pl.ds(pl.multiple_of(c * rows, rows), rows)
            w_ref[r, :] = wf_ref[r, :].astype(BF16)
            return carry
        lax.fori_loop(0, wf_ref.shape[0] // rows, cast, 0)

    gm = gm_ref[...].astype(F32)
    gd = gd_ref[...].astype(F32)
    mm = (om_ref[...].astype(F32) * (gm * jax.nn.sigmoid(gm))).astype(BF16)
    md = (od_ref[...].astype(F32) * (gd * jax.nn.sigmoid(gd))).astype(BF16)
    half = mm.shape[1]
    y = _dot(mm, w_ref[:half, :]) + _dot(md, w_ref[half:, :])
    o_ref[...] = x_ref[...] + _rms(y, gpost_ref[...])


def _outproj(x2, o_mla, o_diff, big, w_out, g_post, tm):
    m, d = x2.shape
    half = o_mla.shape[1]
    gate_diff_blk = (big.shape[1] - half) // half
    return pl.pallas_call(
        _outproj_body,
        grid=(m // tm,),
        in_specs=[
            pl.BlockSpec((tm, d), lambda i: (i, 0)),
            pl.BlockSpec((tm, half), lambda i: (i, 0)),
            pl.BlockSpec((tm, half), lambda i: (i, 0)),
            pl.BlockSpec((tm, half), lambda i: (i, 0)),
            pl.BlockSpec((tm, half), lambda i: (i, gate_diff_blk)),
            pl.BlockSpec(w_out.shape, lambda i: (0, 0), pipeline_mode=pl.Buffered(1)),
            pl.BlockSpec((1, d), lambda i: (0, 0)),
        ],
        out_specs=pl.BlockSpec((tm, d), lambda i: (i, 0)),
        out_shape=jax.ShapeDtypeStruct((m, d), F32),
        scratch_shapes=[pltpu.VMEM(w_out.shape, BF16)],
        compiler_params=pltpu.CompilerParams(
            dimension_semantics=("arbitrary",),
            vmem_limit_bytes=VMEM_LIMIT),
        name="outproj",
    )(x2, o_mla, o_diff, big, big, w_out, g_post)


def kernel(x, positions, g_pre, w_in, g_q_a, w_q_b, g_kv_a, w_kv_b, lambda_q1, lambda_k1,
           lambda_q2, lambda_k2, g_diff_sub, w_out, g_post):
    batch, seq, d = x.shape
    depth = g_pre.shape[0]
    q_rank = w_q_b.shape[1]
    kv_rank = w_kv_b.shape[1]
    half_rope = MLA_ROPE // 2
    lat_end = q_rank + kv_rank
    pe_end = lat_end + MLA_ROPE

    pos_icol = positions.astype(jnp.int32).reshape(batch * seq, 1)
    pos_irow = positions.astype(jnp.int32).reshape(batch, 1, seq)
    freqs = 1.0 / (ROPE_THETA ** (jnp.arange(0, MLA_ROPE, 2, dtype=F32) / MLA_ROPE))
    freq = jnp.tile(freqs, 4).reshape(1, LANE)
    slopes = 2.0 ** (-8.0 * (jnp.arange(DIFF_HEADS, dtype=F32) + 1.0) / DIFF_HEADS)
    slopes_l2e = slopes * LOG2E

    mla_scale = MLA_QK ** -0.5 * LOG2E
    diff_scale = DIFF_HEAD_DIM ** -0.5 * LOG2E
    width = DIFF_HEADS * 2 * DIFF_HEAD_DIM

    x2 = x.reshape(batch * seq, d)
    for l in range(depth):
        lam_init = 0.8 - 0.6 * math.exp(-0.3 * l)
        w_lat, w_n, w_t = _regroup_w_in(w_in, l, lat_end, half_rope, width, rows=REGROUP_ROWS)
        wq = w_q_b[l]
        t1 = wq[:, :, MLA_NOPE:MLA_NOPE + half_rope]
        t2 = wq[:, :, MLA_NOPE + half_rope:]
        w_q = jnp.concatenate([wq[:, :, :MLA_NOPE], t1, t2, t2, t1], axis=-1)
        w_qt = w_q.reshape(q_rank, MLA_HEADS * QK_PAD).T.astype(BF16)
        wkv = w_kv_b[l]
        w_k = wkv[:, :, :MLA_NOPE].reshape(kv_rank, -1).astype(BF16)
        w_vt = wkv[:, :, MLA_NOPE:].reshape(kv_rank, -1).T.astype(BF16)
        gp = g_pre[l].reshape(1, d)

        h, big_n, dqt, dvt = _inproj(x2, gp, w_n, w_t, diff_scale, tm=PROJ_ROWS, tn=PROJ_COLS)
        qt, k, vt = _latent(h, pos_icol, freq, w_lat, g_q_a[l].reshape(1, -1), w_qt,
                            g_kv_a[l].reshape(1, -1), w_k, w_vt, tm=PROJ_ROWS, q_scale=mla_scale)
        o_mla, o_diff = _attention(
            qt, k, vt, big_n, dqt, dvt, pos_icol, pos_irow, slopes_l2e,
            lambda_q1[l].reshape(1, -1), lambda_k1[l].reshape(1, -1),
            lambda_q2[l].reshape(1, -1), lambda_k2[l].reshape(1, -1),
            g_diff_sub[l].reshape(1, -1), batch, seq, blk=ATTN_BLOCK, lam_init=lam_init)
        x2 = _outproj(x2, o_mla, o_diff, big_n, w_out[l], g_post[l].reshape(1, d), tm=PROJ_ROWS)
    return x2.reshape(batch, seq, d)
```

```python
import functools
import math

import numpy as np
import jax
import jax.numpy as jnp
from jax import lax
from jax.experimental import pallas as pl
from jax.experimental.pallas import tpu as pltpu

F32 = jnp.float32
BF16 = jnp.bfloat16

EPS = 1e-6
LOG2E = 1.4426950408889634
ROPE_THETA = 10000.0

MLA_HEADS = 8
MLA_NOPE = 128
MLA_ROPE = 64
MLA_V = 128
VT_ROWS = MLA_V + 16
MLA_QK = MLA_NOPE + MLA_ROPE
DIFF_HEADS = 8
DIFF_HEAD_DIM = 64
LANE = 128
QK_PAD = 256
ALIBI_LANES = 9
ALIBI_BASE = (DIFF_HEAD_DIM, 0)

VMEM_LIMIT = 56 * 1024 * 1024

PROJ_ROWS = 512
PROJ_COLS = 512
ATTN_BLOCK = 512
REGROUP_ROWS = 256


def _rms(xf, g):
    ms = jnp.mean(xf * xf, axis=-1, keepdims=True)
    return xf * lax.rsqrt(ms + EPS) * g


def _dot(a, b):
    return jnp.dot(a, b, preferred_element_type=F32)


def _dot_nt(a, b):
    return lax.dot_general(a, b, (((1,), (1,)), ((), ())), preferred_element_type=F32)


def _store_vt_ext(vt_ref, head, vt_head):
    r = head * VT_ROWS
    vt_ref[r:r + MLA_V, :] = vt_head
    vt_ref[r + MLA_V:r + VT_ROWS, :] = jnp.ones((VT_ROWS - MLA_V, vt_head.shape[1]), vt_head.dtype)


def _regroup_body(wt_ref, lat_ref, n_ref, t_ref, *, lat_end, half_rope, width):
    pe_end = lat_end + 2 * half_rope
    cols = lambda a, b: wt_ref[0, a:b, :]
    u1 = cols(lat_end, lat_end + half_rope)
    u2 = cols(lat_end + half_rope, pe_end)
    lat_ref[:, :lat_end] = cols(0, lat_end).T.astype(BF16)
    lat_ref[:, lat_end:] = jnp.concatenate([u1, u1, u2, u2, u2, u2, u1, u1], axis=0).T.astype(BF16)
    group = lambda n: cols(pe_end + n * width, pe_end + (n + 1) * width).T.astype(BF16)
    for dst, src in enumerate((0, 2, 4)):
        n_ref[:, dst * width:(dst + 1) * width] = group(src)
    for dst, src in enumerate((1, 3)):
        t_ref[:, dst * width:(dst + 1) * width] = group(src)


def _regroup_w_in(w_in, layer, lat_end, half_rope, width, rows):
    _, d, n_in = w_in.shape
    n_lat = lat_end + 8 * half_rope
    w_in_t = jnp.swapaxes(w_in, 1, 2)
    return pl.pallas_call(
        functools.partial(_regroup_body, lat_end=lat_end, half_rope=half_rope, width=width),
        grid=(d // rows,),
        in_specs=[pl.BlockSpec((1, n_in, rows), lambda i: (layer, 0, i))],
        out_specs=[
            pl.BlockSpec((rows, n_lat), lambda i: (i, 0)),
            pl.BlockSpec((rows, 3 * width), lambda i: (i, 0)),
            pl.BlockSpec((rows, 2 * width), lambda i: (i, 0)),
        ],
        out_shape=[
            jax.ShapeDtypeStruct((d, n_lat), BF16),
            jax.ShapeDtypeStruct((d, 3 * width), BF16),
            jax.ShapeDtypeStruct((d, 2 * width), BF16),
        ],
        compiler_params=pltpu.CompilerParams(
            dimension_semantics=("arbitrary",),
            vmem_limit_bytes=VMEM_LIMIT),
        name="regroup_w_in",
    )(w_in_t)


def _inproj_body(x_ref, g_ref, wn_ref, wt_ref, h_ref, on_ref, oq_ref, ov_ref, *, tn, q_scale):
    h = _rms(x_ref[...], g_ref[...]).astype(BF16)
    h_ref[...] = h
    for c in range(wn_ref.shape[1] // tn):
        cols = slice(c * tn, (c + 1) * tn)
        on_ref[:, cols] = _dot(h, wn_ref[:, cols]).astype(BF16)
    n_q = oq_ref.shape[0] // tn
    heads_per_chunk = tn // MLA_V
    for c in range(wt_ref.shape[1] // tn):
        cols = slice(c * tn, (c + 1) * tn)
        acc = _dot(h, wt_ref[:, cols])
        if c < n_q:
            oq_ref[cols, :] = (acc * q_scale).T.astype(BF16)
        else:
            vt = acc.T.astype(BF16)
            for j in range(heads_per_chunk):
                _store_vt_ext(ov_ref, (c - n_q) * heads_per_chunk + j, vt[j * MLA_V:(j + 1) * MLA_V, :])


def _inproj(x2, g_pre, w_n, w_t, q_scale, tm, tn):
    m, d = x2.shape
    width = w_t.shape[1] // 2
    nv = width // MLA_V * VT_ROWS
    resident = lambda a: pl.BlockSpec(a.shape, lambda i: (0,) * a.ndim, pipeline_mode=pl.Buffered(1))
    return pl.pallas_call(
        functools.partial(_inproj_body, tn=tn, q_scale=q_scale),
        grid=(m // tm,),
        in_specs=[
            pl.BlockSpec((tm, d), lambda i: (i, 0)),
            pl.BlockSpec((1, d), lambda i: (0, 0)),
            resident(w_n),
            resident(w_t),
        ],
        out_specs=[
            pl.BlockSpec((tm, d), lambda i: (i, 0)),
            pl.BlockSpec((tm, w_n.shape[1]), lambda i: (i, 0)),
            pl.BlockSpec((width, tm), lambda i: (0, i)),
            pl.BlockSpec((nv, tm), lambda i: (0, i)),
        ],
        out_shape=[
            jax.ShapeDtypeStruct((m, d), BF16),
            jax.ShapeDtypeStruct((m, w_n.shape[1]), BF16),
            jax.ShapeDtypeStruct((width, m), BF16),
            jax.ShapeDtypeStruct((nv, m), BF16),
        ],
        compiler_params=pltpu.CompilerParams(
            dimension_semantics=("arbitrary",),
            vmem_limit_bytes=VMEM_LIMIT),
        name="inproj",
    )(x2, g_pre, w_n, w_t)


def _latent_body(h_ref, pos_ref, freq_ref, wlat_ref, gq_ref, wqt_ref,
                 gkv_ref, wk_ref, wvt_ref, qt_ref, k_ref, vt_ref, *, q_scale):
    lat = _dot(h_ref[...], wlat_ref[...])
    ang = pos_ref[...].astype(F32) * freq_ref[...]
    cos = jnp.cos(ang)
    sin = jnp.sin(ang)
    seg = lax.broadcasted_iota(jnp.int32, (1, LANE), 1) // (MLA_ROPE // 2)
    fq = jnp.where(seg == 1, -sin, jnp.where(seg == 3, sin, cos)) * q_scale
    sk = jnp.where(seg < 2, -sin, sin)
    k_ext = (lat[:, 768:896] * cos + lat[:, 896:1024] * sk).astype(BF16)

    c_q = _rms(lat[:, :512], gq_ref[...]).astype(BF16)
    qft = _dot_nt(wqt_ref[...], c_q)
    fqt = fq.T
    c_kv = _rms(lat[:, 512:768], gkv_ref[...]).astype(BF16)
    kf = _dot(c_kv, wk_ref[...])
    vt = _dot_nt(wvt_ref[...], c_kv).astype(BF16)
    for hd in range(MLA_HEADS):
        _store_vt_ext(vt_ref, hd, vt[hd * MLA_V:(hd + 1) * MLA_V, :])
        o = hd * QK_PAD
        qt_ref[o:o + LANE, :] = (qft[o:o + LANE, :] * q_scale).astype(BF16)
        qt_ref[o + LANE:o + QK_PAD, :] = (qft[o + LANE:o + QK_PAD, :] * fqt).astype(BF16)
        k_ref[:, o:o + LANE] = kf[:, hd * LANE:(hd + 1) * LANE].astype(BF16)
        k_ref[:, o + LANE:o + QK_PAD] = k_ext


def _latent(h, pos_col, freq, w_lat, g_q, w_qt, g_kv, w_k, w_vt, tm, q_scale):
    m, d = h.shape
    nq = MLA_HEADS * QK_PAD
    nv = MLA_HEADS * VT_ROWS
    full = lambda a: pl.BlockSpec(a.shape, lambda i: (0,) * a.ndim)
    return pl.pallas_call(
        functools.partial(_latent_body, q_scale=q_scale),
        grid=(m // tm,),
        in_specs=[
            pl.BlockSpec((tm, d), lambda i: (i, 0)),
            pl.BlockSpec((tm, 1), lambda i: (i, 0)),
            full(freq), full(w_lat), full(g_q), full(w_qt), full(g_kv), full(w_k),
            full(w_vt),
        ],
        out_specs=[
            pl.BlockSpec((nq, tm), lambda i: (0, i)),
            pl.BlockSpec((tm, nq), lambda i: (i, 0)),
            pl.BlockSpec((nv, tm), lambda i: (0, i)),
        ],
        out_shape=[
            jax.ShapeDtypeStruct((nq, m), BF16),
            jax.ShapeDtypeStruct((m, nq), BF16),
            jax.ShapeDtypeStruct((nv, m), BF16),
        ],
        compiler_params=pltpu.CompilerParams(
            dimension_semantics=("arbitrary",),
            vmem_limit_bytes=VMEM_LIMIT),
        name="latent",
    )(h, pos_col, freq, w_lat, g_q, w_qt, g_kv, w_k, w_vt)


def _blk_slice(idx, blk):
    if isinstance(idx, int):
        return slice(idx * blk, (idx + 1) * blk)
    return pl.ds(pl.multiple_of(idx * blk, blk), blk)


def _offdiag_pairs(nblk):
    return [(i, t) for i in range(1, nblk) for t in range(i)]


def _pair_table(nblk):
    return jnp.asarray(np.array(_offdiag_pairs(nblk), dtype=np.int32).T)


def _half_slices(idx, blk):
    half = blk // 2
    if isinstance(idx, int):
        return slice(idx * blk, idx * blk + half), slice(idx * blk + half, (idx + 1) * blk)
    start = pl.multiple_of(idx * blk, blk)
    return pl.ds(start, half), pl.ds(pl.multiple_of(start + half, half), half)


def _init_mask_bias(mb_ref):
    kv = lax.broadcasted_iota(jnp.int32, mb_ref.shape, 0)
    qi = lax.broadcasted_iota(jnp.int32, mb_ref.shape, 1)
    mb_ref[...] = jnp.where(kv <= qi, 0.0, -jnp.inf).astype(F32)


def _diag_softmax(s_top, s_bot, mb):
    h = s_bot.shape[0]
    a0 = s_top[:, :h] + mb
    a1 = s_top[:, h:]
    b1 = s_bot + mb
    m0 = jnp.max(a0, axis=0, keepdims=True)
    m1 = jnp.maximum(jnp.max(a1, axis=0, keepdims=True), jnp.max(b1, axis=0, keepdims=True))
    p0 = jnp.exp2(a0 - m0)
    pa1 = jnp.exp2(a1 - m1)
    pb1 = jnp.exp2(b1 - m1)
    return jnp.concatenate([m0, m1], axis=1), jnp.concatenate([p0, pa1], axis=1), pb1


def _diag_pv(vt_top, vt_bot, p_top, p_bot):
    h = p_bot.shape[0]
    a = _dot(vt_top, p_top)
    b = _dot(vt_bot, p_bot)
    return jnp.concatenate([a[:, :h], a[:, h:] + b], axis=1)


def _run_pipeline(heads, tab_ref, nblk):
    A, A_DIAG, B_DIAG, C_DIAG, B, C = range(6)
    pairs = _offdiag_pairs(nblk)
    npairs = len(pairs)
    assert (nblk - 2) % 2 == 0 and (npairs - 2) % 2 == 0

    def step(c=None, b=None, a=None):
        for call in (c, b, a):
            if call is not None:
                for stages in heads:
                    stages[call[0]](*call[1])

    step(a=(A_DIAG, (0,)))
    step(b=(B_DIAG, (0,)), a=(A_DIAG, (1,)))

    def diag_loop(u, carry):
        for d in (1, 2):
            g = d + 2 * u
            step(c=(C_DIAG, (g - 1,)), b=(B_DIAG, (g,)), a=(A_DIAG, (g + 1,)))
        return carry

    lax.fori_loop(0, (nblk - 2) // 2, diag_loop, 0)

    step(c=(C_DIAG, (nblk - 2,)), b=(B_DIAG, (nblk - 1,)), a=(A, pairs[0]))
    step(c=(C_DIAG, (nblk - 1,)), b=(B, pairs[0]), a=(A, pairs[1]))

    def off_loop(u, carry):
        pair = lambda f: (tab_ref[0, f], tab_ref[1, f])
        for d in (1, 2):
            f = d + 2 * u
            step(c=(C, pair(f - 1)), b=(B, pair(f)), a=(A, pair(f + 1)))
        return carry

    lax.fori_loop(0, (npairs - 2) // 2, off_loop, 0)

    step(c=(C, pairs[npairs - 2]), b=(B, pairs[npairs - 1]))
    step(c=(C, pairs[npairs - 1]))


def _mla_stages(qt_ref, k_ref, vt_ref, o_ref, mb_ref, s_ref, p_ref, al_ref, m_ref, acc_ref, blk, nblk):
    half = blk // 2

    def stage_a(i, t):
        s_ref[...] = _dot(k_ref[_blk_slice(t, blk), :], qt_ref[:, _blk_slice(i, blk)])

    def stage_a_diag(i):
        top, bot = _half_slices(i, blk)
        s_ref[:half, :] = _dot(k_ref[top, :], qt_ref[:, _blk_slice(i, blk)])
        s_ref[half:, half:] = _dot(k_ref[bot, :], qt_ref[:, bot])

    def stage_b_diag(i):
        m, p_top, p_bot = _diag_softmax(s_ref[:half, :], s_ref[half:, half:], mb_ref[...])
        m_ref[i] = m
        p_ref[:half, :] = p_top.astype(BF16)
        p_ref[half:, half:] = p_bot.astype(BF16)

    def stage_c_diag(i):
        top, bot = _half_slices(i, blk)
        acc_ref[i] = _diag_pv(vt_ref[:, top], vt_ref[:, bot], p_ref[:half, :], p_ref[half:, half:])

    def stage_b(i, t):
        s = s_ref[...]
        m_prev = m_ref[i]
        m_new = jnp.maximum(m_prev, jnp.max(s, axis=0, keepdims=True))
        alpha = jnp.exp2(m_prev - m_new)
        p = jnp.exp2(s - m_new)
        m_ref[i] = m_new
        al_ref[...] = alpha
        p_ref[...] = p.astype(BF16)

    def stage_c(i, t):
        acc_ref[i] = al_ref[...] * acc_ref[i] + _dot(vt_ref[:, _blk_slice(t, blk)], p_ref[...])

    def finalize():
        for i in range(nblk):
            acc = acc_ref[i]
            o_t = acc[:MLA_V] * (1.0 / acc[MLA_V:MLA_V + 1])
            o_ref[_blk_slice(i, blk), :] = o_t.T.astype(BF16)

    return (stage_a, stage_a_diag, stage_b_diag, stage_c_diag, stage_b, stage_c), finalize


def _mla_scratch(blk, nblk):
    return [
        pltpu.VMEM((blk, blk), F32),
        pltpu.VMEM((blk, blk), BF16),
        pltpu.VMEM((1, blk), F32),
        pltpu.VMEM((nblk, 1, blk), F32),
        pltpu.VMEM((nblk, VT_ROWS, blk), F32),
    ]


def _alibi_lane_tables(slopes_l2e):
    c1 = slopes_l2e.astype(BF16).astype(F32)
    c2 = (slopes_l2e - c1).astype(BF16).astype(F32)
    c3 = (slopes_l2e - c1 - c2).astype(BF16).astype(F32)
    cw = (jnp.stack([c1, c2, c3], axis=1)[:, :, None]
          * jnp.asarray([1.0, 256.0, 65536.0], F32)[None, None, :]).reshape(-1, ALIBI_LANES)
    heads = slopes_l2e.shape[0]
    ktab = jnp.zeros((heads, 2, LANE), F32)
    for x, base in enumerate(ALIBI_BASE):
        ktab = ktab.at[:, x, base:base + ALIBI_LANES].set(-cw)
    qcol = jnp.zeros((heads, DIFF_HEAD_DIM), F32).at[:, ALIBI_LANES:2 * ALIBI_LANES].set(cw)
    return ktab.astype(BF16), qcol.reshape(heads, DIFF_HEAD_DIM, 1)


def _diff_stages(qt_ref, k_ref, vt_ref, pc_ref, pr_ref, shr_ref, shc_ref, kt_ref, qc_ref,
                 lq1_ref, lk1_ref, lq2_ref, lk2_ref, gsub_ref, o_ref, mb_ref, dig_ref, digt_ref,
                 qm_ref, km_ref, s_ref, p_ref, al_ref, m_ref, acc_ref, blk, nblk, lam_init):
    half = DIFF_HEAD_DIM

    @pl.when(pl.program_id(1) == 0)
    def _():
        pcol = pc_ref[...]
        prel = jnp.broadcast_to(pcol - pcol[0:1, :], dig_ref.shape)
        shift = jnp.broadcast_to(shr_ref[...], dig_ref.shape)
        dig_ref[...] = (lax.shift_right_logical(prel, shift) & 255).astype(F32).astype(BF16)
        prow = pr_ref[0]
        prel_t = jnp.broadcast_to(prow - prow[:, 0:1], digt_ref.shape)
        shift_t = jnp.broadcast_to(shc_ref[...], digt_ref.shape)
        digt_ref[...] = (lax.shift_right_logical(prel_t, shift_t) & 255).astype(F32).astype(BF16)

    row = lax.broadcasted_iota(jnp.int32, (half, 1), 0)
    alibi_q = jnp.where(row < ALIBI_LANES, digt_ref[...],
                        jnp.broadcast_to(qc_ref[0], digt_ref.shape).astype(BF16))
    qm_ref[0, :half, :] = qt_ref[:half, :]
    qm_ref[0, half:, :] = alibi_q
    qm_ref[1, :half, :] = alibi_q
    qm_ref[1, half:, :] = qt_ref[half:, :]
    lane = lax.broadcasted_iota(jnp.int32, (1, LANE), 1)
    dig = dig_ref[...]
    k = k_ref[...]
    for x, base in enumerate(ALIBI_BASE):
        own = (lane < half) if x == 0 else (lane >= half)
        k_digit = (lane >= base + ALIBI_LANES) & (lane < base + 2 * ALIBI_LANES)
        km_ref[x] = jnp.where(own, k, jnp.where(k_digit, dig, kt_ref[0, x:x + 1, :]))

    def stage_a(i, t):
        for x in range(2):
            s_ref[x] = _dot(km_ref[x, _blk_slice(t, blk), :], qm_ref[x, :, _blk_slice(i, blk)])

    hb = blk // 2

    def stage_a_diag(i):
        top, bot = _half_slices(i, blk)
        for x in range(2):
            s_ref[x, :hb, :] = _dot(km_ref[x, top, :], qm_ref[x, :, _blk_slice(i, blk)])
            s_ref[x, hb:, hb:] = _dot(km_ref[x, bot, :], qm_ref[x, :, bot])

    def stage_b_diag(i):
        for x in range(2):
            m, p_top, p_bot = _diag_softmax(s_ref[x, :hb, :], s_ref[x, hb:, hb:], mb_ref[...])
            m_ref[x, i] = m
            p_ref[x, :hb, :] = p_top.astype(BF16)
            p_ref[x, hb:, hb:] = p_bot.astype(BF16)

    def stage_c_diag(i):
        top, bot = _half_slices(i, blk)
        for x in range(2):
            acc_ref[x, i] = _diag_pv(vt_ref[:, top], vt_ref[:, bot], p_ref[x, :hb, :], p_ref[x, hb:, hb:])

    def stage_b(i, t):
        for x in range(2):
            s = s_ref[x]
            m_prev = m_ref[x, i]
            m_new = jnp.maximum(m_prev, jnp.max(s, axis=0, keepdims=True))
            alpha = jnp.exp2(m_prev - m_new)
            p = jnp.exp2(s - m_new)
            m_ref[x, i] = m_new
            al_ref[x] = alpha
            p_ref[x] = p.astype(BF16)

    def stage_c(i, t):
        vt = vt_ref[:, _blk_slice(t, blk)]
        for x in range(2):
            acc_ref[x, i] = al_ref[x] * acc_ref[x, i] + _dot(vt, p_ref[x])

    def finalize():
        lam = (jnp.exp(jnp.sum(lq1_ref[...] * lk1_ref[...], axis=-1, keepdims=True))
               - jnp.exp(jnp.sum(lq2_ref[...] * lk2_ref[...], axis=-1, keepdims=True))
               + lam_init)
        for i in range(nblk):
            a1, a2 = acc_ref[0, i], acc_ref[1, i]
            o_t = (a1[:MLA_V] * (1.0 / a1[MLA_V:MLA_V + 1])
                   - lam * (a2[:MLA_V] * (1.0 / a2[MLA_V:MLA_V + 1])))
            o = o_t.T
            o_ref[_blk_slice(i, blk), :] = (_rms(o, gsub_ref[...]) * (1.0 - lam_init)).astype(BF16)

    return (stage_a, stage_a_diag, stage_b_diag, stage_c_diag, stage_b, stage_c), finalize


def _diff_scratch(seq, blk, nblk):
    return [
        pltpu.VMEM((seq, LANE), BF16),
        pltpu.VMEM((DIFF_HEAD_DIM, seq), BF16),
        pltpu.VMEM((2, LANE, seq), BF16),
        pltpu.VMEM((2, seq, LANE), BF16),
        pltpu.VMEM((2, blk, blk), F32),
        pltpu.VMEM((2, blk, blk), BF16),
        pltpu.VMEM((2, 1, blk), F32),
        pltpu.VMEM((2, nblk, 1, blk), F32),
        pltpu.VMEM((2, nblk, VT_ROWS, blk), F32),
    ]


N_MLA_IN, N_DIFF_IN = 3, 14
N_MLA_SCRATCH, N_DIFF_SCRATCH = 5, 9


def _attention_body(tab_ref, *refs, blk, nblk, lam_init):
    mla_in, refs = refs[:N_MLA_IN], refs[N_MLA_IN:]
    diff_in, refs = refs[:N_DIFF_IN], refs[N_DIFF_IN:]
    (o_mla_ref, o_diff_ref, mb_ref), refs = refs[:3], refs[3:]
    mla_scr, diff_scr = refs[:N_MLA_SCRATCH], refs[N_MLA_SCRATCH:]
    assert len(diff_scr) == N_DIFF_SCRATCH

    @pl.when((pl.program_id(0) == 0) & (pl.program_id(1) == 0))
    def _():
        _init_mask_bias(mb_ref)

    mla, mla_fin = _mla_stages(*mla_in, o_mla_ref, mb_ref, *mla_scr, blk, nblk)
    diff, diff_fin = _diff_stages(*diff_in, o_diff_ref, mb_ref, *diff_scr, blk, nblk, lam_init)

    _run_pipeline([mla, diff], tab_ref, nblk)
    mla_fin()
    diff_fin()


def _attention(qt, k, vt, big_n, dqt, dvt, pos_icol, pos_irow, slopes_l2e, lq1, lk1, lq2, lk2, g_sub,
               batch, seq, blk, lam_init):
    assert MLA_HEADS == DIFF_HEADS
    nblk = seq // blk
    assert nblk >= 3
    width = DIFF_HEADS * 2 * DIFF_HEAD_DIM
    cb = width // LANE
    k_off = 1 * cb
    k_tab, q_col = _alibi_lane_tables(slopes_l2e)
    digit_k = np.arange(LANE) % DIFF_HEAD_DIM % 3
    shift_row = jnp.asarray((digit_k * 8).reshape(1, LANE), jnp.int32)
    shift_col = jnp.asarray((digit_k[:DIFF_HEAD_DIM] * 8).reshape(DIFF_HEAD_DIM, 1), jnp.int32)
    small = lambda a: pl.BlockSpec(a.shape, lambda b, h: (0,) * a.ndim)
    mla_specs = [
        pl.BlockSpec((QK_PAD, seq), lambda b, h: (h, b)),
        pl.BlockSpec((seq, QK_PAD), lambda b, h: (b, h)),
        pl.BlockSpec((VT_ROWS, seq), lambda b, h: (h, b)),
    ]
    diff_specs = [
        pl.BlockSpec((LANE, seq), lambda b, h: (h, b)),
        pl.BlockSpec((seq, LANE), lambda b, h: (b, k_off + h)),
        pl.BlockSpec((VT_ROWS, seq), lambda b, h: (h, b)),
        pl.BlockSpec((seq, 1), lambda b, h: (b, 0)),
        pl.BlockSpec((1, 1, seq), lambda b, h: (b, 0, 0)),
        small(shift_row), small(shift_col),
        pl.BlockSpec((1, 2, LANE), lambda b, h: (h, 0, 0)),
        pl.BlockSpec((1, DIFF_HEAD_DIM, 1), lambda b, h: (h, 0, 0)),
        small(lq1), small(lk1), small(lq2), small(lk2), small(g_sub),
    ]
    assert len(mla_specs) == N_MLA_IN and len(diff_specs) == N_DIFF_IN
    out_spec = pl.BlockSpec((seq, LANE), lambda b, h: (b, h))
    return pl.pallas_call(
        functools.partial(_attention_body, blk=blk, nblk=nblk, lam_init=lam_init),
        grid=(batch, MLA_HEADS),
        in_specs=[pl.BlockSpec(memory_space=pltpu.SMEM)] + mla_specs + diff_specs,
        out_specs=[out_spec, out_spec],
        out_shape=[jax.ShapeDtypeStruct((batch * seq, MLA_HEADS * MLA_V), BF16),
                   jax.ShapeDtypeStruct((batch * seq, width), BF16)],
        scratch_shapes=([pltpu.VMEM((blk // 2, blk // 2), F32)]
                        + _mla_scratch(blk, nblk) + _diff_scratch(seq, blk, nblk)),
        compiler_params=pltpu.CompilerParams(
            dimension_semantics=("arbitrary", "arbitrary"),
            vmem_limit_bytes=VMEM_LIMIT),
        name="attention",
    )(_pair_table(nblk), qt, k, vt, dqt, big_n, dvt, pos_icol, pos_irow, shift_row, shift_col,
      k_tab, q_col, lq1, lk1, lq2, lk2, g_sub)


def _outproj_body(x_ref, om_ref, od_ref, gm_ref, gd_ref, w_ref, gpost_ref, o_ref):
    gm = gm_ref[...].astype(F32)
    gd = gd_ref[...].astype(F32)
    mm = (om_ref[...].astype(F32) * (gm * jax.nn.sigmoid(gm))).astype(BF16)
    md = (od_ref[...].astype(F32) * (gd * jax.nn.sigmoid(gd))).astype(BF16)
    half = mm.shape[1]
    d = o_ref.shape[1]
    ss = jnp.zeros((mm.shape[0], 1), F32)
    tn = 2 * PROJ_COLS
    for c in range(d // tn):
        cols = slice(c * tn, (c + 1) * tn)
        y = _dot(mm, w_ref[:half, cols]) + _dot(md, w_ref[half:, cols])
        ss = ss + jnp.sum(y * y, axis=-1, keepdims=True)
        o_ref[:, cols] = y
    inv = lax.rsqrt(ss * (1.0 / d) + EPS)
    o_ref[...] = x_ref[...] + o_ref[...] * inv * gpost_ref[...]


def _outproj(x2, o_mla, o_diff, big, w_out, g_post, tm):
    m, d = x2.shape
    half = o_mla.shape[1]
    gate_diff_blk = (big.shape[1] - half) // half
    return pl.pallas_call(
        _outproj_body,
        grid=(m // tm,),
        in_specs=[
            pl.BlockSpec((tm, d), lambda i: (i, 0)),
            pl.BlockSpec((tm, half), lambda i: (i, 0)),
            pl.BlockSpec((tm, half), lambda i: (i, 0)),
            pl.BlockSpec((tm, half), lambda i: (i, 0)),
            pl.BlockSpec((tm, half), lambda i: (i, gate_diff_blk)),
            pl.BlockSpec(w_out.shape, lambda i: (0, 0)),
            pl.BlockSpec((1, d), lambda i: (0, 0)),
        ],
        out_specs=pl.BlockSpec((tm, d), lambda i: (i, 0)),
        out_shape=jax.ShapeDtypeStruct((m, d), F32),
        compiler_params=pltpu.CompilerParams(
            dimension_semantics=("arbitrary",),
            vmem_limit_bytes=VMEM_LIMIT),
        name="outproj",
    )(x2, o_mla, o_diff, big, big, w_out, g_post)


def kernel(x, positions, g_pre, w_in, g_q_a, w_q_b, g_kv_a, w_kv_b, lambda_q1, lambda_k1,
           lambda_q2, lambda_k2, g_diff_sub, w_out, g_post):
    batch, seq, d = x.shape
    depth = g_pre.shape[0]
    q_rank = w_q_b.shape[1]
    kv_rank = w_kv_b.shape[1]
    half_rope = MLA_ROPE // 2
    lat_end = q_rank + kv_rank
    pe_end = lat_end + MLA_ROPE

    pos_icol = positions.astype(jnp.int32).reshape(batch * seq, 1)
    pos_irow = positions.astype(jnp.int32).reshape(batch, 1, seq)
    freqs = 1.0 / (ROPE_THETA ** (jnp.arange(0, MLA_ROPE, 2, dtype=F32) / MLA_ROPE))
    freq = jnp.tile(freqs, 4).reshape(1, LANE)
    slopes = 2.0 ** (-8.0 * (jnp.arange(DIFF_HEADS, dtype=F32) + 1.0) / DIFF_HEADS)
    slopes_l2e = slopes * LOG2E

    mla_scale = MLA_QK ** -0.5 * LOG2E
    diff_scale = DIFF_HEAD_DIM ** -0.5 * LOG2E
    width = DIFF_HEADS * 2 * DIFF_HEAD_DIM

    x2 = x.reshape(batch * seq, d)
    for l in range(depth):
        lam_init = 0.8 - 0.6 * math.exp(-0.3 * l)
        w_lat, w_n, w_t = _regroup_w_in(w_in, l, lat_end, half_rope, width, rows=REGROUP_ROWS)
        wq = w_q_b[l]
        t1 = wq[:, :, MLA_NOPE:MLA_NOPE + half_rope]
        t2 = wq[:, :, MLA_NOPE + half_rope:]
        w_q = jnp.concatenate([wq[:, :, :MLA_NOPE], t1, t2, t2, t1], axis=-1)
        w_qt = w_q.reshape(q_rank, MLA_HEADS * QK_PAD).T.astype(BF16)
        wkv = w_kv_b[l]
        w_k = wkv[:, :, :MLA_NOPE].reshape(kv_rank, -1).astype(BF16)
        w_vt = wkv[:, :, MLA_NOPE:].reshape(kv_rank, -1).T.astype(BF16)
        gp = g_pre[l].reshape(1, d)

        h, big_n, dqt, dvt = _inproj(x2, gp, w_n, w_t, diff_scale, tm=PROJ_ROWS, tn=PROJ_COLS)
        qt, k, vt = _latent(h, pos_icol, freq, w_lat, g_q_a[l].reshape(1, -1), w_qt,
                            g_kv_a[l].reshape(1, -1), w_k, w_vt, tm=PROJ_ROWS, q_scale=mla_scale)
        o_mla, o_diff = _attention(
            qt, k, vt, big_n, dqt, dvt, pos_icol, pos_irow, slopes_l2e,
            lambda_q1[l].reshape(1, -1), lambda_k1[l].reshape(1, -1),
            lambda_q2[l].reshape(1, -1), lambda_k2[l].reshape(1, -1),
            g_diff_sub[l].reshape(1, -1), batch, seq, blk=ATTN_BLOCK, lam_init=lam_init)
        x2 = _outproj(x2, o_mla, o_diff, big_n, w_out[l].astype(BF16), g_post[l].reshape(1, d), tm=PROJ_ROWS)
    return x2.reshape(batch, seq, d)
```

```python
import functools
import math

import numpy as np
import jax
import jax.numpy as jnp
from jax import lax
from jax.experimental import pallas as pl
from jax.experimental.pallas import tpu as pltpu

F32 = jnp.float32
BF16 = jnp.bfloat16

EPS = 1e-6
LOG2E = 1.4426950408889634
ROPE_THETA = 10000.0

MLA_HEADS = 8
MLA_NOPE = 128
MLA_ROPE = 64
MLA_V = 128
VT_ROWS = MLA_V + 16
MLA_QK = MLA_NOPE + MLA_ROPE
DIFF_HEADS = 8
DIFF_HEAD_DIM = 64
LANE = 128
QK_PAD = 256
ALIBI_LANES = 9
ALIBI_BASE = (DIFF_HEAD_DIM, 0)

VMEM_LIMIT = 56 * 1024 * 1024

PROJ_ROWS = 512
PROJ_COLS = 512
ATTN_BLOCK = 512
REGROUP_ROWS = 256


def _rms(xf, g):
    ms = jnp.mean(xf * xf, axis=-1, keepdims=True)
    return xf * lax.rsqrt(ms + EPS) * g


def _dot(a, b):
    return jnp.dot(a, b, preferred_element_type=F32)


def _dot_nt(a, b):
    return lax.dot_general(a, b, (((1,), (1,)), ((), ())), preferred_element_type=F32)


def _store_vt_ext(vt_ref, head, vt_head):
    r = head * VT_ROWS
    vt_ref[r:r + MLA_V, :] = vt_head
    vt_ref[r + MLA_V:r + VT_ROWS, :] = jnp.ones((VT_ROWS - MLA_V, vt_head.shape[1]), vt_head.dtype)


def _regroup_body(wt_ref, lat_ref, n_ref, t_ref, *, lat_end, half_rope, width):
    pe_end = lat_end + 2 * half_rope
    cols = lambda a, b: wt_ref[0, a:b, :]
    u1 = cols(lat_end, lat_end + half_rope)
    u2 = cols(lat_end + half_rope, pe_end)
    lat_ref[:, :lat_end] = cols(0, lat_end).T.astype(BF16)
    lat_ref[:, lat_end:] = jnp.concatenate([u1, u1, u2, u2, u2, u2, u1, u1], axis=0).T.astype(BF16)
    group = lambda n: cols(pe_end + n * width, pe_end + (n + 1) * width).T.astype(BF16)
    for dst, src in enumerate((0, 2, 4)):
        n_ref[:, dst * width:(dst + 1) * width] = group(src)
    for dst, src in enumerate((1, 3)):
        t_ref[:, dst * width:(dst + 1) * width] = group(src)


def _regroup_w_in(w_in, layer, lat_end, half_rope, width, rows):
    _, d, n_in = w_in.shape
    n_lat = lat_end + 8 * half_rope
    w_in_t = jnp.swapaxes(w_in, 1, 2)
    return pl.pallas_call(
        functools.partial(_regroup_body, lat_end=lat_end, half_rope=half_rope, width=width),
        grid=(d // rows,),
        in_specs=[pl.BlockSpec((1, n_in, rows), lambda i: (layer, 0, i))],
        out_specs=[
            pl.BlockSpec((rows, n_lat), lambda i: (i, 0)),
            pl.BlockSpec((rows, 3 * width), lambda i: (i, 0)),
            pl.BlockSpec((rows, 2 * width), lambda i: (i, 0)),
        ],
        out_shape=[
            jax.ShapeDtypeStruct((d, n_lat), BF16),
            jax.ShapeDtypeStruct((d, 3 * width), BF16),
            jax.ShapeDtypeStruct((d, 2 * width), BF16),
        ],
        compiler_params=pltpu.CompilerParams(
            dimension_semantics=("arbitrary",),
            vmem_limit_bytes=VMEM_LIMIT),
        name="regroup_w_in",
    )(w_in_t)


def _inproj_body(x_ref, g_ref, wn_ref, wt_ref, h_ref, on_ref, oq_ref, ov_ref, *, tn, q_scale):
    h = _rms(x_ref[...], g_ref[...]).astype(BF16)
    h_ref[...] = h
    for c in range(wn_ref.shape[1] // tn):
        cols = slice(c * tn, (c + 1) * tn)
        on_ref[:, cols] = _dot(h, wn_ref[:, cols]).astype(BF16)
    n_q = oq_ref.shape[0] // tn
    heads_per_chunk = tn // MLA_V
    for c in range(wt_ref.shape[1] // tn):
        cols = slice(c * tn, (c + 1) * tn)
        acc = _dot(h, wt_ref[:, cols])
        if c < n_q:
            oq_ref[cols, :] = (acc * q_scale).T.astype(BF16)
        else:
            vt = acc.T.astype(BF16)
            for j in range(heads_per_chunk):
                _store_vt_ext(ov_ref, (c - n_q) * heads_per_chunk + j, vt[j * MLA_V:(j + 1) * MLA_V, :])


def _inproj(x2, g_pre, w_n, w_t, q_scale, tm, tn):
    m, d = x2.shape
    width = w_t.shape[1] // 2
    nv = width // MLA_V * VT_ROWS
    resident = lambda a: pl.BlockSpec(a.shape, lambda i: (0,) * a.ndim, pipeline_mode=pl.Buffered(1))
    return pl.pallas_call(
        functools.partial(_inproj_body, tn=tn, q_scale=q_scale),
        grid=(m // tm,),
        in_specs=[
            pl.BlockSpec((tm, d), lambda i: (i, 0)),
            pl.BlockSpec((1, d), lambda i: (0, 0)),
            resident(w_n),
            resident(w_t),
        ],
        out_specs=[
            pl.BlockSpec((tm, d), lambda i: (i, 0)),
            pl.BlockSpec((tm, w_n.shape[1]), lambda i: (i, 0)),
            pl.BlockSpec((width, tm), lambda i: (0, i)),
            pl.BlockSpec((nv, tm), lambda i: (0, i)),
        ],
        out_shape=[
            jax.ShapeDtypeStruct((m, d), BF16),
            jax.ShapeDtypeStruct((m, w_n.shape[1]), BF16),
            jax.ShapeDtypeStruct((width, m), BF16),
            jax.ShapeDtypeStruct((nv, m), BF16),
        ],
        compiler_params=pltpu.CompilerParams(
            dimension_semantics=("arbitrary",),
            vmem_limit_bytes=VMEM_LIMIT),
        name="inproj",
    )(x2, g_pre, w_n, w_t)


def _latent_body(h_ref, pos_ref, freq_ref, wlat_ref, gq_ref, wqt_ref,
                 gkv_ref, wk_ref, wvt_ref, qt_ref, k_ref, vt_ref, *, q_scale):
    lat = _dot(h_ref[...], wlat_ref[...])
    ang = pos_ref[...].astype(F32) * freq_ref[...]
    cos = jnp.cos(ang)
    sin = jnp.sin(ang)
    seg = lax.broadcasted_iota(jnp.int32, (1, LANE), 1) // (MLA_ROPE // 2)
    fq = jnp.where(seg == 1, -sin, jnp.where(seg == 3, sin, cos)) * q_scale
    sk = jnp.where(seg < 2, -sin, sin)
    k_ext = (lat[:, 768:896] * cos + lat[:, 896:1024] * sk).astype(BF16)

    c_q = _rms(lat[:, :512], gq_ref[...]).astype(BF16)
    qft = _dot_nt(wqt_ref[...], c_q)
    fqt = fq.T
    c_kv = _rms(lat[:, 512:768], gkv_ref[...]).astype(BF16)
    kf = _dot(c_kv, wk_ref[...])
    vt = _dot_nt(wvt_ref[...], c_kv).astype(BF16)
    for hd in range(MLA_HEADS):
        _store_vt_ext(vt_ref, hd, vt[hd * MLA_V:(hd + 1) * MLA_V, :])
        o = hd * QK_PAD
        qt_ref[o:o + LANE, :] = (qft[o:o + LANE, :] * q_scale).astype(BF16)
        qt_ref[o + LANE:o + QK_PAD, :] = (qft[o + LANE:o + QK_PAD, :] * fqt).astype(BF16)
        k_ref[:, o:o + LANE] = kf[:, hd * LANE:(hd + 1) * LANE].astype(BF16)
        k_ref[:, o + LANE:o + QK_PAD] = k_ext


def _latent(h, pos_col, freq, w_lat, g_q, w_qt, g_kv, w_k, w_vt, tm, q_scale):
    m, d = h.shape
    nq = MLA_HEADS * QK_PAD
    nv = MLA_HEADS * VT_ROWS
    full = lambda a: pl.BlockSpec(a.shape, lambda i: (0,) * a.ndim)
    return pl.pallas_call(
        functools.partial(_latent_body, q_scale=q_scale),
        grid=(m // tm,),
        in_specs=[
            pl.BlockSpec((tm, d), lambda i: (i, 0)),
            pl.BlockSpec((tm, 1), lambda i: (i, 0)),
            full(freq), full(w_lat), full(g_q), full(w_qt), full(g_kv), full(w_k),
            full(w_vt),
        ],
        out_specs=[
            pl.BlockSpec((nq, tm), lambda i: (0, i)),
            pl.BlockSpec((tm, nq), lambda i: (i, 0)),
            pl.BlockSpec((nv, tm), lambda i: (0, i)),
        ],
        out_shape=[
            jax.ShapeDtypeStruct((nq, m), BF16),
            jax.ShapeDtypeStruct((m, nq), BF16),
            jax.ShapeDtypeStruct((nv, m), BF16),
        ],
        compiler_params=pltpu.CompilerParams(
            dimension_semantics=("arbitrary",),
            vmem_limit_bytes=VMEM_LIMIT),
        name="latent",
    )(h, pos_col, freq, w_lat, g_q, w_qt, g_kv, w_k, w_vt)


def _blk_slice(idx, blk):
    if isinstance(idx, int):
        return slice(idx * blk, (idx + 1) * blk)
    return pl.ds(pl.multiple_of(idx * blk, blk), blk)


def _offdiag_pairs(nblk):
    return [(i, t) for i in range(1, nblk) for t in range(i)]


def _pair_table(nblk):
    return jnp.asarray(np.array(_offdiag_pairs(nblk), dtype=np.int32).T)


def _half_slices(idx, blk):
    half = blk // 2
    if isinstance(idx, int):
        return slice(idx * blk, idx * blk + half), slice(idx * blk + half, (idx + 1) * blk)
    start = pl.multiple_of(idx * blk, blk)
    return pl.ds(start, half), pl.ds(pl.multiple_of(start + half, half), half)


def _init_mask_bias(mb_ref):
    kv = lax.broadcasted_iota(jnp.int32, mb_ref.shape, 0)
    qi = lax.broadcasted_iota(jnp.int32, mb_ref.shape, 1)
    mb_ref[...] = jnp.where(kv <= qi, 0.0, -jnp.inf).astype(F32)


def _diag_softmax(s_top, s_bot, mb):
    h = s_bot.shape[0]
    a0 = s_top[:, :h] + mb
    a1 = s_top[:, h:]
    b1 = s_bot + mb
    m0 = jnp.max(a0, axis=0, keepdims=True)
    m1 = jnp.maximum(jnp.max(a1, axis=0, keepdims=True), jnp.max(b1, axis=0, keepdims=True))
    p0 = jnp.exp2(a0 - m0)
    pa1 = jnp.exp2(a1 - m1)
    pb1 = jnp.exp2(b1 - m1)
    return jnp.concatenate([m0, m1], axis=1), jnp.concatenate([p0, pa1], axis=1), pb1


def _diag_pv(vt_top, vt_bot, p_top, p_bot):
    h = p_bot.shape[0]
    a = _dot(vt_top, p_top)
    b = _dot(vt_bot, p_bot)
    return jnp.concatenate([a[:, :h], a[:, h:] + b], axis=1)


def _run_pipeline(heads, tab_ref, nblk):
    A, A_DIAG, B_DIAG, C_DIAG, B, C = range(6)
    pairs = _offdiag_pairs(nblk)
    npairs = len(pairs)
    assert (nblk - 2) % 2 == 0 and (npairs - 2) % 2 == 0

    def step(c=None, b=None, a=None):
        for call in (c, b, a):
            if call is not None:
                for stages in heads:
                    stages[call[0]](*call[1])

    step(a=(A_DIAG, (0,)))
    step(b=(B_DIAG, (0,)), a=(A_DIAG, (1,)))

    def diag_loop(u, carry):
        for d in (1, 2):
            g = d + 2 * u
            step(c=(C_DIAG, (g - 1,)), b=(B_DIAG, (g,)), a=(A_DIAG, (g + 1,)))
        return carry

    lax.fori_loop(0, (nblk - 2) // 2, diag_loop, 0)

    step(c=(C_DIAG, (nblk - 2,)), b=(B_DIAG, (nblk - 1,)), a=(A, pairs[0]))
    step(c=(C_DIAG, (nblk - 1,)), b=(B, pairs[0]), a=(A, pairs[1]))

    def off_loop(u, carry):
        pair = lambda f: (tab_ref[0, f], tab_ref[1, f])
        for d in (1, 2):
            f = d + 2 * u
            step(c=(C, pair(f - 1)), b=(B, pair(f)), a=(A, pair(f + 1)))
        return carry

    lax.fori_loop(0, (npairs - 2) // 2, off_loop, 0)

    step(c=(C, pairs[npairs - 2]), b=(B, pairs[npairs - 1]))
    step(c=(C, pairs[npairs - 1]))


def _mla_stages(qt_ref, k_ref, vt_ref, o_ref, mb_ref, s_ref, p_ref, al_ref, m_ref, acc_ref, blk, nblk):
    half = blk // 2

    def stage_a(i, t):
        s_ref[...] = _dot(k_ref[_blk_slice(t, blk), :], qt_ref[:, _blk_slice(i, blk)])

    def stage_a_diag(i):
        top, bot = _half_slices(i, blk)
        s_ref[:half, :] = _dot(k_ref[top, :], qt_ref[:, _blk_slice(i, blk)])
        s_ref[half:, half:] = _dot(k_ref[bot, :], qt_ref[:, bot])

    def stage_b_diag(i):
        m, p_top, p_bot = _diag_softmax(s_ref[:half, :], s_ref[half:, half:], mb_ref[...])
        m_ref[i] = m
        p_ref[:half, :] = p_top.astype(BF16)
        p_ref[half:, half:] = p_bot.astype(BF16)

    def stage_c_diag(i):
        top, bot = _half_slices(i, blk)
        acc_ref[i] = _diag_pv(vt_ref[:, top], vt_ref[:, bot], p_ref[:half, :], p_ref[half:, half:])

    def stage_b(i, t):
        s = s_ref[...]
        m_prev = m_ref[i]
        m_new = jnp.maximum(m_prev, jnp.max(s, axis=0, keepdims=True))
        alpha = jnp.exp2(m_prev - m_new)
        p = jnp.exp2(s - m_new)
        m_ref[i] = m_new
        al_ref[...] = alpha
        p_ref[...] = p.astype(BF16)

    def stage_c(i, t):
        acc_ref[i] = al_ref[...] * acc_ref[i] + _dot(vt_ref[:, _blk_slice(t, blk)], p_ref[...])

    def finalize():
        for i in range(nblk):
            acc = acc_ref[i]
            o_t = acc[:MLA_V] * (1.0 / acc[MLA_V:MLA_V + 1])
            o_ref[_blk_slice(i, blk), :] = o_t.T.astype(BF16)

    return (stage_a, stage_a_diag, stage_b_diag, stage_c_diag, stage_b, stage_c), finalize


def _mla_scratch(blk, nblk):
    return [
        pltpu.VMEM((blk, blk), F32),
        pltpu.VMEM((blk, blk), BF16),
        pltpu.VMEM((1, blk), F32),
        pltpu.VMEM((nblk, 1, blk), F32),
        pltpu.VMEM((nblk, VT_ROWS, blk), F32),
    ]


def _alibi_lane_tables(slopes_l2e):
    c1 = slopes_l2e.astype(BF16).astype(F32)
    c2 = (slopes_l2e - c1).astype(BF16).astype(F32)
    c3 = (slopes_l2e - c1 - c2).astype(BF16).astype(F32)
    cw = (jnp.stack([c1, c2, c3], axis=1)[:, :, None]
          * jnp.asarray([1.0, 256.0, 65536.0], F32)[None, None, :]).reshape(-1, ALIBI_LANES)
    heads = slopes_l2e.shape[0]
    ktab = jnp.zeros((heads, 2, LANE), F32)
    for x, base in enumerate(ALIBI_BASE):
        ktab = ktab.at[:, x, base:base + ALIBI_LANES].set(-cw)
    qcol = jnp.zeros((heads, DIFF_HEAD_DIM), F32).at[:, ALIBI_LANES:2 * ALIBI_LANES].set(cw)
    return ktab.astype(BF16), qcol.reshape(heads, DIFF_HEAD_DIM, 1)


def _diff_stages(qt_ref, k_ref, vt_ref, pc_ref, pr_ref, shr_ref, shc_ref, kt_ref, qc_ref,
                 lq1_ref, lk1_ref, lq2_ref, lk2_ref, gsub_ref, o_ref, mb_ref, dig_ref, digt_ref,
                 qm_ref, km_ref, s_ref, p_ref, al_ref, m_ref, acc_ref, blk, nblk, lam_init):
    half = DIFF_HEAD_DIM

    @pl.when(pl.program_id(1) == 0)
    def _():
        pcol = pc_ref[...]
        prel = jnp.broadcast_to(pcol - pcol[0:1, :], dig_ref.shape)
        shift = jnp.broadcast_to(shr_ref[...], dig_ref.shape)
        dig_ref[...] = (lax.shift_right_logical(prel, shift) & 255).astype(F32).astype(BF16)
        prow = pr_ref[0]
        prel_t = jnp.broadcast_to(prow - prow[:, 0:1], digt_ref.shape)
        shift_t = jnp.broadcast_to(shc_ref[...], digt_ref.shape)
        digt_ref[...] = (lax.shift_right_logical(prel_t, shift_t) & 255).astype(F32).astype(BF16)

    row = lax.broadcasted_iota(jnp.int32, (half, 1), 0)
    alibi_q = jnp.where(row < ALIBI_LANES, digt_ref[...],
                        jnp.broadcast_to(qc_ref[0], digt_ref.shape).astype(BF16))
    qm_ref[0, :half, :] = qt_ref[:half, :]
    qm_ref[0, half:, :] = alibi_q
    qm_ref[1, :half, :] = alibi_q
    qm_ref[1, half:, :] = qt_ref[half:, :]
    lane = lax.broadcasted_iota(jnp.int32, (1, LANE), 1)
    dig = dig_ref[...]
    k = k_ref[...]
    for x, base in enumerate(ALIBI_BASE):
        own = (lane < half) if x == 0 else (lane >= half)
        k_digit = (lane >= base + ALIBI_LANES) & (lane < base + 2 * ALIBI_LANES)
        km_ref[x] = jnp.where(own, k, jnp.where(k_digit, dig, kt_ref[0, x:x + 1, :]))

    def stage_a(i, t):
        for x in range(2):
            s_ref[x] = _dot(km_ref[x, _blk_slice(t, blk), :], qm_ref[x, :, _blk_slice(i, blk)])

    hb = blk // 2

    def stage_a_diag(i):
        top, bot = _half_slices(i, blk)
        for x in range(2):
            s_ref[x, :hb, :] = _dot(km_ref[x, top, :], qm_ref[x, :, _blk_slice(i, blk)])
            s_ref[x, hb:, hb:] = _dot(km_ref[x, bot, :], qm_ref[x, :, bot])

    def stage_b_diag(i):
        for x in range(2):
            m, p_top, p_bot = _diag_softmax(s_ref[x, :hb, :], s_ref[x, hb:, hb:], mb_ref[...])
            m_ref[x, i] = m
            p_ref[x, :hb, :] = p_top.astype(BF16)
            p_ref[x, hb:, hb:] = p_bot.astype(BF16)

    def stage_c_diag(i):
        top, bot = _half_slices(i, blk)
        for x in range(2):
            acc_ref[x, i] = _diag_pv(vt_ref[:, top], vt_ref[:, bot], p_ref[x, :hb, :], p_ref[x, hb:, hb:])

    def stage_b(i, t):
        for x in range(2):
            s = s_ref[x]
            m_prev = m_ref[x, i]
            m_new = jnp.maximum(m_prev, jnp.max(s, axis=0, keepdims=True))
            alpha = jnp.exp2(m_prev - m_new)
            p = jnp.exp2(s - m_new)
            m_ref[x, i] = m_new
            al_ref[x] = alpha
            p_ref[x] = p.astype(BF16)

    def stage_c(i, t):
        vt = vt_ref[:, _blk_slice(t, blk)]
        for x in range(2):
            acc_ref[x, i] = al_ref[x] * acc_ref[x, i] + _dot(vt, p_ref[x])

    def finalize():
        lam = (jnp.exp(jnp.sum(lq1_ref[...] * lk1_ref[...], axis=-1, keepdims=True))
               - jnp.exp(jnp.sum(lq2_ref[...] * lk2_ref[...], axis=-1, keepdims=True))
               + lam_init)
        for i in range(nblk):
            a1, a2 = acc_ref[0, i], acc_ref[1, i]
            o_t = (a1[:MLA_V] * (1.0 / a1[MLA_V:MLA_V + 1])
                   - lam * (a2[:MLA_V] * (1.0 / a2[MLA_V:MLA_V + 1])))
            o = o_t.T
            o_ref[_blk_slice(i, blk), :] = (_rms(o, gsub_ref[...]) * (1.0 - lam_init)).astype(BF16)

    return (stage_a, stage_a_diag, stage_b_diag, stage_c_diag, stage_b, stage_c), finalize


def _diff_scratch(seq, blk, nblk):
    return [
        pltpu.VMEM((seq, LANE), BF16),
        pltpu.VMEM((DIFF_HEAD_DIM, seq), BF16),
        pltpu.VMEM((2, LANE, seq), BF16),
        pltpu.VMEM((2, seq, LANE), BF16),
        pltpu.VMEM((2, blk, blk), F32),
        pltpu.VMEM((2, blk, blk), BF16),
        pltpu.VMEM((2, 1, blk), F32),
        pltpu.VMEM((2, nblk, 1, blk), F32),
        pltpu.VMEM((2, nblk, VT_ROWS, blk), F32),
    ]


N_MLA_IN, N_DIFF_IN = 3, 14
N_MLA_SCRATCH, N_DIFF_SCRATCH = 5, 9


def _attention_body(tab_ref, *refs, blk, nblk, lam_init):
    mla_in, refs = refs[:N_MLA_IN], refs[N_MLA_IN:]
    diff_in, refs = refs[:N_DIFF_IN], refs[N_DIFF_IN:]
    (o_mla_ref, o_diff_ref, mb_ref), refs = refs[:3], refs[3:]
    mla_scr, diff_scr = refs[:N_MLA_SCRATCH], refs[N_MLA_SCRATCH:]
    assert len(diff_scr) == N_DIFF_SCRATCH

    @pl.when((pl.program_id(0) == 0) & (pl.program_id(1) == 0))
    def _():
        _init_mask_bias(mb_ref)

    mla, mla_fin = _mla_stages(*mla_in, o_mla_ref, mb_ref, *mla_scr, blk, nblk)
    diff, diff_fin = _diff_stages(*diff_in, o_diff_ref, mb_ref, *diff_scr, blk, nblk, lam_init)

    _run_pipeline([mla, diff], tab_ref, nblk)
    mla_fin()
    diff_fin()


def _attention(qt, k, vt, big_n, dqt, dvt, pos_icol, pos_irow, slopes_l2e, lq1, lk1, lq2, lk2, g_sub,
               batch, seq, blk, lam_init):
    assert MLA_HEADS == DIFF_HEADS
    nblk = seq // blk
    assert nblk >= 3
    width = DIFF_HEADS * 2 * DIFF_HEAD_DIM
    cb = width // LANE
    k_off = 1 * cb
    k_tab, q_col = _alibi_lane_tables(slopes_l2e)
    digit_k = np.arange(LANE) % DIFF_HEAD_DIM % 3
    shift_row = jnp.asarray((digit_k * 8).reshape(1, LANE), jnp.int32)
    shift_col = jnp.asarray((digit_k[:DIFF_HEAD_DIM] * 8).reshape(DIFF_HEAD_DIM, 1), jnp.int32)
    small = lambda a: pl.BlockSpec(a.shape, lambda b, h: (0,) * a.ndim)
    mla_specs = [
        pl.BlockSpec((QK_PAD, seq), lambda b, h: (h, b)),
        pl.BlockSpec((seq, QK_PAD), lambda b, h: (b, h)),
        pl.BlockSpec((VT_ROWS, seq), lambda b, h: (h, b)),
    ]
    diff_specs = [
        pl.BlockSpec((LANE, seq), lambda b, h: (h, b)),
        pl.BlockSpec((seq, LANE), lambda b, h: (b, k_off + h)),
        pl.BlockSpec((VT_ROWS, seq), lambda b, h: (h, b)),
        pl.BlockSpec((seq, 1), lambda b, h: (b, 0)),
        pl.BlockSpec((1, 1, seq), lambda b, h: (b, 0, 0)),
        small(shift_row), small(shift_col),
        pl.BlockSpec((1, 2, LANE), lambda b, h: (h, 0, 0)),
        pl.BlockSpec((1, DIFF_HEAD_DIM, 1), lambda b, h: (h, 0, 0)),
        small(lq1), small(lk1), small(lq2), small(lk2), small(g_sub),
    ]
    assert len(mla_specs) == N_MLA_IN and len(diff_specs) == N_DIFF_IN
    out_spec = pl.BlockSpec((seq, LANE), lambda b, h: (b, h))
    return pl.pallas_call(
        functools.partial(_attention_body, blk=blk, nblk=nblk, lam_init=lam_init),
        grid=(batch, MLA_HEADS),
        in_specs=[pl.BlockSpec(memory_space=pltpu.SMEM)] + mla_specs + diff_specs,
        out_specs=[out_spec, out_spec],
        out_shape=[jax.ShapeDtypeStruct((batch * seq, MLA_HEADS * MLA_V), BF16),
                   jax.ShapeDtypeStruct((batch * seq, width), BF16)],
        scratch_shapes=([pltpu.VMEM((blk // 2, blk // 2), F32)]
                        + _mla_scratch(blk, nblk) + _diff_scratch(seq, blk, nblk)),
        compiler_params=pltpu.CompilerParams(
            dimension_semantics=("arbitrary", "arbitrary"),
            vmem_limit_bytes=VMEM_LIMIT),
        name="attention",
    )(_pair_table(nblk), qt, k, vt, dqt, big_n, dvt, pos_icol, pos_irow, shift_row, shift_col,
      k_tab, q_col, lq1, lk1, lq2, lk2, g_sub)


def _outproj_body(x_ref, om_ref, od_ref, gm_ref, gd_ref, wf_ref, gpost_ref, o_ref, w_ref):
    @pl.when(pl.program_id(0) == 0)
    def _():
        rows = 256
        def cast(c, carry):
            r = pl.ds(pl.multiple_of(c * rows, rows), rows)
            w_ref[r, :] = wf_ref[r, :].astype(BF16)
            return carry
        lax.fori_loop(0, wf_ref.shape[0] // rows, cast, 0)

    gm = gm_ref[...].astype(F32)
    gd = gd_ref[...].astype(F32)
    mm = (om_ref[...].astype(F32) * (gm * jax.nn.sigmoid(gm))).astype(BF16)
    md = (od_ref[...].astype(F32) * (gd * jax.nn.sigmoid(gd))).astype(BF16)
    half = mm.shape[1]
    d = o_ref.shape[1]
    ss = jnp.zeros((mm.shape[0], 1), F32)
    tn = 2 * PROJ_COLS
    for c in range(d // tn):
        cols = slice(c * tn, (c + 1) * tn)
        y = _dot(mm, w_ref[:half, cols]) + _dot(md, w_ref[half:, cols])
        ss = ss + jnp.sum(y * y, axis=-1, keepdims=True)
        o_ref[:, cols] = y
    inv = lax.rsqrt(ss * (1.0 / d) + EPS)
    o_ref[...] = x_ref[...] + o_ref[...] * inv * gpost_ref[...]


def _outproj(x2, o_mla, o_diff, big, w_out, g_post, tm):
    m, d = x2.shape
    half = o_mla.shape[1]
    gate_diff_blk = (big.shape[1] - half) // half
    return pl.pallas_call(
        _outproj_body,
        grid=(m // tm,),
        in_specs=[
            pl.BlockSpec((tm, d), lambda i: (i, 0)),
            pl.BlockSpec((tm, half), lambda i: (i, 0)),
            pl.BlockSpec((tm, half), lambda i: (i, 0)),
            pl.BlockSpec((tm, half), lambda i: (i, 0)),
            pl.BlockSpec((tm, half), lambda i: (i, gate_diff_blk)),
            pl.BlockSpec(w_out.shape, lambda i: (0, 0), pipeline_mode=pl.Buffered(1)),
            pl.BlockSpec((1, d), lambda i: (0, 0)),
        ],
        out_specs=pl.BlockSpec((tm, d), lambda i: (i, 0)),
        out_shape=jax.ShapeDtypeStruct((m, d), F32),
        scratch_shapes=[pltpu.VMEM(w_out.shape, BF16)],
        compiler_params=pltpu.CompilerParams(
            dimension_semantics=("arbitrary",),
            vmem_limit_bytes=VMEM_LIMIT),
        name="outproj",
    )(x2, o_mla, o_diff, big, big, w_out, g_post)


def kernel(x, positions, g_pre, w_in, g_q_a, w_q_b, g_kv_a, w_kv_b, lambda_q1, lambda_k1,
           lambda_q2, lambda_k2, g_diff_sub, w_out, g_post):
    batch, seq, d = x.shape
    depth = g_pre.shape[0]
    q_rank = w_q_b.shape[1]
    kv_rank = w_kv_b.shape[1]
    half_rope = MLA_ROPE // 2
    lat_end = q_rank + kv_rank
    pe_end = lat_end + MLA_ROPE

    pos_icol = positions.astype(jnp.int32).reshape(batch * seq, 1)
    pos_irow = positions.astype(jnp.int32).reshape(batch, 1, seq)
    freqs = 1.0 / (ROPE_THETA ** (jnp.arange(0, MLA_ROPE, 2, dtype=F32) / MLA_ROPE))
    freq = jnp.tile(freqs, 4).reshape(1, LANE)
    slopes = 2.0 ** (-8.0 * (jnp.arange(DIFF_HEADS, dtype=F32) + 1.0) / DIFF_HEADS)
    slopes_l2e = slopes * LOG2E

    mla_scale = MLA_QK ** -0.5 * LOG2E
    diff_scale = DIFF_HEAD_DIM ** -0.5 * LOG2E
    width = DIFF_HEADS * 2 * DIFF_HEAD_DIM

    x2 = x.reshape(batch * seq, d)
    for l in range(depth):
        lam_init = 0.8 - 0.6 * math.exp(-0.3 * l)
        w_lat, w_n, w_t = _regroup_w_in(w_in, l, lat_end, half_rope, width, rows=REGROUP_ROWS)
        wq = w_q_b[l]
        t1 = wq[:, :, MLA_NOPE:MLA_NOPE + half_rope]
        t2 = wq[:, :, MLA_NOPE + half_rope:]
        w_q = jnp.concatenate([wq[:, :, :MLA_NOPE], t1, t2, t2, t1], axis=-1)
        w_qt = w_q.reshape(q_rank, MLA_HEADS * QK_PAD).T.astype(BF16)
        wkv = w_kv_b[l]
        w_k = wkv[:, :, :MLA_NOPE].reshape(kv_rank, -1).astype(BF16)
        w_vt = wkv[:, :, MLA_NOPE:].reshape(kv_rank, -1).T.astype(BF16)
        gp = g_pre[l].reshape(1, d)

        h, big_n, dqt, dvt = _inproj(x2, gp, w_n, w_t, diff_scale, tm=PROJ_ROWS, tn=PROJ_COLS)
        qt, k, vt = _latent(h, pos_icol, freq, w_lat, g_q_a[l].reshape(1, -1), w_qt,
                            g_kv_a[l].reshape(1, -1), w_k, w_vt, tm=PROJ_ROWS, q_scale=mla_scale)
        o_mla, o_diff = _attention(
            qt, k, vt, big_n, dqt, dvt, pos_icol, pos_irow, slopes_l2e,
            lambda_q1[l].reshape(1, -1), lambda_k1[l].reshape(1, -1),
            lambda_q2[l].reshape(1, -1), lambda_k2[l].reshape(1, -1),
            g_diff_sub[l].reshape(1, -1), batch, seq, blk=ATTN_BLOCK, lam_init=lam_init)
        x2 = _outproj(x2, o_mla, o_diff, big_n, w_out[l], g_post[l].reshape(1, d), tm=PROJ_ROWS)
    return x2.reshape(batch, seq, d)
```
